```python
import math
import jax
import jax.numpy as jnp
from jax import lax
import numpy as np

D_MODEL = 1024
BATCH = 4
SEQ = 4096
DEPTH = 2
DEC_BATCH = 32
DEC_SEQ = 8
PAST_LEN = 8192
PAGE_SIZE = 128

HEAD_DIM = 64
GLA_HEADS = D_MODEL // 256
GLA_DV = 64
GLA_DK = 32
GLA_RANK = 16
GLA_TAU = 16.0
HGRN_HEADS = D_MODEL // 256
HGRN_DK = 64
HGRN_DV = 64
ATT_HEADS = D_MODEL // 128
DILATED_GROUPS = ((128, 1), (512, 4), (2048, 16))
WINDOW_MAX = 2048
ROT_DIM = HEAD_DIM // 4
ROPE_THETA = 500000.0
CHUNK = 64
D_FF = 2816
N_EXPERTS = 8
TOP_K = 2
D_FF_EXPERT = 1408
EPS = 1e-6

MIX_WIDTH = GLA_HEADS * GLA_DV + HGRN_HEADS * HGRN_DV + ATT_HEADS * HEAD_DIM
IN_SPLITS = (
    GLA_HEADS * GLA_DK,
    GLA_HEADS * GLA_DK,
    GLA_HEADS * GLA_DV,
    GLA_RANK,
    GLA_HEADS * GLA_DV,
    HGRN_HEADS * HGRN_DK,
    HGRN_HEADS * HGRN_DK,
    HGRN_HEADS * HGRN_DV,
    HGRN_HEADS * HGRN_DV,
    ATT_HEADS * HEAD_DIM,
    ATT_HEADS * HEAD_DIM,
    ATT_HEADS * HEAD_DIM,
)
N_IN = sum(IN_SPLITS)
N_DENSE = (DEPTH + 1) // 2
N_MOE = DEPTH // 2

kernel_name = 'hymba_gla_hgrn2_dilated_decoder_step'


def rmsnorm(x, g):
    xf = x.astype(jnp.float32)
    y = xf * lax.rsqrt(jnp.mean(xf * xf, axis=-1, keepdims=True) + EPS)
    return (y * g.astype(jnp.float32)).astype(x.dtype)


def rope_partial(x, pos):
    half = ROT_DIM // 2
    inv_freq = ROPE_THETA ** (-jnp.arange(0, ROT_DIM, 2, dtype=jnp.float32) / ROT_DIM)
    ang = pos.astype(jnp.float32)[:, None] * inv_freq[None, :]
    cos = jnp.cos(ang)[None, :, None, :]
    sin = jnp.sin(ang)[None, :, None, :]
    xf = x.astype(jnp.float32)
    x1, x2, rest = xf[..., :half], xf[..., half:ROT_DIM], xf[..., ROT_DIM:]
    return jnp.concatenate([x1 * cos - x2 * sin, x2 * cos + x1 * sin, rest], axis=-1).astype(x.dtype)


def gated_linear_chunked(q, k, v, log_a, s0, chunk):
    B, L, H, K = q.shape
    V = v.shape[-1]
    n = L // chunk

    def blocks(t):
        return t.astype(jnp.float32).reshape(B, n, chunk, H, t.shape[-1]).transpose(1, 0, 2, 3, 4)

    causal = jnp.tril(jnp.ones((chunk, chunk), dtype=bool))[None, :, :, None, None]

    def step(S, inp):
        qc, kc, vc, gc = inp
        b = jnp.cumsum(gc, axis=1)
        decay = jnp.exp(jnp.where(causal, b[:, :, None] - b[:, None, :], -jnp.inf))
        A = jnp.einsum('btshk,bshk->btsh', decay * qc[:, :, None], kc)
        o = (jnp.einsum('btsh,bshv->bthv', A, vc)
             + jnp.einsum('bthk,bhkv->bthv', qc * jnp.exp(b), S))
        b_last = b[:, -1]
        S = (jnp.exp(b_last)[..., None] * S
             + jnp.einsum('bshk,bshv->bhkv', kc * jnp.exp(b_last[:, None] - b), vc))
        return S, o

    S, o = lax.scan(step, s0.astype(jnp.float32), (blocks(q), blocks(k), blocks(v), blocks(log_a)))
    o = o.transpose(1, 0, 2, 3, 4).reshape(B, L, H, V)
    return o, S


def combine_dilations(parts):
    m_all = jnp.stack([p[0] for p in parts])
    m_top = jnp.max(m_all, axis=0)
    w = jnp.exp(m_all - m_top)
    den = jnp.sum(w * jnp.stack([p[1] for p in parts]), axis=0)
    num = jnp.sum(w[..., None] * jnp.stack([p[2] for p in parts]), axis=0)
    return num / den[..., None]


def dilated_attention_prompt(q, k, v):
    B, S, H, Dh = q.shape
    scale = Dh ** -0.5
    parts = []
    for window, dil in DILATED_GROUPS:
        band = window // dil
        L = S // dil
        nb = -(-L // band)
        Lp = nb * band

        def residues(t):
            t = t.reshape(B, L, dil, H, Dh).transpose(0, 2, 1, 3, 4)
            t = jnp.pad(t, ((0, 0), (0, 0), (0, Lp - L), (0, 0), (0, 0)))
            return t.reshape(B, dil, nb, band, H, Dh)

        def with_prev(t):
            prev = jnp.concatenate([jnp.zeros_like(t[:, :, :1]), t[:, :, :-1]], axis=2)
            return jnp.concatenate([prev, t], axis=3)

        qr = residues(q)
        kk = with_prev(residues(k))
        vv = with_prev(residues(v))
        s = jnp.einsum('brnqhd,brnkhd->brnhqk', qr, kk).astype(jnp.float32) * scale
        qi = jnp.arange(band)[:, None]
        ki = jnp.arange(2 * band)[None, :]
        rel = band + qi - ki
        valid = ((rel >= 0) & (rel <= band))[None] & ((jnp.arange(nb)[:, None, None] > 0) | (ki >= band)[None])
        s = jnp.where(valid[:, None], s, -jnp.inf)
        m = jnp.max(s, axis=-1)
        p = jnp.exp(s - m[..., None])
        den = jnp.sum(p, axis=-1)
        num = jnp.einsum('brnhqk,brnkhd->brnqhd', p, vv.astype(jnp.float32))

        def back(t):
            t = t.reshape((B, dil, Lp) + t.shape[4:])[:, :, :L]
            t = jnp.moveaxis(t, 1, 2)
            return t.reshape((B, S) + t.shape[3:])

        parts.append((back(jnp.moveaxis(m, 3, 4)), back(jnp.moveaxis(den, 3, 4)), back(num)))
    return combine_dilations(parts)


def dilated_attention_sample(q, k, v, kbuf, vbuf):
    B, T, H, Dh = q.shape
    buf = kbuf.shape[1]
    scale = Dh ** -0.5
    kc = jnp.concatenate([kbuf.astype(k.dtype), k], axis=1)
    vc = jnp.concatenate([vbuf.astype(v.dtype), v], axis=1)
    parts = []
    for window, dil in DILATED_GROUPS:
        band = window // dil
        idx = buf + jnp.arange(T)[:, None] - dil * jnp.arange(band + 1)[None, :]
        valid = idx >= 0
        idx = jnp.maximum(idx, 0)
        kg = kc[:, idx]
        vg = vc[:, idx]
        s = jnp.einsum('bthd,btkhd->bthk', q, kg).astype(jnp.float32) * scale
        s = jnp.where(valid[None, :, None, :], s, -jnp.inf)
        m = jnp.max(s, axis=-1)
        p = jnp.exp(s - m[..., None])
        parts.append((m, jnp.sum(p, axis=-1), jnp.einsum('bthk,btkhd->bthd', p, vg.astype(jnp.float32))))
    return combine_dilations(parts), kc[:, T:], vc[:, T:]


def token_mixers(h, pos, s_gla0, s_hgrn0, kbuf, vbuf, w_in, gla_w_gate_up, gla_b_gate, gla_norm_g,
                 lower_bound, hgrn_norm_g, q_norm_g, k_norm_g):
    B, L, _ = h.shape
    f32 = jnp.float32
    proj = h @ w_in
    offsets = [int(o) for o in np.cumsum(IN_SPLITS)[:-1]]
    (a_q, a_k, a_v, a_lr, a_g, b_q, b_f, b_i, b_g, c_q, c_k, c_v) = jnp.split(proj, offsets, axis=-1)

    def heads(t, n):
        return t.reshape(B, L, n, -1)

    chunk = CHUNK if L % CHUNK == 0 else L

    log_alpha = jax.nn.log_sigmoid((a_lr @ gla_w_gate_up + gla_b_gate).astype(f32)) / GLA_TAU
    o_a, s_gla = gated_linear_chunked(heads(a_q, GLA_HEADS).astype(f32) * GLA_DK ** -0.5,
                                      heads(a_k, GLA_HEADS), heads(a_v, GLA_HEADS),
                                      heads(log_alpha, GLA_HEADS), s_gla0, chunk)
    o_a = rmsnorm(o_a, gla_norm_g) * jax.nn.silu(heads(a_g, GLA_HEADS).astype(f32))

    lb = lower_bound.astype(f32)
    z = b_f.astype(f32)
    log_f = jnp.log(lb + (1.0 - lb) * jax.nn.sigmoid(z))
    key = (1.0 - lb) * jax.nn.sigmoid(-z)
    o_b, s_hgrn = gated_linear_chunked(heads(jax.nn.silu(b_q.astype(f32)), HGRN_HEADS),
                                       heads(key, HGRN_HEADS), heads(b_i, HGRN_HEADS),
                                       heads(log_f, HGRN_HEADS), s_hgrn0, chunk)
    o_b = rmsnorm(o_b, hgrn_norm_g) * jax.nn.silu(heads(b_g, HGRN_HEADS).astype(f32))

    q = rope_partial(rmsnorm(heads(c_q, ATT_HEADS), q_norm_g), pos)
    k = rope_partial(rmsnorm(heads(c_k, ATT_HEADS), k_norm_g), pos)
    v = heads(c_v, ATT_HEADS)
    if kbuf is None:
        o_c = dilated_attention_prompt(q, k, v)
        keep = min(WINDOW_MAX, L)
        k_new, v_new = k[:, L - keep:], v[:, L - keep:]
    else:
        o_c, k_new, v_new = dilated_attention_sample(q, k, v, kbuf, vbuf)

    mixed = jnp.concatenate([o_a.reshape(B, L, -1), o_b.reshape(B, L, -1), o_c.reshape(B, L, -1)],
                            axis=-1).astype(h.dtype)
    return mixed, s_gla.astype(s_gla0.dtype), s_hgrn.astype(s_hgrn0.dtype), k_new, v_new


def swiglu(h, w_gate, w_up, w_down):
    return (jax.nn.silu(h @ w_gate) * (h @ w_up)) @ w_down


def moe_swiglu(h, w_router, w_gate, w_up, w_down):
    logits = (h @ w_router).astype(jnp.float32)
    top_v, top_i = lax.top_k(logits, TOP_K)
    gates = jax.nn.softmax(top_v, axis=-1)
    combine_w = jnp.sum(jax.nn.one_hot(top_i, N_EXPERTS, dtype=jnp.float32) * gates[..., None], axis=-2)
    out = jnp.zeros(h.shape, jnp.float32)
    for e in range(N_EXPERTS):
        out = out + combine_w[..., e:e + 1] * swiglu(h, w_gate[e], w_up[e], w_down[e]).astype(jnp.float32)
    return out.astype(h.dtype)


def setup_inputs(seed: int = 0) -> dict:
    key = jax.random.key(seed)
    ks = jax.random.split(key, 24)
    buf = min(WINDOW_MAX, PAST_LEN)

    def nrm(k, shape, scale):
        return jax.random.normal(k, shape, jnp.float32) * scale

    return {
        'x_prompt': nrm(ks[0], (BATCH, SEQ, D_MODEL), 1.0),
        'x_sample': nrm(ks[1], (DEC_BATCH, DEC_SEQ, D_MODEL), 1.0),
        'state_gla': nrm(ks[2], (DEPTH, DEC_BATCH, GLA_HEADS, GLA_DK, GLA_DV), 0.5),
        'state_hgrn': nrm(ks[3], (DEPTH, DEC_BATCH, HGRN_HEADS, HGRN_DK, HGRN_DV), 0.5),
        'cache_k_win': nrm(ks[4], (DEPTH, DEC_BATCH, buf, ATT_HEADS, HEAD_DIM), 1.0),
        'cache_v_win': nrm(ks[5], (DEPTH, DEC_BATCH, buf, ATT_HEADS, HEAD_DIM), 1.0),
        'norm_mix_g': 1.0 + nrm(ks[6], (DEPTH, D_MODEL), 0.02),
        'w_in': nrm(ks[7], (DEPTH, D_MODEL, N_IN), D_MODEL ** -0.5),
        'gla_w_gate_up': nrm(ks[8], (DEPTH, GLA_RANK, GLA_HEADS * GLA_DK), GLA_RANK ** -0.5),
        'gla_b_gate': nrm(ks[9], (DEPTH, GLA_HEADS * GLA_DK), 0.1),
        'gla_norm_g': 1.0 + nrm(ks[10], (DEPTH, GLA_DV), 0.02),
        'hgrn_lb_logits': nrm(ks[11], (DEPTH, HGRN_HEADS * HGRN_DK), 0.5),
        'hgrn_norm_g': 1.0 + nrm(ks[12], (DEPTH, HGRN_DV), 0.02),
        'attn_q_norm_g': 1.0 + nrm(ks[13], (DEPTH, HEAD_DIM), 0.02),
        'attn_k_norm_g': 1.0 + nrm(ks[14], (DEPTH, HEAD_DIM), 0.02),
        'w_out': nrm(ks[15], (DEPTH, MIX_WIDTH, D_MODEL), MIX_WIDTH ** -0.5),
        'norm_ffn_g': 1.0 + nrm(ks[16], (DEPTH, D_MODEL), 0.02),
        'ffn_w_gate': nrm(ks[17], (N_DENSE, D_MODEL, D_FF), D_MODEL ** -0.5),
        'ffn_w_up': nrm(ks[18], (N_DENSE, D_MODEL, D_FF), D_MODEL ** -0.5),
        'ffn_w_down': nrm(ks[19], (N_DENSE, D_FF, D_MODEL), D_FF ** -0.5),
        'moe_w_router': nrm(ks[20], (N_MOE, D_MODEL, N_EXPERTS), D_MODEL ** -0.5),
        'moe_w_gate': nrm(ks[21], (N_MOE, N_EXPERTS, D_MODEL, D_FF_EXPERT), D_MODEL ** -0.5),
        'moe_w_up': nrm(ks[22], (N_MOE, N_EXPERTS, D_MODEL, D_FF_EXPERT), D_MODEL ** -0.5),
        'moe_w_down': nrm(ks[23], (N_MOE, N_EXPERTS, D_FF_EXPERT, D_MODEL), D_FF_EXPERT ** -0.5),
    }


def reference(x_prompt, x_sample, state_gla, state_hgrn, cache_k_win, cache_v_win, norm_mix_g, w_in,
              gla_w_gate_up, gla_b_gate, gla_norm_g, hgrn_lb_logits, hgrn_norm_g, attn_q_norm_g,
              attn_k_norm_g, w_out, norm_ffn_g, ffn_w_gate, ffn_w_up, ffn_w_down, moe_w_router,
              moe_w_gate, moe_w_up, moe_w_down):
    lb_w = jax.nn.softmax(hgrn_lb_logits.astype(jnp.float32), axis=0)
    lower_bounds = jnp.cumsum(lb_w, axis=0) - lb_w[0]

    b_p, l_p = x_prompt.shape[0], x_prompt.shape[1]
    pos_p = jnp.arange(l_p, dtype=jnp.int32)
    pos_s = PAST_LEN + jnp.arange(x_sample.shape[1], dtype=jnp.int32)
    zero_gla = jnp.zeros((b_p, GLA_HEADS, GLA_DK, GLA_DV), x_prompt.dtype)
    zero_hgrn = jnp.zeros((b_p, HGRN_HEADS, HGRN_DK, HGRN_DV), x_prompt.dtype)

    xp, xs = x_prompt, x_sample
    gla_p, hgrn_p, kw_p, vw_p = [], [], [], []
    gla_s, hgrn_s, kw_s, vw_s = [], [], [], []
    for layer in range(DEPTH):
        mix_w = (w_in[layer], gla_w_gate_up[layer], gla_b_gate[layer], gla_norm_g[layer],
                 lower_bounds[layer], hgrn_norm_g[layer], attn_q_norm_g[layer], attn_k_norm_g[layer])
        mixed, sg, sh, kw, vw = token_mixers(rmsnorm(xp, norm_mix_g[layer]), pos_p, zero_gla, zero_hgrn,
                                             None, None, *mix_w)
        xp = xp + mixed @ w_out[layer]
        gla_p.append(sg)
        hgrn_p.append(sh)
        kw_p.append(kw)
        vw_p.append(vw)
        mixed, sg, sh, kw, vw = token_mixers(rmsnorm(xs, norm_mix_g[layer]), pos_s, state_gla[layer],
                                             state_hgrn[layer], cache_k_win[layer], cache_v_win[layer], *mix_w)
        xs = xs + mixed @ w_out[layer]
        gla_s.append(sg)
        hgrn_s.append(sh)
        kw_s.append(kw)
        vw_s.append(vw)
        j = layer // 2
        if layer % 2 == 0:
            xp = xp + swiglu(rmsnorm(xp, norm_ffn_g[layer]), ffn_w_gate[j], ffn_w_up[j], ffn_w_down[j])
            xs = xs + swiglu(rmsnorm(xs, norm_ffn_g[layer]), ffn_w_gate[j], ffn_w_up[j], ffn_w_down[j])
        else:
            xp = xp + moe_swiglu(rmsnorm(xp, norm_ffn_g[layer]), moe_w_router[j], moe_w_gate[j], moe_w_up[j], moe_w_down[j])
            xs = xs + moe_swiglu(rmsnorm(xs, norm_ffn_g[layer]), moe_w_router[j], moe_w_gate[j], moe_w_up[j], moe_w_down[j])

    return (xp, xs, jnp.stack(gla_p), jnp.stack(hgrn_p), jnp.stack(kw_p), jnp.stack(vw_p),
            jnp.stack(gla_s), jnp.stack(hgrn_s), jnp.stack(kw_s), jnp.stack(vw_s))
```

```python
import functools

import numpy as np
import jax
import jax.numpy as jnp
from jax import lax
from jax.experimental import pallas as pl
from jax.experimental.pallas import tpu as pltpu

F32 = jnp.float32
BF16 = jnp.bfloat16

D_MODEL = 1024
HEAD_DIM = 64
GLA_HEADS = 4
GLA_DK = 32
GLA_DV = 64
GLA_RANK = 16
GLA_TAU = 16.0
HGRN_HEADS = 4
HGRN_DK = 64
HGRN_DV = 64
ATT_HEADS = 8
DILATIONS = (1, 4, 16)
BAND = 128
ROT_DIM = 16
ROPE_THETA = 500000.0
CHUNK = 64
N_EXPERTS = 8
EPS = 1e-6
PAST_LEN = 8192
NEG = -1e30

LANES = 128
SUBLANES = 8
VMEM_LIMIT = 56 * 1024 * 1024

COL_CV = 0
COL_AQ, COL_AK, COL_AV, COL_AG = 512, 640, 768, 1024
COL_BQ, COL_BF, COL_BI, COL_BG = 1280, 1536, 1792, 2048
COL_CQ, COL_CK = 2304, 2816
COL_LR = 3328
N_PROJ = 3456


def _cparams(*sem):
    return pltpu.CompilerParams(dimension_semantics=sem, vmem_limit_bytes=VMEM_LIMIT)


def _sigmoid(x):
    return 1.0 / (1.0 + jnp.exp(-x))


def _silu(x):
    return x * _sigmoid(x)


def _log_sigmoid(x):
    return -(jnp.maximum(-x, 0.0) + jnp.log(1.0 + jnp.exp(-jnp.abs(x))))


def _dot(a, b):
    return jnp.dot(a, b, preferred_element_type=F32)


def _dot_nt(a, b):
    return lax.dot_general(a, b, (((1,), (1,)), ((), ())), preferred_element_type=F32)


def _dot_tn(a, b):
    return lax.dot_general(a, b, (((0,), (0,)), ((), ())), preferred_element_type=F32)


def _group_sum(x, ones_bd):
    hi = x.astype(BF16)
    lo = (x - hi.astype(F32)).astype(BF16)
    return _dot(hi, ones_bd) + _dot(lo, ones_bd)


def _inproj_kernel(x_ref, g_ref, w_ref, o_ref):
    x = x_ref[...]
    ms = jnp.mean(x * x, axis=-1, keepdims=True)
    h = (x * lax.rsqrt(ms + EPS)) * g_ref[...]
    o_ref[...] = _dot(h.astype(BF16), w_ref[...])


def _inproj(x, g, w, tm):
    t = x.shape[0]
    return pl.pallas_call(
        _inproj_kernel,
        grid=(t // tm,),
        in_specs=[pl.BlockSpec((tm, D_MODEL), lambda i: (i, 0)),
                  pl.BlockSpec((1, D_MODEL), lambda i: (0, 0)),
                  pl.BlockSpec((D_MODEL, N_PROJ), lambda i: (0, 0))],
        out_specs=pl.BlockSpec((tm, N_PROJ), lambda i: (i, 0)),
        out_shape=jax.ShapeDtypeStruct((t, N_PROJ), F32),
        compiler_params=_cparams("parallel"),
        name="inproj",
    )(x, g, w)


def _qkrope_kernel(q_ref, k_ref, gq_ref, gk_ref, cos_ref, sa_ref, sb_ref, ones_ref, qo_ref, ko_ref):
    cos, sa, sb = cos_ref[...], sa_ref[...], sb_ref[...]
    ones_bd = ones_ref[...]

    def norm_rope(x, g):
        ms = _group_sum(x * x, ones_bd) * (1.0 / HEAD_DIM)
        y = (x * lax.rsqrt(ms + EPS)) * g
        up = pltpu.roll(y, LANES - ROT_DIM // 2, axis=1)
        dn = pltpu.roll(y, ROT_DIM // 2, axis=1)
        return y * cos + up * sa + dn * sb

    qo_ref[...] = norm_rope(q_ref[...], gq_ref[...]) * (HEAD_DIM ** -0.5)
    ko_ref[...] = norm_rope(k_ref[...], gk_ref[...])


def _rope_tables(pos):
    half = ROT_DIM // 2
    inv_freq = ROPE_THETA ** (-jnp.arange(0, ROT_DIM, 2, dtype=F32) / ROT_DIM)
    ang = pos.astype(F32)[:, None] * inv_freq[None, :]
    cos, sin = jnp.cos(ang), jnp.sin(ang)
    n = pos.shape[0]
    rest = HEAD_DIM - ROT_DIM
    c = jnp.concatenate([cos, cos, jnp.ones((n, rest), F32)], axis=1)
    sa = jnp.concatenate([-sin, jnp.zeros((n, half + rest), F32)], axis=1)
    sb = jnp.concatenate([jnp.zeros((n, half), F32), sin, jnp.zeros((n, rest), F32)], axis=1)
    return tuple(jnp.tile(a, (1, 2)) for a in (c, sa, sb))


def _qkrope(proj, gq, gk, tables, ones_bd, tm, table_blocks):
    t = proj.shape[0]
    cq, ck = COL_CQ // LANES, COL_CK // LANES
    tab_spec = pl.BlockSpec((tm, LANES), lambda i, j: (i % table_blocks, 0))
    row_spec = pl.BlockSpec((1, LANES), lambda i, j: (0, 0))
    out_spec = pl.BlockSpec((tm, LANES), lambda i, j: (i, j))
    out_sds = jax.ShapeDtypeStruct((t, ATT_HEADS * HEAD_DIM), F32)
    return pl.pallas_call(
        _qkrope_kernel,
        grid=(t // tm, ATT_HEADS // 2),
        in_specs=[pl.BlockSpec((tm, LANES), lambda i, j: (i, cq + j)),
                  pl.BlockSpec((tm, LANES), lambda i, j: (i, ck + j)),
                  row_spec, row_spec, tab_spec, tab_spec, tab_spec,
                  pl.BlockSpec((LANES, LANES), lambda i, j: (0, 0))],
        out_specs=[out_spec, out_spec],
        out_shape=[out_sds, out_sds],
        compiler_params=_cparams("parallel", "parallel"),
        name="qkrope",
    )(proj, proj, gq, gk, *tables, ones_bd)


def _segment_scans(g, chunk):
    hk = g.shape[1]
    row = lax.broadcasted_iota(jnp.int32, g.shape, 0)

    def at(x, shift):
        return pltpu.roll(x, shift % chunk, axis=0)

    zero = jnp.zeros_like(g)
    c = {1: g, 2: g + jnp.where(row % 2 == 1, at(g, 1), zero)}
    c[4] = c[2] + jnp.where(row % 4 == 2, at(c[2], 1), zero) + jnp.where(row % 4 == 3, at(c[2], 2), zero)
    d = {1: zero, 2: jnp.where(row % 2 == 0, at(g, -1), zero)}
    c4_end = jnp.where(row % 4 == 0, at(c[4], -3),
                       jnp.where(row % 4 == 1, at(c[4], -2),
                                 jnp.where(row % 4 == 2, at(c[4], -1), c[4])))
    d[4] = c4_end - c[4]
    tiles = chunk // SUBLANES
    c4_t = c[4].reshape(tiles, SUBLANES, hk)
    row_t = row.reshape(tiles, SUBLANES, hk)
    c8_t = c4_t + jnp.where(row_t % 8 >= 4, c4_t[:, 3:4, :], jnp.zeros_like(c4_t))
    ends = c8_t[:, SUBLANES - 1:SUBLANES, :]
    prefix = [jnp.zeros((1, 1, hk), F32)]
    for j in range(tiles):
        prefix.append(prefix[-1] + ends[j:j + 1])
    b_t = c8_t + jnp.concatenate(prefix[:tiles], axis=0)
    width = 1
    while width * SUBLANES < chunk:
        m = width * SUBLANES
        start = jnp.concatenate([prefix[(j // width) * width] for j in range(tiles)], axis=0)
        stop = jnp.concatenate([prefix[(j // width + 1) * width] for j in range(tiles)], axis=0)
        c[m] = (b_t - start).reshape(chunk, hk)
        d[m] = (stop - b_t).reshape(chunk, hk)
        width *= 2
    b_last = prefix[tiles]
    return c, d, b_t.reshape(chunk, hk), (b_last - b_t).reshape(chunk, hk), b_last.reshape(1, hk)


def _pair_masks(chunk):
    t = lax.broadcasted_iota(jnp.int32, (chunk, chunk), 0)
    s = lax.broadcasted_iota(jnp.int32, (chunk, chunk), 1)
    masks = {0: (t == s).astype(F32)}
    m = 1
    while m < chunk:
        masks[m] = ((t // (2 * m) == s // (2 * m)) & (t % (2 * m) >= m) & (s % (2 * m) < m)).astype(F32)
        m *= 2
    return masks


def _gl_chunk(q, k, g, v, st, heads, dk, dv, chunk, masks):
    hk, hv = heads * dk, heads * dv
    c, d, b, b_rest, b_last = _segment_scans(g, chunk)
    lane_k = lax.broadcasted_iota(jnp.int32, (chunk, hk), 1) // dk
    lane_v = lax.broadcasted_iota(jnp.int32, (chunk, hv), 1) // dv
    levels = sorted(masks)
    q_lvl = {0: q}
    k_lvl = {0: k.astype(BF16), 1: k.astype(BF16)}
    for m in levels[1:]:
        q_lvl[m] = q * jnp.exp(c[m])
        if m > 1:
            k_lvl[m] = (k * jnp.exp(d[m])).astype(BF16)
    o = _dot_nt((q * jnp.exp(b)).astype(BF16), st.astype(BF16))
    for h in range(heads):
        a = jnp.zeros((chunk, chunk), F32)
        for m in levels:
            qh = jnp.where(lane_k == h, q_lvl[m], 0.0).astype(BF16)
            a = a + masks[m] * _dot_nt(qh, k_lvl[m])
        vh = jnp.where(lane_v == h, v, 0.0).astype(BF16)
        o = o + _dot(a.astype(BF16), vh)
    kd = (k * jnp.exp(b_rest)).astype(BF16)
    same_head = (lax.broadcasted_iota(jnp.int32, (hv, hk), 0) // dv
                 == lax.broadcasted_iota(jnp.int32, (hv, hk), 1) // dk)
    st_new = st * jnp.exp(b_last) + jnp.where(same_head, _dot_tn(v.astype(BF16), kd), 0.0)
    return o, st_new


def _head_norm_gate(o, gain, gate, ones_bd, dv):
    ms = _group_sum(o * o, ones_bd) * (1.0 / dv)
    return ((o * lax.rsqrt(ms + EPS)) * gain) * _silu(gate)


def _gl_kernel(*refs, mode, heads, dk, dv, chunk, n_chunks):
    if mode == "gla":
        (q_ref, k_ref, v_ref, gate_ref, lr_ref, wup_ref, bias_ref, gain_ref, s0_ref, ones_ref,
         o_ref, sT_ref, st_ref) = refs
    else:
        (q_ref, f_ref, v_ref, gate_ref, lb_ref, gain_ref, s0_ref, ones_ref,
         o_ref, sT_ref, st_ref) = refs
    i = pl.program_id(1)

    @pl.when(i == 0)
    def _():
        st_ref[...] = s0_ref[0]

    masks = _pair_masks(chunk)
    gain = gain_ref[...]
    ones_bd = ones_ref[...]

    def body(ci, carry):
        rows = pl.ds(pl.multiple_of(ci * chunk, chunk), chunk)
        if mode == "gla":
            q = q_ref[rows, :] * (dk ** -0.5)
            k = k_ref[rows, :]
            pre = _dot(lr_ref[rows, :].astype(BF16), wup_ref[...]) + bias_ref[...]
            g = _log_sigmoid(pre) * (1.0 / GLA_TAU)
        else:
            lb = lb_ref[...]
            q = _silu(q_ref[rows, :])
            z = f_ref[rows, :]
            g = jnp.log(lb + (1.0 - lb) * _sigmoid(z))
            k = (1.0 - lb) * _sigmoid(-z)
        o, st_new = _gl_chunk(q, k, g, v_ref[rows, :], st_ref[...], heads, dk, dv, chunk, masks)
        st_ref[...] = st_new
        o_ref[rows, :] = _head_norm_gate(o, gain, gate_ref[rows, :], ones_bd, dv)
        return carry

    if n_chunks == 1:
        body(0, 0)
    else:
        lax.fori_loop(0, n_chunks, body, 0)

    @pl.when(i == pl.num_programs(1) - 1)
    def _():
        sT_ref[0] = st_ref[...]


def _embed_state(s):
    b, h, k, v = s.shape
    eye = jnp.eye(h, dtype=s.dtype)
    return jnp.einsum("bhkv,hg->bhvgk", s, eye).reshape(b, h * v, h * k)


def _extract_state(st, h, k, v):
    b = st.shape[0]
    st = st.reshape(b, h, v, h, k)
    return jnp.stack([st[:, i, :, i, :] for i in range(h)], axis=1).transpose(0, 1, 3, 2)


def _gated_linear(mode, proj, batch, seq, s0, gain, ones_bd, extra, block_tokens, chunk):
    heads, dk, dv = (GLA_HEADS, GLA_DK, GLA_DV) if mode == "gla" else (HGRN_HEADS, HGRN_DK, HGRN_DV)
    hk, hv = heads * dk, heads * dv
    nb = seq // block_tokens
    tb = block_tokens

    def cols(width, col):
        return pl.BlockSpec((tb, width), lambda b, i: (b * nb + i, col // width))

    def const(shape):
        return pl.BlockSpec(shape, lambda b, i: (0,) * len(shape))

    state_spec = pl.BlockSpec((1, hv, hk), lambda b, i: (b, 0, 0))
    if mode == "gla":
        wup, bias = extra
        in_specs = [cols(hk, COL_AQ), cols(hk, COL_AK), cols(hv, COL_AV), cols(hv, COL_AG), cols(LANES, COL_LR),
                    const((LANES, hk)), const((1, hk)), const((1, hv)), state_spec, const((hv, hv))]
        args = (proj, proj, proj, proj, proj, wup, bias, gain, _embed_state(s0), ones_bd)
    else:
        (lb,) = extra
        in_specs = [cols(hk, COL_BQ), cols(hk, COL_BF), cols(hv, COL_BI), cols(hv, COL_BG),
                    const((1, hk)), const((1, hv)), state_spec, const((hv, hv))]
        args = (proj, proj, proj, proj, lb, gain, _embed_state(s0), ones_bd)
    o, st = pl.pallas_call(
        functools.partial(_gl_kernel, mode=mode, heads=heads, dk=dk, dv=dv, chunk=chunk,
                          n_chunks=tb // chunk),
        grid=(batch, nb),
        in_specs=in_specs,
        out_specs=[pl.BlockSpec((tb, hv), lambda b, i: (b * nb + i, 0)), state_spec],
        out_shape=[jax.ShapeDtypeStruct((batch * seq, hv), F32),
                   jax.ShapeDtypeStruct((batch, hv, hk), F32)],
        scratch_shapes=[pltpu.VMEM((hv, hk), F32)],
        compiler_params=_cparams("parallel", "arbitrary"),
        name=mode + "_scan",
    )(*args)
    return o, _extract_state(st, heads, dk, dv)


def _attn_prompt_kernel(q_ref, k_ref, v_ref, o_ref, m_ref, l_ref, *, seq):
    m_ref[...] = jnp.full(m_ref.shape, NEG, F32)
    l_ref[...] = jnp.zeros(l_ref.shape, F32)
    o_ref[...] = jnp.zeros(o_ref.shape, F32)

    qi = lax.broadcasted_iota(jnp.int32, (BAND, 2 * BAND), 0)
    ki = lax.broadcasted_iota(jnp.int32, (BAND, 2 * BAND), 1)
    in_band = (ki >= qi) & (ki <= qi + BAND)
    bias_prev = jnp.where(in_band, 0.0, NEG).astype(F32)
    bias_first = jnp.where(in_band & (ki >= BAND), 0.0, NEG).astype(F32)
    head0 = lax.broadcasted_iota(jnp.int32, (BAND, LANES), 1) < HEAD_DIM
    n_blocks = seq // BAND

    for dil in DILATIONS:
        def body(idx, carry, dil=dil):
            res = idx % dil
            blk = idx // dil
            start = res + blk * (BAND * dil)
            prev_start = jnp.maximum(start - BAND * dil, res)

            def rows(s):
                return pl.ds(s, BAND, stride=dil) if dil > 1 else pl.ds(pl.multiple_of(s, BAND), BAND)

            q = q_ref[rows(start), :]
            k2 = jnp.concatenate([k_ref[rows(prev_start), :], k_ref[rows(start), :]], axis=0).astype(BF16)
            v2 = jnp.concatenate([v_ref[rows(prev_start), :], v_ref[rows(start), :]], axis=0).astype(BF16)
            bias = jnp.where(blk > 0, bias_prev, bias_first)
            m_old = m_ref[rows(start), :]
            l_old = l_ref[rows(start), :]
            acc_old = o_ref[rows(start), :]
            m_cols, alpha_cols, l_cols, pv = [], [], [], []
            for h in range(2):
                lane0 = h * HEAD_DIM
                qh = jnp.where(head0 if h == 0 else ~head0, q, 0.0).astype(BF16)
                s = _dot_nt(qh, k2) + bias
                m_prev = m_old[:, lane0:lane0 + 1]
                m_new = jnp.maximum(m_prev, jnp.max(s, axis=-1, keepdims=True))
                p = jnp.exp(s - m_new)
                alpha = jnp.exp(m_prev - m_new)
                m_cols.append(m_new)
                alpha_cols.append(alpha)
                l_cols.append(alpha * l_old[:, lane0:lane0 + 1] + jnp.sum(p, axis=-1, keepdims=True))
                pv.append(_dot(p.astype(BF16), v2))
            alpha_b = jnp.where(head0, alpha_cols[0], alpha_cols[1])
            m_ref[rows(start), :] = jnp.where(head0, m_cols[0], m_cols[1])
            l_ref[rows(start), :] = jnp.where(head0, l_cols[0], l_cols[1])
            o_ref[rows(start), :] = alpha_b * acc_old + jnp.where(head0, pv[0], pv[1])
            return carry

        lax.fori_loop(0, n_blocks, body, 0)

    o_ref[...] = o_ref[...] / l_ref[...]


def _attn_prompt(q, k, v_src, v_col, batch, seq):
    spec = pl.BlockSpec((seq, LANES), lambda b, j: (b, j))
    vc = v_col // LANES
    return pl.pallas_call(
        functools.partial(_attn_prompt_kernel, seq=seq),
        grid=(batch, ATT_HEADS // 2),
        in_specs=[spec, spec, pl.BlockSpec((seq, LANES), lambda b, j: (b, vc + j))],
        out_specs=spec,
        out_shape=jax.ShapeDtypeStruct((batch * seq, ATT_HEADS * HEAD_DIM), F32),
        scratch_shapes=[pltpu.VMEM((seq, LANES), F32), pltpu.VMEM((seq, LANES), F32)],
        compiler_params=_cparams("parallel", "parallel"),
        name="attn_prompt",
    )(q, k, v_src)


def _sample_counts(steps, buf, padded):
    t = np.arange(steps)[:, None]
    key = np.arange(padded)[None, :]
    dist = buf + t - key
    cnt = np.zeros((steps, padded), np.float32)
    for dil in DILATIONS:
        cnt += ((dist >= 0) & (dist % dil == 0) & (dist <= BAND * dil) & (key < buf + steps)).astype(np.float32)
    return np.tile(cnt, (ATT_HEADS, 1))


def _attn_sample_kernel(q_ref, kn_ref, vn_ref, kc_ref, vc_ref, cnt_ref, *rest, steps, buf, aliased):
    if aliased:
        rest = rest[2:]
    o_ref, ko_ref, vo_ref, kf_ref, vf_ref = rest
    width = ATT_HEADS * HEAD_DIM
    keep = buf - steps
    kn, vn = kn_ref[...], vn_ref[...]
    ko_ref[0, 0, 0:keep, :] = kc_ref[0, 0, steps:buf, :]
    vo_ref[0, 0, 0:keep, :] = vc_ref[0, 0, steps:buf, :]
    ko_ref[0, 0, keep:buf, :] = kn
    vo_ref[0, 0, keep:buf, :] = vn
    pad = kf_ref.shape[0] - buf - steps
    kf_ref[0:buf, :] = kc_ref[0, 0].astype(BF16)
    vf_ref[0:buf, :] = vc_ref[0, 0].astype(BF16)
    kf_ref[buf:, :] = jnp.concatenate([kn, jnp.zeros((pad, width), F32)], axis=0).astype(BF16)
    vf_ref[buf:, :] = jnp.concatenate([vn, jnp.zeros((pad, width), F32)], axis=0).astype(BF16)

    q = q_ref[...]
    head_of_lane = lax.broadcasted_iota(jnp.int32, (steps, width), 1) // HEAD_DIM
    q_rows = jnp.concatenate([jnp.where(head_of_lane == h, q, 0.0) for h in range(ATT_HEADS)], axis=0)
    s = _dot_nt(q_rows.astype(BF16), kf_ref[...])
    cnt = cnt_ref[...]
    s = jnp.where(cnt > 0.0, s, NEG)
    m = jnp.max(s, axis=-1, keepdims=True)
    p = cnt * jnp.exp(s - m)
    den = jnp.sum(p, axis=-1, keepdims=True)
    out_rows = _dot(p.astype(BF16), vf_ref[...]) / den
    o = jnp.zeros((steps, width), F32)
    for h in range(ATT_HEADS):
        o = o + jnp.where(head_of_lane == h, out_rows[h * steps:(h + 1) * steps, :], 0.0)
    o_ref[...] = o


def _attn_sample(q, k_new, v_src, v_col, cache_k, cache_v, layer, prev_out, batch, steps):
    depth, _, buf, width = cache_k.shape
    padded = buf + LANES
    cnt = jnp.asarray(_sample_counts(steps, buf, padded))
    tok = pl.BlockSpec((steps, width), lambda b: (b, 0))
    vtok = pl.BlockSpec((steps, width), lambda b: (b, v_col // width))
    cache_spec = pl.BlockSpec((1, 1, buf, width), lambda b: (layer, b, 0, 0))
    in_specs = [tok, tok, vtok, cache_spec, cache_spec,
                pl.BlockSpec((ATT_HEADS * steps, padded), lambda b: (0, 0))]
    args = [q, k_new, v_src, cache_k, cache_v, cnt]
    aliases = {}
    if prev_out is not None:
        any_spec = pl.BlockSpec(memory_space=pl.ANY)
        in_specs += [any_spec, any_spec]
        args += list(prev_out)
        aliases = {6: 1, 7: 2}
    cache_sds = jax.ShapeDtypeStruct(cache_k.shape, F32)
    return pl.pallas_call(
        functools.partial(_attn_sample_kernel, steps=steps, buf=buf, aliased=prev_out is not None),
        grid=(batch,),
        in_specs=in_specs,
        out_specs=[tok, cache_spec, cache_spec],
        out_shape=[jax.ShapeDtypeStruct((batch * steps, width), F32), cache_sds, cache_sds],
        scratch_shapes=[pltpu.VMEM((padded, width), BF16), pltpu.VMEM((padded, width), BF16)],
        input_output_aliases=aliases,
        compiler_params=_cparams("arbitrary"),
        name="attn_sample",
    )(*args)


def _outproj(oa_ref, ob_ref, oc_ref, wout_ref, x_ref):
    na, nb = oa_ref.shape[1], ob_ref.shape[1]
    mixed = (_dot(oa_ref[...].astype(BF16), wout_ref[0:na, :])
             + _dot(ob_ref[...].astype(BF16), wout_ref[na:na + nb, :])
             + _dot(oc_ref[...].astype(BF16), wout_ref[na + nb:, :]))
    return x_ref[...] + mixed


def _rmsnorm(x, g):
    return (x * lax.rsqrt(jnp.mean(x * x, axis=-1, keepdims=True) + EPS)) * g


def _mix_ffn_kernel(x_ref, oa_ref, ob_ref, oc_ref, wout_ref, g_ref, wg_ref, wu_ref, wd_ref, o_ref):
    x2 = _outproj(oa_ref, ob_ref, oc_ref, wout_ref, x_ref)
    h = _rmsnorm(x2, g_ref[...]).astype(BF16)
    act = _silu(_dot(h, wg_ref[...])) * _dot(h, wu_ref[...])
    o_ref[...] = x2 + _dot(act.astype(BF16), wd_ref[...])


def _mix_ffn(x, oa, ob, oc, wout, g, wg, wu, wd, tm):
    t = x.shape[0]
    d_ff = wg.shape[1]

    def rows(width):
        return pl.BlockSpec((tm, width), lambda i: (i, 0))

    def const(shape):
        return pl.BlockSpec(shape, lambda i: (0, 0), pipeline_mode=pl.Buffered(1))

    return pl.pallas_call(
        _mix_ffn_kernel,
        grid=(t // tm,),
        in_specs=[rows(D_MODEL), rows(oa.shape[1]), rows(ob.shape[1]), rows(oc.shape[1]),
                  const((D_MODEL, D_MODEL)), const((1, D_MODEL)),
                  const((D_MODEL, d_ff)), const((D_MODEL, d_ff)), const((d_ff, D_MODEL))],
        out_specs=rows(D_MODEL),
        out_shape=jax.ShapeDtypeStruct((t, D_MODEL), F32),
        compiler_params=_cparams("parallel"),
        name="mix_ffn",
    )(x, oa, ob, oc, wout, g, wg, wu, wd)


def _mix_router_kernel(x_ref, oa_ref, ob_ref, oc_ref, wout_ref, g_ref, wr_ref, x2_ref, h_ref, cw_ref):
    x2 = _outproj(oa_ref, ob_ref, oc_ref, wout_ref, x_ref)
    x2_ref[...] = x2
    h = _rmsnorm(x2, g_ref[...])
    h_ref[...] = h.astype(BF16)
    logits = jnp.dot(h, wr_ref[...], preferred_element_type=F32, precision=lax.Precision.HIGHEST)
    lane = lax.broadcasted_iota(jnp.int32, logits.shape, 1)
    logits = jnp.where(lane < N_EXPERTS, logits, -jnp.inf)
    v1 = jnp.max(logits, axis=-1, keepdims=True)
    i1 = jnp.min(jnp.where(logits == v1, lane, LANES), axis=-1, keepdims=True)
    rest = jnp.where(lane == i1, -jnp.inf, logits)
    v2 = jnp.max(rest, axis=-1, keepdims=True)
    i2 = jnp.min(jnp.where(rest == v2, lane, LANES), axis=-1, keepdims=True)
    e2 = jnp.exp(v2 - v1)
    cw_ref[...] = jnp.where(lane == i1, 1.0 / (1.0 + e2), 0.0) + jnp.where(lane == i2, e2 / (1.0 + e2), 0.0)


def _mix_router(x, oa, ob, oc, wout, g, wr, tm):
    t = x.shape[0]

    def rows(width):
        return pl.BlockSpec((tm, width), lambda i: (i, 0))

    def const(shape):
        return pl.BlockSpec(shape, lambda i: (0, 0))

    return pl.pallas_call(
        _mix_router_kernel,
        grid=(t // tm,),
        in_specs=[rows(D_MODEL), rows(oa.shape[1]), rows(ob.shape[1]), rows(oc.shape[1]),
                  const((D_MODEL, D_MODEL)), const((1, D_MODEL)), const((D_MODEL, LANES))],
        out_specs=[rows(D_MODEL), rows(D_MODEL), rows(LANES)],
        out_shape=[jax.ShapeDtypeStruct((t, D_MODEL), F32), jax.ShapeDtypeStruct((t, D_MODEL), BF16),
                   jax.ShapeDtypeStruct((t, LANES), F32)],
        compiler_params=_cparams("parallel"),
        name="mix_router",
    )(x, oa, ob, oc, wout, g, wr)


def _moe_kernel(x2_ref, h_ref, cw_ref, wg_ref, wu_ref, wd_ref, o_ref):
    e = pl.program_id(1)

    @pl.when(e == 0)
    def _():
        o_ref[...] = x2_ref[...]

    h = h_ref[...]
    act = _silu(_dot(h, wg_ref[0])) * _dot(h, wu_ref[0])
    y = _dot(act.astype(BF16), wd_ref[0])
    lane = lax.broadcasted_iota(jnp.int32, cw_ref.shape, 1)
    w = jnp.sum(jnp.where(lane == e, cw_ref[...], 0.0), axis=-1, keepdims=True)
    o_ref[...] += w * y


def _moe(x2, h, cw, wg, wu, wd, tm):
    t = x2.shape[0]
    n_e, _, d_ff = wg.shape

    def rows(width):
        return pl.BlockSpec((tm, width), lambda i, e: (i, 0))

    return pl.pallas_call(
        _moe_kernel,
        grid=(t // tm, n_e),
        in_specs=[rows(D_MODEL), rows(D_MODEL), rows(LANES),
                  pl.BlockSpec((1, D_MODEL, d_ff), lambda i, e: (e, 0, 0)),
                  pl.BlockSpec((1, D_MODEL, d_ff), lambda i, e: (e, 0, 0)),
                  pl.BlockSpec((1, d_ff, D_MODEL), lambda i, e: (e, 0, 0))],
        out_specs=rows(D_MODEL),
        out_shape=jax.ShapeDtypeStruct((t, D_MODEL), F32),
        compiler_params=_cparams("parallel", "arbitrary"),
        name="moe",
    )(x2, h, cw, wg, wu, wd)


def _block_diag_ones(n, group):
    idx = np.arange(n) // group
    return jnp.asarray((idx[:, None] == idx[None, :]).astype(np.float32), dtype=BF16)


def _rearrange_w_in(w):
    offs = np.cumsum([0, 128, 128, 256, GLA_RANK, 256, 256, 256, 256, 256, 512, 512, 512])
    a_q, a_k, a_v, a_lr, a_g, b_q, b_f, b_i, b_g, c_q, c_k, c_v = [
        w[:, offs[n]:offs[n + 1]] for n in range(12)]
    lr = jnp.pad(a_lr, ((0, 0), (0, LANES - GLA_RANK)))
    return jnp.concatenate([c_v, a_q, a_k, a_v, a_g, b_q, b_f, b_i, b_g, c_q, c_k, lr], axis=1).astype(BF16)


def _tile_rows(seq, cap):
    t = cap
    while seq % t:
        t //= 2
    return t


def kernel(x_prompt, x_sample, state_gla, state_hgrn, cache_k_win, cache_v_win, norm_mix_g, w_in,
           gla_w_gate_up, gla_b_gate, gla_norm_g, hgrn_lb_logits, hgrn_norm_g, attn_q_norm_g,
           attn_k_norm_g, w_out, norm_ffn_g, ffn_w_gate, ffn_w_up, ffn_w_down, moe_w_router,
           moe_w_gate, moe_w_up, moe_w_down):
    depth = w_in.shape[0]
    bp, lp, _ = x_prompt.shape
    bs, ls, _ = x_sample.shape
    buf = cache_k_win.shape[2]
    width = ATT_HEADS * HEAD_DIM
    assert lp % (BAND * max(DILATIONS)) == 0 and lp % CHUNK == 0 and ls % CHUNK != 0 and ls == SUBLANES
    assert buf >= BAND * max(DILATIONS)

    lb_w = jax.nn.softmax(hgrn_lb_logits.astype(F32), axis=0)
    lower_bounds = jnp.cumsum(lb_w, axis=0) - lb_w[0]

    ones_att = _block_diag_ones(LANES, HEAD_DIM)
    ones_gla = _block_diag_ones(GLA_HEADS * GLA_DV, GLA_DV)
    ones_hgrn = _block_diag_ones(HGRN_HEADS * HGRN_DV, HGRN_DV)
    tm_p = _tile_rows(bp * lp, 256)
    tm_s = _tile_rows(bs * ls, 256)
    tables_p = _rope_tables(jnp.arange(lp, dtype=jnp.int32))
    tables_s = _rope_tables(jnp.tile(PAST_LEN + jnp.arange(ls, dtype=jnp.int32), tm_s // ls))
    ck = cache_k_win.reshape(depth, bs, buf, width)
    cv = cache_v_win.reshape(depth, bs, buf, width)

    xp = x_prompt.reshape(bp * lp, D_MODEL)
    xs = x_sample.reshape(bs * ls, D_MODEL)
    zero_gla = jnp.zeros((bp, GLA_HEADS, GLA_DK, GLA_DV), F32)
    zero_hgrn = jnp.zeros((bp, HGRN_HEADS, HGRN_DK, HGRN_DV), F32)
    gla_p, hgrn_p, kw_p, vw_p, gla_s, hgrn_s = [], [], [], [], [], []
    caches_s = None
    keep = min(buf, lp)

    for layer in range(depth):
        w_proj = _rearrange_w_in(w_in[layer])
        g_mix = norm_mix_g[layer].reshape(1, D_MODEL)
        wup = jnp.pad(gla_w_gate_up[layer], ((0, LANES - GLA_RANK), (0, 0))).astype(BF16)
        bias = gla_b_gate[layer].reshape(1, -1)
        gain_a = jnp.tile(gla_norm_g[layer], GLA_HEADS).reshape(1, -1)
        gain_b = jnp.tile(hgrn_norm_g[layer], HGRN_HEADS).reshape(1, -1)
        lb = lower_bounds[layer].reshape(1, -1)
        gq = jnp.tile(attn_q_norm_g[layer], 2).reshape(1, LANES)
        gk = jnp.tile(attn_k_norm_g[layer], 2).reshape(1, LANES)
        wout = w_out[layer].astype(BF16)
        g_ffn = norm_ffn_g[layer].reshape(1, D_MODEL)

        mixed = []
        for which, x, batch, seq, tm in (("p", xp, bp, lp, tm_p), ("s", xs, bs, ls, tm_s)):
            proj = _inproj(x, g_mix, w_proj, tm)
            if which == "p":
                qn, kn = _qkrope(proj, gq, gk, tables_p, ones_att, tm, lp // tm)
                s0_a, s0_b, block, chunk = zero_gla, zero_hgrn, min(seq, 8 * CHUNK), CHUNK
            else:
                qn, kn = _qkrope(proj, gq, gk, tables_s, ones_att, tm, 1)
                s0_a, s0_b, block, chunk = state_gla[layer], state_hgrn[layer], seq, seq
            o_a, s_a = _gated_linear("gla", proj, batch, seq, s0_a, gain_a, ones_gla, (wup, bias), block, chunk)
            o_b, s_b = _gated_linear("hgrn", proj, batch, seq, s0_b, gain_b, ones_hgrn, (lb,), block, chunk)
            if which == "p":
                o_c = _attn_prompt(qn, kn, proj, COL_CV, batch, seq)
                gla_p.append(s_a)
                hgrn_p.append(s_b)
                kw_p.append(kn.reshape(batch, seq, ATT_HEADS, HEAD_DIM)[:, seq - keep:])
                vw_p.append(proj[:, COL_CV:COL_CV + width].reshape(batch, seq, ATT_HEADS, HEAD_DIM)[:, seq - keep:])
            else:
                o_c, ck_new, cv_new = _attn_sample(qn, kn, proj, COL_CV, ck, cv, layer, caches_s, batch, seq)
                caches_s = (ck_new, cv_new)
                gla_s.append(s_a)
                hgrn_s.append(s_b)
            mixed.append((o_a, o_b, o_c))

        j = layer // 2
        new_x = []
        for (o_a, o_b, o_c), x, tm in ((mixed[0], xp, tm_p), (mixed[1], xs, tm_s)):
            if layer % 2 == 0:
                new_x.append(_mix_ffn(x, o_a, o_b, o_c, wout, g_ffn, ffn_w_gate[j].astype(BF16),
                                      ffn_w_up[j].astype(BF16), ffn_w_down[j].astype(BF16), tm))
            else:
                wr = jnp.pad(moe_w_router[j], ((0, 0), (0, LANES - N_EXPERTS)))
                x2, h, cw = _mix_router(x, o_a, o_b, o_c, wout, g_ffn, wr, tm)
                new_x.append(_moe(x2, h, cw, moe_w_gate[j].astype(BF16), moe_w_up[j].astype(BF16),
                                  moe_w_down[j].astype(BF16), _tile_rows(x.shape[0], 512)))
        xp, xs = new_x

    cache_shape = (depth, bs, buf, ATT_HEADS, HEAD_DIM)
    return (xp.reshape(bp, lp, D_MODEL), xs.reshape(bs, ls, D_MODEL),
            jnp.stack(gla_p), jnp.stack(hgrn_p), jnp.stack(kw_p), jnp.stack(vw_p),
            jnp.stack(gla_s), jnp.stack(hgrn_s),
            caches_s[0].reshape(cache_shape), caches_s[1].reshape(cache_shape))
```

```python
import functools

import numpy as np
import jax
import jax.numpy as jnp
from jax import lax
from jax.experimental import pallas as pl
from jax.experimental.pallas import tpu as pltpu

F32 = jnp.float32
BF16 = jnp.bfloat16

D_MODEL = 1024
HEAD_DIM = 64
GLA_HEADS = 4
GLA_DK = 32
GLA_DV = 64
GLA_RANK = 16
GLA_TAU = 16.0
HGRN_HEADS = 4
HGRN_DK = 64
HGRN_DV = 64
ATT_HEADS = 8
DILATIONS = (1, 4, 16)
BAND = 128
ROT_DIM = 16
ROPE_THETA = 500000.0
CHUNK = 64
N_EXPERTS = 8
EPS = 1e-6
PAST_LEN = 8192
NEG = -1e30

LANES = 128
SUBLANES = 8
VMEM_LIMIT = 56 * 1024 * 1024

COL_CV = 0
COL_AQ, COL_AK, COL_AV, COL_AG = 512, 640, 768, 1024
COL_BQ, COL_BF, COL_BI, COL_BG = 1280, 1536, 1792, 2048
COL_CQ, COL_CK = 2304, 2816
COL_LR = 3328
N_PROJ = 3456


def _cparams(*sem):
    return pltpu.CompilerParams(dimension_semantics=sem, vmem_limit_bytes=VMEM_LIMIT)


def _sigmoid(x):
    return 1.0 / (1.0 + jnp.exp(-x))


def _silu(x):
    return x * _sigmoid(x)


def _log_sigmoid(x):
    return -(jnp.maximum(-x, 0.0) + jnp.log(1.0 + jnp.exp(-jnp.abs(x))))


def _dot(a, b):
    return jnp.dot(a, b, preferred_element_type=F32)


def _dot_nt(a, b):
    return lax.dot_general(a, b, (((1,), (1,)), ((), ())), preferred_element_type=F32)


def _dot_tn(a, b):
    return lax.dot_general(a, b, (((0,), (0,)), ((), ())), preferred_element_type=F32)


def _group_sum(x, ones_bd):
    hi = x.astype(BF16)
    lo = (x - hi.astype(F32)).astype(BF16)
    return _dot(hi, ones_bd) + _dot(lo, ones_bd)


def _inproj_kernel(x_ref, g_ref, w_ref, o_ref):
    x = x_ref[...]
    ms = jnp.mean(x * x, axis=-1, keepdims=True)
    h = (x * lax.rsqrt(ms + EPS)) * g_ref[...]
    o_ref[...] = _dot(h.astype(BF16), w_ref[...])


def _inproj(x, g, w, tm):
    t = x.shape[0]
    return pl.pallas_call(
        _inproj_kernel,
        grid=(t // tm,),
        in_specs=[pl.BlockSpec((tm, D_MODEL), lambda i: (i, 0)),
                  pl.BlockSpec((1, D_MODEL), lambda i: (0, 0)),
                  pl.BlockSpec((D_MODEL, N_PROJ), lambda i: (0, 0))],
        out_specs=pl.BlockSpec((tm, N_PROJ), lambda i: (i, 0)),
        out_shape=jax.ShapeDtypeStruct((t, N_PROJ), F32),
        compiler_params=_cparams("parallel"),
        name="inproj",
    )(x, g, w)


def _qkrope_kernel(q_ref, k_ref, gq_ref, gk_ref, cos_ref, sa_ref, sb_ref, ones_ref, qo_ref, ko_ref):
    cos, sa, sb = cos_ref[...], sa_ref[...], sb_ref[...]
    ones_bd = ones_ref[...]

    def norm_rope(x, g):
        ms = _group_sum(x * x, ones_bd) * (1.0 / HEAD_DIM)
        y = (x * lax.rsqrt(ms + EPS)) * g
        up = pltpu.roll(y, LANES - ROT_DIM // 2, axis=1)
        dn = pltpu.roll(y, ROT_DIM // 2, axis=1)
        return y * cos + up * sa + dn * sb

    qo_ref[...] = norm_rope(q_ref[...], gq_ref[...]) * (HEAD_DIM ** -0.5)
    ko_ref[...] = norm_rope(k_ref[...], gk_ref[...])


def _rope_tables(pos):
    half = ROT_DIM // 2
    inv_freq = ROPE_THETA ** (-jnp.arange(0, ROT_DIM, 2, dtype=F32) / ROT_DIM)
    ang = pos.astype(F32)[:, None] * inv_freq[None, :]
    cos, sin = jnp.cos(ang), jnp.sin(ang)
    n = pos.shape[0]
    rest = HEAD_DIM - ROT_DIM
    c = jnp.concatenate([cos, cos, jnp.ones((n, rest), F32)], axis=1)
    sa = jnp.concatenate([-sin, jnp.zeros((n, half + rest), F32)], axis=1)
    sb = jnp.concatenate([jnp.zeros((n, half), F32), sin, jnp.zeros((n, rest), F32)], axis=1)
    return tuple(jnp.tile(a, (1, 2)) for a in (c, sa, sb))


def _qkrope(proj, gq, gk, tables, ones_bd, tm, table_blocks):
    t = proj.shape[0]
    cq, ck = COL_CQ // LANES, COL_CK // LANES
    tab_spec = pl.BlockSpec((tm, LANES), lambda i, j: (i % table_blocks, 0))
    row_spec = pl.BlockSpec((1, LANES), lambda i, j: (0, 0))
    out_spec = pl.BlockSpec((tm, LANES), lambda i, j: (i, j))
    out_sds = jax.ShapeDtypeStruct((t, ATT_HEADS * HEAD_DIM), F32)
    return pl.pallas_call(
        _qkrope_kernel,
        grid=(t // tm, ATT_HEADS // 2),
        in_specs=[pl.BlockSpec((tm, LANES), lambda i, j: (i, cq + j)),
                  pl.BlockSpec((tm, LANES), lambda i, j: (i, ck + j)),
                  row_spec, row_spec, tab_spec, tab_spec, tab_spec,
                  pl.BlockSpec((LANES, LANES), lambda i, j: (0, 0))],
        out_specs=[out_spec, out_spec],
        out_shape=[out_sds, out_sds],
        compiler_params=_cparams("parallel", "parallel"),
        name="qkrope",
    )(proj, proj, gq, gk, *tables, ones_bd)


def _segment_scans(g, chunk):
    rows, hk = g.shape
    n_chunks = rows // chunk
    row = lax.broadcasted_iota(jnp.int32, g.shape, 0)

    def at(x, shift):
        return pltpu.roll(x, shift % rows, axis=0)

    zero = jnp.zeros_like(g)
    c = {1: g, 2: g + jnp.where(row % 2 == 1, at(g, 1), zero)}
    c[4] = c[2] + jnp.where(row % 4 == 2, at(c[2], 1), zero) + jnp.where(row % 4 == 3, at(c[2], 2), zero)
    d = {1: zero, 2: jnp.where(row % 2 == 0, at(g, -1), zero)}
    c4_end = jnp.where(row % 4 == 0, at(c[4], -3),
                       jnp.where(row % 4 == 1, at(c[4], -2),
                                 jnp.where(row % 4 == 2, at(c[4], -1), c[4])))
    d[4] = c4_end - c[4]
    tiles = chunk // SUBLANES
    c4_t = c[4].reshape(n_chunks * tiles, SUBLANES, hk)
    row_t = row.reshape(n_chunks * tiles, SUBLANES, hk)
    c8_t = c4_t + jnp.where(row_t % 8 >= 4, c4_t[:, 3:4, :], jnp.zeros_like(c4_t))
    ends = c8_t[:, SUBLANES - 1:SUBLANES, :]
    prefix = []
    for ci in range(n_chunks):
        acc = [jnp.zeros((1, 1, hk), F32)]
        for j in range(tiles):
            acc.append(acc[-1] + ends[ci * tiles + j:ci * tiles + j + 1])
        prefix.append(acc)

    def per_tile(pick):
        return jnp.concatenate([prefix[ci][pick(j)] for ci in range(n_chunks) for j in range(tiles)], axis=0)

    b_t = c8_t + per_tile(lambda j: j)
    width = 1
    while width * SUBLANES < chunk:
        m = width * SUBLANES
        c[m] = (b_t - per_tile(lambda j: (j // width) * width)).reshape(rows, hk)
        d[m] = (per_tile(lambda j: (j // width + 1) * width) - b_t).reshape(rows, hk)
        width *= 2
    b_rest = (per_tile(lambda j: tiles) - b_t).reshape(rows, hk)
    b_last = [prefix[ci][tiles].reshape(1, hk) for ci in range(n_chunks)]
    return c, d, b_t.reshape(rows, hk), b_rest, b_last


def _pair_masks(chunk, copies):
    t = lax.broadcasted_iota(jnp.int32, (copies * chunk, chunk), 0) % chunk
    s = lax.broadcasted_iota(jnp.int32, (copies * chunk, chunk), 1)
    masks = {0: t == s}
    m = 1
    while m < chunk:
        masks[m] = (t // (2 * m) == s // (2 * m)) & (t % (2 * m) >= m) & (s % (2 * m) < m)
        m *= 2
    return masks


def _gl_block(q, k, g, v, st, heads, dk, dv, chunk):
    rows = q.shape[0]
    hk, hv = heads * dk, heads * dv
    stacked_masks = _pair_masks(chunk, heads)
    levels = sorted(stacked_masks)
    c, d, b, b_rest, b_last = _segment_scans(g, chunk)
    k_bf = k.astype(BF16)
    q_lvl = {0: q.astype(BF16)}
    k_lvl = {0: k_bf, 1: k_bf}
    for m in levels[1:]:
        q_lvl[m] = (q * jnp.exp(c[m])).astype(BF16)
        if m > 1:
            k_lvl[m] = (k * jnp.exp(d[m])).astype(BF16)
    q_in = (q * jnp.exp(b)).astype(BF16)
    k_out = (k * jnp.exp(b_rest)).astype(BF16)
    v_bf = v.astype(BF16)
    lane_k = lax.broadcasted_iota(jnp.int32, (chunk, hk), 1) // dk
    lane_v = lax.broadcasted_iota(jnp.int32, (chunk, hv), 1) // dv
    same_head = (lax.broadcasted_iota(jnp.int32, (hv, hk), 0) // dv
                 == lax.broadcasted_iota(jnp.int32, (hv, hk), 1) // dk)
    zero_k = jnp.zeros((chunk, hk), BF16)
    zero_v = jnp.zeros((chunk, hv), BF16)
    o_intra, st_inc = [], []
    for ci in range(rows // chunk):
        r = slice(ci * chunk, (ci + 1) * chunk)
        a = jnp.zeros((heads * chunk, chunk), F32)
        for m in levels:
            q_heads = jnp.concatenate([jnp.where(lane_k == h, q_lvl[m][r], zero_k) for h in range(heads)], axis=0)
            a = jnp.where(stacked_masks[m], _dot_nt(q_heads, k_lvl[m][r]), a)
        a = a.astype(BF16)
        o = None
        for h in range(heads):
            part = _dot(a[h * chunk:(h + 1) * chunk], jnp.where(lane_v == h, v_bf[r], zero_v))
            o = part if o is None else o + part
        o_intra.append(o)
        st_inc.append(jnp.where(same_head, _dot_tn(v_bf[r], k_out[r]), 0.0))
    outs = []
    for ci in range(rows // chunk):
        r = slice(ci * chunk, (ci + 1) * chunk)
        outs.append(o_intra[ci] + _dot_nt(q_in[r], st.astype(BF16)))
        st = st * jnp.exp(b_last[ci]) + st_inc[ci]
    return (outs[0] if len(outs) == 1 else jnp.concatenate(outs, axis=0)), st


def _head_norm_gate(o, gain, gate, ones_bd, dv):
    ms = _group_sum(o * o, ones_bd) * (1.0 / dv)
    return ((o * lax.rsqrt(ms + EPS)) * gain) * _silu(gate)


def _gl_kernel(*refs, mode, heads, dk, dv, chunk):
    if mode == "gla":
        (q_ref, k_ref, v_ref, gate_ref, lr_ref, wup_ref, bias_ref, gain_ref, s0_ref, ones_ref,
         o_ref, sT_ref, st_ref) = refs
        q = q_ref[...] * (dk ** -0.5)
        k = k_ref[...]
        g = _log_sigmoid(_dot(lr_ref[...].astype(BF16), wup_ref[...]) + bias_ref[...]) * (1.0 / GLA_TAU)
    else:
        (q_ref, f_ref, v_ref, gate_ref, lb_ref, gain_ref, s0_ref, ones_ref,
         o_ref, sT_ref, st_ref) = refs
        lb = lb_ref[...]
        q = _silu(q_ref[...])
        z = f_ref[...]
        g = jnp.log(lb + (1.0 - lb) * _sigmoid(z))
        k = (1.0 - lb) * _sigmoid(-z)
    i = pl.program_id(1)

    @pl.when(i == 0)
    def _():
        st_ref[...] = s0_ref[0]

    o, st = _gl_block(q, k, g, v_ref[...], st_ref[...], heads, dk, dv, chunk)
    st_ref[...] = st
    o_ref[...] = _head_norm_gate(o, gain_ref[...], gate_ref[...], ones_ref[...], dv)

    @pl.when(i == pl.num_programs(1) - 1)
    def _():
        sT_ref[0] = st


def _embed_state(s):
    b, h, k, v = s.shape
    eye = jnp.eye(h, dtype=s.dtype)
    return jnp.einsum("bhkv,hg->bhvgk", s, eye).reshape(b, h * v, h * k)


def _extract_state(st, h, k, v):
    b = st.shape[0]
    st = st.reshape(b, h, v, h, k)
    return jnp.stack([st[:, i, :, i, :] for i in range(h)], axis=1).transpose(0, 1, 3, 2)


def _gated_linear(mode, proj, batch, seq, s0, gain, ones_bd, extra, block_tokens, chunk):
    heads, dk, dv = (GLA_HEADS, GLA_DK, GLA_DV) if mode == "gla" else (HGRN_HEADS, HGRN_DK, HGRN_DV)
    hk, hv = heads * dk, heads * dv
    nb = seq // block_tokens
    tb = block_tokens

    def cols(width, col):
        return pl.BlockSpec((tb, width), lambda b, i: (b * nb + i, col // width))

    def const(shape):
        return pl.BlockSpec(shape, lambda b, i: (0,) * len(shape))

    state_spec = pl.BlockSpec((1, hv, hk), lambda b, i: (b, 0, 0))
    if mode == "gla":
        wup, bias = extra
        in_specs = [cols(hk, COL_AQ), cols(hk, COL_AK), cols(hv, COL_AV), cols(hv, COL_AG), cols(LANES, COL_LR),
                    const((LANES, hk)), const((1, hk)), const((1, hv)), state_spec, const((hv, hv))]
        args = (proj, proj, proj, proj, proj, wup, bias, gain, _embed_state(s0), ones_bd)
    else:
        (lb,) = extra
        in_specs = [cols(hk, COL_BQ), cols(hk, COL_BF), cols(hv, COL_BI), cols(hv, COL_BG),
                    const((1, hk)), const((1, hv)), state_spec, const((hv, hv))]
        args = (proj, proj, proj, proj, lb, gain, _embed_state(s0), ones_bd)
    o, st = pl.pallas_call(
        functools.partial(_gl_kernel, mode=mode, heads=heads, dk=dk, dv=dv, chunk=chunk),
        grid=(batch, nb),
        in_specs=in_specs,
        out_specs=[pl.BlockSpec((tb, hv), lambda b, i: (b * nb + i, 0)), state_spec],
        out_shape=[jax.ShapeDtypeStruct((batch * seq, hv), F32),
                   jax.ShapeDtypeStruct((batch, hv, hk), F32)],
        scratch_shapes=[pltpu.VMEM((hv, hk), F32)],
        compiler_params=_cparams("parallel", "arbitrary"),
        name=mode + "_scan",
    )(*args)
    return o, _extract_state(st, heads, dk, dv)


def _attn_prompt_kernel(q_ref, k_ref, v_ref, o_ref, qr_ref, kt_ref, kr_ref, vt_ref, vr_ref,
                        m_ref, l_ref, acc_ref, *, seq):
    n_res = max(DILATIONS)
    per_res = seq // n_res
    head0 = lax.broadcasted_iota(jnp.int32, (per_res, LANES), 1) < HEAD_DIM
    for r in range(n_res):
        tok = slice(r * per_res, (r + 1) * per_res)
        strided = pl.ds(r, per_res, stride=n_res)
        k_t, v_t = k_ref[tok, :], v_ref[tok, :]
        k_r, v_r = k_ref[strided, :], v_ref[strided, :]
        qr_ref[tok, :] = q_ref[strided, :]
        for h in range(2):
            mine = head0 if h == 0 else ~head0
            kt_ref[h, tok, :] = jnp.where(mine, k_t, 0.0).astype(BF16)
            kr_ref[h, tok, :] = jnp.where(mine, k_r, 0.0).astype(BF16)
            vt_ref[h, tok, :] = jnp.where(mine, v_t, 1.0).astype(BF16)
            vr_ref[h, tok, :] = jnp.where(mine, v_r, 1.0).astype(BF16)
    m_ref[...] = jnp.full(m_ref.shape, NEG, F32)
    l_ref[...] = jnp.zeros(l_ref.shape, F32)
    acc_ref[...] = jnp.zeros(acc_ref.shape, F32)

    head0 = lax.broadcasted_iota(jnp.int32, (BAND, LANES), 1) < HEAD_DIM
    qi = lax.broadcasted_iota(jnp.int32, (BAND, 2 * BAND), 0)
    ki = lax.broadcasted_iota(jnp.int32, (BAND, 2 * BAND), 1)

    for dil in DILATIONS:
        n_runs = n_res // dil
        run = SUBLANES * dil
        keys_by_token = dil == 1

        def seq_index(u, n_runs=n_runs, run=run):
            return n_runs * (u % run) + u // run

        kj = ki if keys_by_token else seq_index(ki % BAND) + BAND * (ki // BAND)
        rel = BAND + seq_index(qi) - kj
        in_band = (rel >= 0) & (rel <= BAND)
        bias_prev = jnp.where(in_band, 0.0, NEG).astype(F32)
        bias_first = jnp.where(in_band & (ki >= BAND), 0.0, NEG).astype(F32)

        def body(idx, carry, dil=dil, n_runs=n_runs, run=run, keys_by_token=keys_by_token,
                 bias_prev=bias_prev, bias_first=bias_first):
            res = idx % dil
            blk = idx // dil
            starts = [pl.multiple_of((res + dil * c) * per_res + run * blk, run) for c in range(n_runs)]
            prevs = [pl.multiple_of((res + dil * c) * per_res + run * jnp.maximum(blk - 1, 0), run)
                     for c in range(n_runs)]

            def gather(ref, offs, *lead):
                parts = [ref[lead + (pl.ds(o, run), slice(None))] for o in offs]
                return parts[0] if len(parts) == 1 else jnp.concatenate(parts, axis=0)

            def scatter(ref, offs, val, *lead):
                for c, o in enumerate(offs):
                    ref[lead + (pl.ds(o, run), slice(None))] = val[c * run:(c + 1) * run, :]

            q = gather(qr_ref, starts).astype(BF16)
            bias = jnp.where(blk > 0, bias_prev, bias_first)
            alphas, pvs = [], []
            for h in range(2):
                if keys_by_token:
                    cur = pl.multiple_of(blk * BAND, BAND)
                    prev = pl.multiple_of(jnp.maximum(blk - 1, 0) * BAND, BAND)
                    k2 = jnp.concatenate([kt_ref[h, pl.ds(prev, BAND), :], kt_ref[h, pl.ds(cur, BAND), :]], axis=0)
                    v2 = jnp.concatenate([vt_ref[h, pl.ds(prev, BAND), :], vt_ref[h, pl.ds(cur, BAND), :]], axis=0)
                else:
                    k2 = jnp.concatenate([gather(kr_ref, prevs, h), gather(kr_ref, starts, h)], axis=0)
                    v2 = jnp.concatenate([gather(vr_ref, prevs, h), gather(vr_ref, starts, h)], axis=0)
                s = _dot_nt(q, k2) + bias
                m_prev = gather(m_ref, starts, h)
                m_new = jnp.maximum(m_prev, jnp.max(s, axis=-1, keepdims=True))
                scatter(m_ref, starts, m_new, h)
                p = jnp.exp(s - jnp.concatenate([m_new, m_new], axis=1))
                alphas.append(jnp.exp(m_prev - m_new))
                pvs.append(_dot(p.astype(BF16), v2))
            scatter(l_ref, starts, jnp.where(head0, alphas[1], alphas[0]) * gather(l_ref, starts)
                    + jnp.where(head0, pvs[1], pvs[0]))
            scatter(acc_ref, starts, jnp.where(head0, alphas[0], alphas[1]) * gather(acc_ref, starts)
                    + jnp.where(head0, pvs[0], pvs[1]))
            return carry

        lax.fori_loop(0, seq // BAND, body, 0, unroll=8)

    for r in range(n_res):
        tok = slice(r * per_res, (r + 1) * per_res)
        o_ref[pl.ds(r, per_res, stride=n_res), :] = acc_ref[tok, :] / pltpu.roll(l_ref[tok, :], HEAD_DIM, axis=1)


def _attn_prompt(q, k, v_src, v_col, batch, seq):
    spec = pl.BlockSpec((seq, LANES), lambda b, j: (b, j))
    vc = v_col // LANES
    return pl.pallas_call(
        functools.partial(_attn_prompt_kernel, seq=seq),
        grid=(batch, ATT_HEADS // 2),
        in_specs=[spec, spec, pl.BlockSpec((seq, LANES), lambda b, j: (b, vc + j))],
        out_specs=spec,
        out_shape=jax.ShapeDtypeStruct((batch * seq, ATT_HEADS * HEAD_DIM), F32),
        scratch_shapes=[pltpu.VMEM((seq, LANES), F32),
                        pltpu.VMEM((2, seq, LANES), BF16), pltpu.VMEM((2, seq, LANES), BF16),
                        pltpu.VMEM((2, seq, LANES), BF16), pltpu.VMEM((2, seq, LANES), BF16),
                        pltpu.VMEM((2, seq, LANES), F32), pltpu.VMEM((seq, LANES), F32),
                        pltpu.VMEM((seq, LANES), F32)],
        compiler_params=_cparams("parallel", "parallel"),
        name="attn_prompt",
    )(q, k, v_src)


def _sample_counts(steps, buf):
    t = np.arange(steps)[:, None]

    def count(dist, live):
        cnt = np.zeros(dist.shape, np.float32)
        for dil in DILATIONS:
            cnt += (live & (dist >= 0) & (dist % dil == 0) & (dist <= BAND * dil)).astype(np.float32)
        return cnt

    lane = np.arange(LANES)[None, :]
    return (count(buf + t - np.arange(buf)[None, :], True),
            count(t - (lane - (LANES - steps)), lane >= LANES - steps))


def _attn_sample_kernel(q_ref, kn_ref, vn_ref, kc_ref, vc_ref, cc_ref, cn_ref, *rest, steps, aliased):
    if aliased:
        rest = rest[2:]
    o_ref, ko_ref, vo_ref = rest
    q = q_ref[0].astype(BF16)
    k_old, v_old = kc_ref[0, 0], vc_ref[0, 0]
    k_new, v_new = kn_ref[0], vn_ref[0]
    cc, cn = cc_ref[...][None], cn_ref[...][None]

    def scores(keys):
        return lax.dot_general(q, keys.astype(BF16), (((2,), (1,)), ((0,), (0,))), preferred_element_type=F32)

    def weighted(p, values):
        return lax.dot_general(p.astype(BF16), values.astype(BF16), (((2,), (2,)), ((0,), (0,))),
                               preferred_element_type=F32)

    s_old = jnp.where(cc > 0.0, scores(k_old), NEG)
    s_new = jnp.where(cn > 0.0, scores(k_new), NEG)
    m = jnp.maximum(jnp.max(s_old, axis=-1, keepdims=True), jnp.max(s_new, axis=-1, keepdims=True))
    p_old = cc * jnp.exp(s_old - m)
    p_new = cn * jnp.exp(s_new - m)
    den = jnp.sum(p_old, axis=-1, keepdims=True) + jnp.sum(p_new, axis=-1, keepdims=True)
    o_ref[0] = (weighted(p_old, v_old) + weighted(p_new, v_new)) / den

    buf = k_old.shape[-1]
    lane = lax.broadcasted_iota(jnp.int32, k_new.shape, 2)
    for old, new, out_ref in ((k_old, k_new, ko_ref), (v_old, v_new, vo_ref)):
        shifted = pltpu.roll(old, buf - steps, axis=2)
        out_ref[0, 0, :, :, 0:buf - LANES] = shifted[:, :, 0:buf - LANES]
        out_ref[0, 0, :, :, buf - LANES:buf] = jnp.where(lane >= LANES - steps, new, shifted[:, :, buf - LANES:buf])


def _attn_sample(q, k_new, v_new, cache_k, cache_v, layer, prev_out, steps, head_split):
    depth, batch, heads, hd, buf = cache_k.shape
    hb = heads // head_split
    cc, cn = (jnp.asarray(c) for c in _sample_counts(steps, buf))
    q_spec = pl.BlockSpec((1, hb, steps, hd), lambda b, j: (b, j, 0, 0))
    new_spec = pl.BlockSpec((1, hb, hd, LANES), lambda b, j: (b, j, 0, 0))
    cache_spec = pl.BlockSpec((1, 1, hb, hd, buf), lambda b, j: (layer, b, j, 0, 0))
    in_specs = [q_spec, new_spec, new_spec, cache_spec, cache_spec,
                pl.BlockSpec((steps, buf), lambda b, j: (0, 0)), pl.BlockSpec((steps, LANES), lambda b, j: (0, 0))]
    args = [q, k_new, v_new, cache_k, cache_v, cc, cn]
    aliases = {}
    if prev_out is not None:
        any_spec = pl.BlockSpec(memory_space=pl.ANY)
        in_specs += [any_spec, any_spec]
        args += list(prev_out)
        aliases = {7: 1, 8: 2}
    cache_sds = jax.ShapeDtypeStruct(cache_k.shape, F32)
    return pl.pallas_call(
        functools.partial(_attn_sample_kernel, steps=steps, aliased=prev_out is not None),
        grid=(batch, head_split),
        in_specs=in_specs,
        out_specs=[q_spec, cache_spec, cache_spec],
        out_shape=[jax.ShapeDtypeStruct(q.shape, F32), cache_sds, cache_sds],
        input_output_aliases=aliases,
        compiler_params=_cparams("arbitrary", "arbitrary"),
        name="attn_sample",
    )(*args)


def _outproj(oa_ref, ob_ref, oc_ref, wout_ref, x_ref):
    na, nb = oa_ref.shape[1], ob_ref.shape[1]
    mixed = (_dot(oa_ref[...].astype(BF16), wout_ref[0:na, :])
             + _dot(ob_ref[...].astype(BF16), wout_ref[na:na + nb, :])
             + _dot(oc_ref[...].astype(BF16), wout_ref[na + nb:, :]))
    return x_ref[...] + mixed


def _rmsnorm(x, g):
    return (x * lax.rsqrt(jnp.mean(x * x, axis=-1, keepdims=True) + EPS)) * g


def _mix_ffn_kernel(x_ref, oa_ref, ob_ref, oc_ref, wout_ref, g_ref, wg_ref, wu_ref, wd_ref, o_ref):
    x2 = _outproj(oa_ref, ob_ref, oc_ref, wout_ref, x_ref)
    h = _rmsnorm(x2, g_ref[...]).astype(BF16)
    act = _silu(_dot(h, wg_ref[...])) * _dot(h, wu_ref[...])
    o_ref[...] = x2 + _dot(act.astype(BF16), wd_ref[...])


def _mix_ffn(x, oa, ob, oc, wout, g, wg, wu, wd, tm):
    t = x.shape[0]
    d_ff = wg.shape[1]

    def rows(width):
        return pl.BlockSpec((tm, width), lambda i: (i, 0))

    def const(shape):
        return pl.BlockSpec(shape, lambda i: (0, 0), pipeline_mode=pl.Buffered(1))

    return pl.pallas_call(
        _mix_ffn_kernel,
        grid=(t // tm,),
        in_specs=[rows(D_MODEL), rows(oa.shape[1]), rows(ob.shape[1]), rows(oc.shape[1]),
                  const((D_MODEL, D_MODEL)), const((1, D_MODEL)),
                  const((D_MODEL, d_ff)), const((D_MODEL, d_ff)), const((d_ff, D_MODEL))],
        out_specs=rows(D_MODEL),
        out_shape=jax.ShapeDtypeStruct((t, D_MODEL), F32),
        compiler_params=_cparams("parallel"),
        name="mix_ffn",
    )(x, oa, ob, oc, wout, g, wg, wu, wd)


def _mix_router_kernel(x_ref, oa_ref, ob_ref, oc_ref, wout_ref, g_ref, wr_ref, x2_ref, h_ref, cw_ref):
    x2 = _outproj(oa_ref, ob_ref, oc_ref, wout_ref, x_ref)
    x2_ref[...] = x2
    h = _rmsnorm(x2, g_ref[...])
    h_ref[...] = h.astype(BF16)
    logits = jnp.dot(h, wr_ref[...], preferred_element_type=F32, precision=lax.Precision.HIGHEST)
    lane = lax.broadcasted_iota(jnp.int32, logits.shape, 1)
    logits = jnp.where(lane < N_EXPERTS, logits, -jnp.inf)
    v1 = jnp.max(logits, axis=-1, keepdims=True)
    i1 = jnp.min(jnp.where(logits == v1, lane, LANES), axis=-1, keepdims=True)
    rest = jnp.where(lane == i1, -jnp.inf, logits)
    v2 = jnp.max(rest, axis=-1, keepdims=True)
    i2 = jnp.min(jnp.where(rest == v2, lane, LANES), axis=-1, keepdims=True)
    e2 = jnp.exp(v2 - v1)
    cw_ref[...] = jnp.where(lane == i1, 1.0 / (1.0 + e2), 0.0) + jnp.where(lane == i2, e2 / (1.0 + e2), 0.0)


def _mix_router(x, oa, ob, oc, wout, g, wr, tm):
    t = x.shape[0]

    def rows(width):
        return pl.BlockSpec((tm, width), lambda i: (i, 0))

    def const(shape):
        return pl.BlockSpec(shape, lambda i: (0, 0))

    return pl.pallas_call(
        _mix_router_kernel,
        grid=(t // tm,),
        in_specs=[rows(D_MODEL), rows(oa.shape[1]), rows(ob.shape[1]), rows(oc.shape[1]),
                  const((D_MODEL, D_MODEL)), const((1, D_MODEL)), const((D_MODEL, LANES))],
        out_specs=[rows(D_MODEL), rows(D_MODEL), rows(LANES)],
        out_shape=[jax.ShapeDtypeStruct((t, D_MODEL), F32), jax.ShapeDtypeStruct((t, D_MODEL), BF16),
                   jax.ShapeDtypeStruct((t, LANES), F32)],
        compiler_params=_cparams("parallel"),
        name="mix_router",
    )(x, oa, ob, oc, wout, g, wr)


def _moe_kernel(x2_ref, h_ref, cw_ref, wg_ref, wu_ref, wd_ref, o_ref):
    e = pl.program_id(1)

    @pl.when(e == 0)
    def _():
        o_ref[...] = x2_ref[...]

    h = h_ref[...]
    act = _silu(_dot(h, wg_ref[0])) * _dot(h, wu_ref[0])
    y = _dot(act.astype(BF16), wd_ref[0])
    lane = lax.broadcasted_iota(jnp.int32, cw_ref.shape, 1)
    w = jnp.sum(jnp.where(lane == e, cw_ref[...], 0.0), axis=-1, keepdims=True)
    o_ref[...] += w * y


def _moe(x2, h, cw, wg, wu, wd, tm):
    t = x2.shape[0]
    n_e, _, d_ff = wg.shape

    def rows(width):
        return pl.BlockSpec((tm, width), lambda i, e: (i, 0))

    return pl.pallas_call(
        _moe_kernel,
        grid=(t // tm, n_e),
        in_specs=[rows(D_MODEL), rows(D_MODEL), rows(LANES),
                  pl.BlockSpec((1, D_MODEL, d_ff), lambda i, e: (e, 0, 0)),
                  pl.BlockSpec((1, D_MODEL, d_ff), lambda i, e: (e, 0, 0)),
                  pl.BlockSpec((1, d_ff, D_MODEL), lambda i, e: (e, 0, 0))],
        out_specs=rows(D_MODEL),
        out_shape=jax.ShapeDtypeStruct((t, D_MODEL), F32),
        compiler_params=_cparams("parallel", "arbitrary"),
        name="moe",
    )(x2, h, cw, wg, wu, wd)


def _block_diag_ones(n, group):
    idx = np.arange(n) // group
    return jnp.asarray((idx[:, None] == idx[None, :]).astype(np.float32), dtype=BF16)


def _rearrange_w_in(w):
    offs = np.cumsum([0, 128, 128, 256, GLA_RANK, 256, 256, 256, 256, 256, 512, 512, 512])
    a_q, a_k, a_v, a_lr, a_g, b_q, b_f, b_i, b_g, c_q, c_k, c_v = [
        w[:, offs[n]:offs[n + 1]] for n in range(12)]
    lr = jnp.pad(a_lr, ((0, 0), (0, LANES - GLA_RANK)))
    return jnp.concatenate([c_v, a_q, a_k, a_v, a_g, b_q, b_f, b_i, b_g, c_q, c_k, lr], axis=1).astype(BF16)


def _tile_rows(seq, cap):
    t = cap
    while seq % t:
        t //= 2
    return t


def kernel(x_prompt, x_sample, state_gla, state_hgrn, cache_k_win, cache_v_win, norm_mix_g, w_in,
           gla_w_gate_up, gla_b_gate, gla_norm_g, hgrn_lb_logits, hgrn_norm_g, attn_q_norm_g,
           attn_k_norm_g, w_out, norm_ffn_g, ffn_w_gate, ffn_w_up, ffn_w_down, moe_w_router,
           moe_w_gate, moe_w_up, moe_w_down):
    depth = w_in.shape[0]
    bp, lp, _ = x_prompt.shape
    bs, ls, _ = x_sample.shape
    buf = cache_k_win.shape[2]
    width = ATT_HEADS * HEAD_DIM
    assert lp % (BAND * max(DILATIONS)) == 0 and lp % CHUNK == 0 and ls % CHUNK != 0 and ls == SUBLANES
    assert buf >= BAND * max(DILATIONS)

    lb_w = jax.nn.softmax(hgrn_lb_logits.astype(F32), axis=0)
    lower_bounds = jnp.cumsum(lb_w, axis=0) - lb_w[0]

    ones_att = _block_diag_ones(LANES, HEAD_DIM)
    ones_gla = _block_diag_ones(GLA_HEADS * GLA_DV, GLA_DV)
    ones_hgrn = _block_diag_ones(HGRN_HEADS * HGRN_DV, HGRN_DV)
    tm_p = _tile_rows(bp * lp, 256)
    tm_s = _tile_rows(bs * ls, 256)
    tables_p = _rope_tables(jnp.arange(lp, dtype=jnp.int32))
    tables_s = _rope_tables(jnp.tile(PAST_LEN + jnp.arange(ls, dtype=jnp.int32), tm_s // ls))
    tm_rope = _tile_rows(lp, 2048)
    ck = jnp.transpose(cache_k_win, (0, 1, 3, 4, 2))
    cv = jnp.transpose(cache_v_win, (0, 1, 3, 4, 2))

    def sample_heads(a):
        return a.reshape(bs, ls, ATT_HEADS, HEAD_DIM)

    def sample_tail(a):
        a = jnp.transpose(sample_heads(a), (0, 2, 3, 1))
        return jnp.pad(a, ((0, 0), (0, 0), (0, 0), (LANES - ls, 0)))

    xp = x_prompt.reshape(bp * lp, D_MODEL)
    xs = x_sample.reshape(bs * ls, D_MODEL)
    zero_gla = jnp.zeros((bp, GLA_HEADS, GLA_DK, GLA_DV), F32)
    zero_hgrn = jnp.zeros((bp, HGRN_HEADS, HGRN_DK, HGRN_DV), F32)
    gla_p, hgrn_p, kw_p, vw_p, gla_s, hgrn_s = [], [], [], [], [], []
    caches_s = None
    keep = min(buf, lp)

    for layer in range(depth):
        w_proj = _rearrange_w_in(w_in[layer])
        g_mix = norm_mix_g[layer].reshape(1, D_MODEL)
        wup = jnp.pad(gla_w_gate_up[layer], ((0, LANES - GLA_RANK), (0, 0))).astype(BF16)
        bias = gla_b_gate[layer].reshape(1, -1)
        gain_a = jnp.tile(gla_norm_g[layer], GLA_HEADS).reshape(1, -1)
        gain_b = jnp.tile(hgrn_norm_g[layer], HGRN_HEADS).reshape(1, -1)
        lb = lower_bounds[layer].reshape(1, -1)
        gq = jnp.tile(attn_q_norm_g[layer], 2).reshape(1, LANES)
        gk = jnp.tile(attn_k_norm_g[layer], 2).reshape(1, LANES)
        wout = w_out[layer].astype(BF16)
        g_ffn = norm_ffn_g[layer].reshape(1, D_MODEL)

        mixed = []
        for which, x, batch, seq, tm in (("p", xp, bp, lp, tm_p), ("s", xs, bs, ls, tm_s)):
            proj = _inproj(x, g_mix, w_proj, tm)
            if which == "p":
                qn, kn = _qkrope(proj, gq, gk, tables_p, ones_att, tm_rope, lp // tm_rope)
                s0_a, s0_b, block, chunk = zero_gla, zero_hgrn, min(seq, 4 * CHUNK), CHUNK
            else:
                qn, kn = _qkrope(proj, gq, gk, tables_s, ones_att, tm, 1)
                s0_a, s0_b, block, chunk = state_gla[layer], state_hgrn[layer], seq, seq
            o_a, s_a = _gated_linear("gla", proj, batch, seq, s0_a, gain_a, ones_gla, (wup, bias), block, chunk)
            o_b, s_b = _gated_linear("hgrn", proj, batch, seq, s0_b, gain_b, ones_hgrn, (lb,), block, chunk)
            if which == "p":
                o_c = _attn_prompt(qn, kn, proj, COL_CV, batch, seq)
                gla_p.append(s_a)
                hgrn_p.append(s_b)
                kw_p.append(kn.reshape(batch, seq, ATT_HEADS, HEAD_DIM)[:, seq - keep:])
                vw_p.append(proj[:, COL_CV:COL_CV + width].reshape(batch, seq, ATT_HEADS, HEAD_DIM)[:, seq - keep:])
            else:
                o_c, ck_new, cv_new = _attn_sample(
                    jnp.transpose(sample_heads(qn), (0, 2, 1, 3)), sample_tail(kn),
                    sample_tail(proj[:, COL_CV:COL_CV + width]), ck, cv, layer, caches_s, seq, 2)
                o_c = jnp.transpose(o_c, (0, 2, 1, 3)).reshape(batch * seq, width)
                caches_s = (ck_new, cv_new)
                gla_s.append(s_a)
                hgrn_s.append(s_b)
            mixed.append((o_a, o_b, o_c))

        j = layer // 2
        new_x = []
        for (o_a, o_b, o_c), x, tm in ((mixed[0], xp, tm_p), (mixed[1], xs, tm_s)):
            if layer % 2 == 0:
                new_x.append(_mix_ffn(x, o_a, o_b, o_c, wout, g_ffn, ffn_w_gate[j].astype(BF16),
                                      ffn_w_up[j].astype(BF16), ffn_w_down[j].astype(BF16), tm))
            else:
                wr = jnp.pad(moe_w_router[j], ((0, 0), (0, LANES - N_EXPERTS)))
                x2, h, cw = _mix_router(x, o_a, o_b, o_c, wout, g_ffn, wr, tm)
                new_x.append(_moe(x2, h, cw, moe_w_gate[j].astype(BF16), moe_w_up[j].astype(BF16),
                                  moe_w_down[j].astype(BF16), _tile_rows(x.shape[0], 512)))
        xp, xs = new_x

    return (xp.reshape(bp, lp, D_MODEL), xs.reshape(bs, ls, D_MODEL),
            jnp.stack(gla_p), jnp.stack(hgrn_p), jnp.stack(kw_p), jnp.stack(vw_p),
            jnp.stack(gla_s), jnp.stack(hgrn_s),
            jnp.transpose(caches_s[0], (0, 1, 4, 2, 3)), jnp.transpose(caches_s[1], (0, 1, 4, 2, 3)))
```

```python
import functools
import math

import numpy as np
import jax
import jax.numpy as jnp
from jax import lax
from jax.experimental import pallas as pl
from jax.experimental.pallas import tpu as pltpu

F32 = jnp.float32
BF16 = jnp.bfloat16

D_MODEL = 1024
HEAD_DIM = 64
GLA_HEADS = 4
GLA_DK = 32
GLA_DV = 64
GLA_RANK = 16
GLA_TAU = 16.0
HGRN_HEADS = 4
HGRN_DK = 64
HGRN_DV = 64
ATT_HEADS = 8
DILATIONS = (1, 4, 16)
BAND = 128
ROT_DIM = 16
ROPE_THETA = 500000.0
CHUNK = 64
N_EXPERTS = 8
EPS = 1e-6
PAST_LEN = 8192
NEG = -1e30

LANES = 128
SUBLANES = 8
VMEM_LIMIT = 56 * 1024 * 1024
ROW_TILES = D_MODEL // LANES
SAMPLE_SEQS = 16
MOE_TILE = 256

COL_CV = 0
COL_AQ, COL_AK, COL_AV, COL_AG = 512, 640, 768, 1024
COL_BQ, COL_BF, COL_BI, COL_BG = 1280, 1536, 1792, 2048
COL_CQ, COL_CK = 2304, 2816
COL_LR = 3328
N_PROJ = 3456


def _cparams(*sem):
    return pltpu.CompilerParams(dimension_semantics=sem, vmem_limit_bytes=VMEM_LIMIT)


def _sigmoid(x):
    return 1.0 / (1.0 + jnp.exp(-x))


def _silu(x):
    return x * _sigmoid(x)


def _log_sigmoid(x):
    return -(jnp.maximum(-x, 0.0) + jnp.log(1.0 + jnp.exp(-jnp.abs(x))))


def _dot(a, b):
    return jnp.dot(a, b, preferred_element_type=F32)


def _dot_nt(a, b):
    return lax.dot_general(a, b, (((1,), (1,)), ((), ())), preferred_element_type=F32)


def _dot_tn(a, b):
    return lax.dot_general(a, b, (((0,), (0,)), ((), ())), preferred_element_type=F32)


def _group_sum(x, ones_bd):
    hi = x.astype(BF16)
    lo = (x - hi.astype(F32)).astype(BF16)
    return _dot(hi, ones_bd) + _dot(lo, ones_bd)


def _inproj_kernel(x_ref, g_ref, w_ref, o_ref):
    x = x_ref[...]
    ms = jnp.mean(x * x, axis=-1, keepdims=True)
    h = (x * lax.rsqrt(ms + EPS)) * g_ref[...]
    o_ref[...] = _dot(h.astype(BF16), w_ref[...])


def _inproj(x, g, w, tm):
    t = x.shape[0]
    return pl.pallas_call(
        _inproj_kernel,
        grid=(t // tm,),
        in_specs=[pl.BlockSpec((tm, D_MODEL), lambda i: (i, 0)),
                  pl.BlockSpec((1, D_MODEL), lambda i: (0, 0)),
                  pl.BlockSpec((D_MODEL, N_PROJ), lambda i: (0, 0))],
        out_specs=pl.BlockSpec((tm, N_PROJ), lambda i: (i, 0)),
        out_shape=jax.ShapeDtypeStruct((t, N_PROJ), F32),
        compiler_params=_cparams("parallel"),
        name="inproj",
    )(x, g, w)


def _qkrope_kernel(q_ref, k_ref, *rest, window):
    if window:
        v_ref, gq_ref, gk_ref, cos_ref, sa_ref, sb_ref, ones_ref, qo_ref, ko_ref, kt_ref, vt_ref = rest
    else:
        gq_ref, gk_ref, cos_ref, sa_ref, sb_ref, ones_ref, qo_ref, ko_ref = rest
    cos, sa, sb = cos_ref[...], sa_ref[...], sb_ref[...]
    ones_bd = ones_ref[...]

    def norm_rope(x, g):
        ms = _group_sum(x * x, ones_bd) * (1.0 / HEAD_DIM)
        y = (x * lax.rsqrt(ms + EPS)) * g
        up = pltpu.roll(y, LANES - ROT_DIM // 2, axis=1)
        dn = pltpu.roll(y, ROT_DIM // 2, axis=1)
        return y * cos + up * sa + dn * sb

    qo_ref[...] = norm_rope(q_ref[...], gq_ref[...]) * (HEAD_DIM ** -0.5)
    k = norm_rope(k_ref[...], gk_ref[...])
    ko_ref[...] = k
    if window:
        @pl.when(pl.program_id(2) == pl.num_programs(2) - 1)
        def _():
            kt_ref[0] = k.T.reshape(kt_ref.shape[1:])
            vt_ref[0] = v_ref[...].T.reshape(vt_ref.shape[1:])


def _rope_tables(pos):
    half = ROT_DIM // 2
    inv_freq = ROPE_THETA ** (-jnp.arange(0, ROT_DIM, 2, dtype=F32) / ROT_DIM)
    ang = pos.astype(F32)[:, None] * inv_freq[None, :]
    cos, sin = jnp.cos(ang), jnp.sin(ang)
    n = pos.shape[0]
    rest = HEAD_DIM - ROT_DIM
    c = jnp.concatenate([cos, cos, jnp.ones((n, rest), F32)], axis=1)
    sa = jnp.concatenate([-sin, jnp.zeros((n, half + rest), F32)], axis=1)
    sb = jnp.concatenate([jnp.zeros((n, half), F32), sin, jnp.zeros((n, rest), F32)], axis=1)
    return tuple(jnp.tile(a, (1, 2)) for a in (c, sa, sb))


def _qkrope(proj, gq, gk, tables, ones_bd, tm, table_blocks):
    t = proj.shape[0]
    cq, ck = COL_CQ // LANES, COL_CK // LANES
    tab_spec = pl.BlockSpec((tm, LANES), lambda i, j: (i % table_blocks, 0))
    row_spec = pl.BlockSpec((1, LANES), lambda i, j: (0, 0))
    out_spec = pl.BlockSpec((tm, LANES), lambda i, j: (i, j))
    out_sds = jax.ShapeDtypeStruct((t, ATT_HEADS * HEAD_DIM), F32)
    return pl.pallas_call(
        functools.partial(_qkrope_kernel, window=False),
        grid=(t // tm, ATT_HEADS // 2),
        in_specs=[pl.BlockSpec((tm, LANES), lambda i, j: (i, cq + j)),
                  pl.BlockSpec((tm, LANES), lambda i, j: (i, ck + j)),
                  row_spec, row_spec, tab_spec, tab_spec, tab_spec,
                  pl.BlockSpec((LANES, LANES), lambda i, j: (0, 0))],
        out_specs=[out_spec, out_spec],
        out_shape=[out_sds, out_sds],
        compiler_params=_cparams("parallel", "parallel"),
        name="qkrope",
    )(proj, proj, gq, gk, *tables, ones_bd)


def _qkrope_window(proj, gq, gk, tables, ones_bd, batch, seq, window):
    cq, ck, cv = COL_CQ // LANES, COL_CK // LANES, COL_CV // LANES
    nt = seq // window

    def tok(col):
        return pl.BlockSpec((window, LANES), lambda b, j, s: (b * nt + s, col + j))

    tab_spec = pl.BlockSpec((window, LANES), lambda b, j, s: (s, 0))
    row_spec = pl.BlockSpec((1, LANES), lambda b, j, s: (0, 0))
    win_spec = pl.BlockSpec((1, 2, HEAD_DIM, window), lambda b, j, s: (b, j, 0, 0))
    out_sds = jax.ShapeDtypeStruct((batch * seq, ATT_HEADS * HEAD_DIM), F32)
    win_sds = jax.ShapeDtypeStruct((batch, ATT_HEADS, HEAD_DIM, window), F32)
    return pl.pallas_call(
        functools.partial(_qkrope_kernel, window=True),
        grid=(batch, ATT_HEADS // 2, nt),
        in_specs=[tok(cq), tok(ck), tok(cv), row_spec, row_spec, tab_spec, tab_spec, tab_spec,
                  pl.BlockSpec((LANES, LANES), lambda b, j, s: (0, 0))],
        out_specs=[tok(0), tok(0), win_spec, win_spec],
        out_shape=[out_sds, out_sds, win_sds, win_sds],
        compiler_params=_cparams("parallel", "parallel", "arbitrary"),
        name="qkrope_window",
    )(proj, proj, proj, gq, gk, *tables, ones_bd)


def _segment_scans(g, chunk):
    rows, hk = g.shape
    n_chunks = rows // chunk
    row = lax.broadcasted_iota(jnp.int32, g.shape, 0)

    def at(x, shift):
        return pltpu.roll(x, shift % rows, axis=0)

    zero = jnp.zeros_like(g)
    c = {1: g, 2: g + jnp.where(row % 2 == 1, at(g, 1), zero)}
    c[4] = c[2] + jnp.where(row % 4 == 2, at(c[2], 1), zero) + jnp.where(row % 4 == 3, at(c[2], 2), zero)
    d = {1: zero, 2: jnp.where(row % 2 == 0, at(g, -1), zero)}
    c4_end = jnp.where(row % 4 == 0, at(c[4], -3),
                       jnp.where(row % 4 == 1, at(c[4], -2),
                                 jnp.where(row % 4 == 2, at(c[4], -1), c[4])))
    d[4] = c4_end - c[4]
    tiles = chunk // SUBLANES
    c4_t = c[4].reshape(n_chunks * tiles, SUBLANES, hk)
    row_t = row.reshape(n_chunks * tiles, SUBLANES, hk)
    c8_t = c4_t + jnp.where(row_t % 8 >= 4, c4_t[:, 3:4, :], jnp.zeros_like(c4_t))
    ends = c8_t[:, SUBLANES - 1:SUBLANES, :]
    prefix = []
    for ci in range(n_chunks):
        acc = [jnp.zeros((1, 1, hk), F32)]
        for j in range(tiles):
            acc.append(acc[-1] + ends[ci * tiles + j:ci * tiles + j + 1])
        prefix.append(acc)

    def per_tile(pick):
        return jnp.concatenate([prefix[ci][pick(j)] for ci in range(n_chunks) for j in range(tiles)], axis=0)

    b_t = c8_t + per_tile(lambda j: j)
    width = 1
    while width * SUBLANES < chunk:
        m = width * SUBLANES
        c[m] = (b_t - per_tile(lambda j: (j // width) * width)).reshape(rows, hk)
        d[m] = (per_tile(lambda j: (j // width + 1) * width) - b_t).reshape(rows, hk)
        width *= 2
    b_rest = (per_tile(lambda j: tiles) - b_t).reshape(rows, hk)
    b_last = [prefix[ci][tiles].reshape(1, hk) for ci in range(n_chunks)]
    return c, d, b_t.reshape(rows, hk), b_rest, b_last


def _pair_masks(chunk, copies):
    t = lax.broadcasted_iota(jnp.int32, (copies * chunk, chunk), 0) % chunk
    s = lax.broadcasted_iota(jnp.int32, (copies * chunk, chunk), 1)
    masks = {0: t == s}
    m = 1
    while m < chunk:
        masks[m] = (t // (2 * m) == s // (2 * m)) & (t % (2 * m) >= m) & (s % (2 * m) < m)
        m *= 2
    return masks


def _gl_block(q, k, g, v, st, heads, dk, dv, chunk):
    rows = q.shape[0]
    hk, hv = heads * dk, heads * dv
    stacked_masks = _pair_masks(chunk, heads)
    levels = sorted(stacked_masks)
    c, d, b, b_rest, b_last = _segment_scans(g, chunk)
    k_bf = k.astype(BF16)
    q_lvl = {0: q.astype(BF16)}
    k_lvl = {0: k_bf, 1: k_bf}
    for m in levels[1:]:
        q_lvl[m] = (q * jnp.exp(c[m])).astype(BF16)
        if m > 1:
            k_lvl[m] = (k * jnp.exp(d[m])).astype(BF16)
    q_in = (q * jnp.exp(b)).astype(BF16)
    k_out = (k * jnp.exp(b_rest)).astype(BF16)
    v_bf = v.astype(BF16)
    lane_k = lax.broadcasted_iota(jnp.int32, (chunk, hk), 1) // dk
    lane_v = lax.broadcasted_iota(jnp.int32, (chunk, hv), 1) // dv
    same_head = (lax.broadcasted_iota(jnp.int32, (hv, hk), 0) // dv
                 == lax.broadcasted_iota(jnp.int32, (hv, hk), 1) // dk)
    zero_k = jnp.zeros((chunk, hk), BF16)
    zero_v = jnp.zeros((chunk, hv), BF16)
    o_intra, st_inc = [], []
    for ci in range(rows // chunk):
        r = slice(ci * chunk, (ci + 1) * chunk)
        a = jnp.zeros((heads * chunk, chunk), F32)
        for m in levels:
            q_heads = jnp.concatenate([jnp.where(lane_k == h, q_lvl[m][r], zero_k) for h in range(heads)], axis=0)
            a = jnp.where(stacked_masks[m], _dot_nt(q_heads, k_lvl[m][r]), a)
        a = a.astype(BF16)
        o = None
        for h in range(heads):
            part = _dot(a[h * chunk:(h + 1) * chunk], jnp.where(lane_v == h, v_bf[r], zero_v))
            o = part if o is None else o + part
        o_intra.append(o)
        st_inc.append(jnp.where(same_head, _dot_tn(v_bf[r], k_out[r]), 0.0))
    independent = isinstance(st, list)
    outs, finals = [], []
    for ci in range(rows // chunk):
        r = slice(ci * chunk, (ci + 1) * chunk)
        cur = st[ci] if independent else st
        outs.append(o_intra[ci] + _dot_nt(q_in[r], cur.astype(BF16)))
        nxt = cur * jnp.exp(b_last[ci]) + st_inc[ci]
        if independent:
            finals.append(nxt)
        else:
            st = nxt
    return (outs[0] if len(outs) == 1 else jnp.concatenate(outs, axis=0)), (finals if independent else st)


def _head_norm_gate(o, gain, gate, ones_bd, dv):
    ms = _group_sum(o * o, ones_bd) * (1.0 / dv)
    return ((o * lax.rsqrt(ms + EPS)) * gain) * _silu(gate)


def _gl_kernel(*refs, mode, heads, dk, dv, chunk):
    if mode == "gla":
        (q_ref, k_ref, v_ref, gate_ref, lr_ref, wup_ref, bias_ref, gain_ref, s0_ref, ones_ref,
         o_ref, sT_ref, st_ref) = refs
        q = q_ref[...] * (dk ** -0.5)
        k = k_ref[...]
        g = _log_sigmoid(_dot(lr_ref[...].astype(BF16), wup_ref[...]) + bias_ref[...]) * (1.0 / GLA_TAU)
    else:
        (q_ref, f_ref, v_ref, gate_ref, lb_ref, gain_ref, s0_ref, ones_ref,
         o_ref, sT_ref, st_ref) = refs
        lb = lb_ref[...]
        q = _silu(q_ref[...])
        z = f_ref[...]
        g = jnp.log(lb + (1.0 - lb) * _sigmoid(z))
        k = (1.0 - lb) * _sigmoid(-z)
    seqs = s0_ref.shape[0]
    if seqs > 1:
        o, finals = _gl_block(q, k, g, v_ref[...], [s0_ref[n] for n in range(seqs)], heads, dk, dv, chunk)
        for n in range(seqs):
            sT_ref[n] = finals[n]
    else:
        i = pl.program_id(1)

        @pl.when(i == 0)
        def _():
            st_ref[...] = s0_ref[0]

        o, st = _gl_block(q, k, g, v_ref[...], st_ref[...], heads, dk, dv, chunk)
        st_ref[...] = st

        @pl.when(i == pl.num_programs(1) - 1)
        def _():
            sT_ref[0] = st
    o_ref[...] = _head_norm_gate(o, gain_ref[...], gate_ref[...], ones_ref[...], dv)


def _embed_state(s):
    b, h, k, v = s.shape
    eye = jnp.eye(h, dtype=s.dtype)
    return jnp.einsum("bhkv,hg->bhvgk", s, eye).reshape(b, h * v, h * k)


def _extract_state(st, h, k, v):
    b = st.shape[0]
    st = st.reshape(b, h, v, h, k)
    return jnp.stack([st[:, i, :, i, :] for i in range(h)], axis=1).transpose(0, 1, 3, 2)


def _gated_linear(mode, proj, batch, seq, s0, gain, ones_bd, extra, block_tokens, chunk, seqs=1):
    assert seqs == 1 or (block_tokens == seq == chunk and batch % seqs == 0)
    heads, dk, dv = (GLA_HEADS, GLA_DK, GLA_DV) if mode == "gla" else (HGRN_HEADS, HGRN_DK, HGRN_DV)
    hk, hv = heads * dk, heads * dv
    nb = seq // block_tokens
    tb = block_tokens * seqs

    def cols(width, col):
        return pl.BlockSpec((tb, width), lambda b, i: (b * nb + i, col // width))

    def const(shape):
        return pl.BlockSpec(shape, lambda b, i: (0,) * len(shape))

    state_spec = pl.BlockSpec((seqs, hv, hk), lambda b, i: (b, 0, 0))
    if mode == "gla":
        wup, bias = extra
        in_specs = [cols(hk, COL_AQ), cols(hk, COL_AK), cols(hv, COL_AV), cols(hv, COL_AG), cols(LANES, COL_LR),
                    const((LANES, hk)), const((1, hk)), const((1, hv)), state_spec, const((hv, hv))]
        args = (proj, proj, proj, proj, proj, wup, bias, gain, _embed_state(s0), ones_bd)
    else:
        (lb,) = extra
        in_specs = [cols(hk, COL_BQ), cols(hk, COL_BF), cols(hv, COL_BI), cols(hv, COL_BG),
                    const((1, hk)), const((1, hv)), state_spec, const((hv, hv))]
        args = (proj, proj, proj, proj, lb, gain, _embed_state(s0), ones_bd)
    o, st = pl.pallas_call(
        functools.partial(_gl_kernel, mode=mode, heads=heads, dk=dk, dv=dv, chunk=chunk),
        grid=(batch // seqs, nb),
        in_specs=in_specs,
        out_specs=[pl.BlockSpec((tb, hv), lambda b, i: (b * nb + i, 0)), state_spec],
        out_shape=[jax.ShapeDtypeStruct((batch * seq, hv), F32),
                   jax.ShapeDtypeStruct((batch, hv, hk), F32)],
        scratch_shapes=[pltpu.VMEM((hv, hk), F32)],
        compiler_params=_cparams("parallel", "arbitrary"),
        name=mode + "_scan",
    )(*args)
    return o, _extract_state(st, heads, dk, dv)


def _attn_prompt_kernel(q_ref, k_ref, v_ref, o_ref, qr_ref, kt_ref, kr_ref, vt_ref, vr_ref,
                        m_ref, l_ref, acc_ref, *, seq):
    n_res = max(DILATIONS)
    per_res = seq // n_res
    head0 = lax.broadcasted_iota(jnp.int32, (per_res, LANES), 1) < HEAD_DIM
    for r in range(n_res):
        tok = slice(r * per_res, (r + 1) * per_res)
        strided = pl.ds(r, per_res, stride=n_res)
        k_t, v_t = k_ref[tok, :], v_ref[tok, :]
        k_r, v_r = k_ref[strided, :], v_ref[strided, :]
        qr_ref[tok, :] = q_ref[strided, :]
        for h in range(2):
            mine = head0 if h == 0 else ~head0
            kt_ref[h, tok, :] = jnp.where(mine, k_t, 0.0).astype(BF16)
            kr_ref[h, tok, :] = jnp.where(mine, k_r, 0.0).astype(BF16)
            vt_ref[h, tok, :] = jnp.where(mine, v_t, 1.0).astype(BF16)
            vr_ref[h, tok, :] = jnp.where(mine, v_r, 1.0).astype(BF16)
    m_ref[...] = jnp.full(m_ref.shape, NEG, F32)
    l_ref[...] = jnp.zeros(l_ref.shape, F32)
    acc_ref[...] = jnp.zeros(acc_ref.shape, F32)

    head0 = lax.broadcasted_iota(jnp.int32, (BAND, LANES), 1) < HEAD_DIM
    qi = lax.broadcasted_iota(jnp.int32, (BAND, 2 * BAND), 0)
    ki = lax.broadcasted_iota(jnp.int32, (BAND, 2 * BAND), 1)

    for dil in DILATIONS:
        n_runs = n_res // dil
        run = SUBLANES * dil
        keys_by_token = dil == 1

        def seq_index(u, n_runs=n_runs, run=run):
            return n_runs * (u % run) + u // run

        kj = ki if keys_by_token else seq_index(ki % BAND) + BAND * (ki // BAND)
        rel = BAND + seq_index(qi) - kj
        in_band = (rel >= 0) & (rel <= BAND)
        bias_prev = jnp.where(in_band, 0.0, NEG).astype(F32)
        bias_first = jnp.where(in_band & (ki >= BAND), 0.0, NEG).astype(F32)

        def body(idx, carry, dil=dil, n_runs=n_runs, run=run, keys_by_token=keys_by_token,
                 bias_prev=bias_prev, bias_first=bias_first):
            res = idx % dil
            blk = idx // dil
            starts = [pl.multiple_of((res + dil * c) * per_res + run * blk, run) for c in range(n_runs)]
            prevs = [pl.multiple_of((res + dil * c) * per_res + run * jnp.maximum(blk - 1, 0), run)
                     for c in range(n_runs)]

            def gather(ref, offs, *lead):
                parts = [ref[lead + (pl.ds(o, run), slice(None))] for o in offs]
                return parts[0] if len(parts) == 1 else jnp.concatenate(parts, axis=0)

            def scatter(ref, offs, val, *lead):
                for c, o in enumerate(offs):
                    ref[lead + (pl.ds(o, run), slice(None))] = val[c * run:(c + 1) * run, :]

            q = gather(qr_ref, starts).astype(BF16)
            bias = jnp.where(blk > 0, bias_prev, bias_first)
            alphas, pvs = [], []
            for h in range(2):
                if keys_by_token:
                    cur = pl.multiple_of(blk * BAND, BAND)
                    prev = pl.multiple_of(jnp.maximum(blk - 1, 0) * BAND, BAND)
                    k2 = jnp.concatenate([kt_ref[h, pl.ds(prev, BAND), :], kt_ref[h, pl.ds(cur, BAND), :]], axis=0)
                    v2 = jnp.concatenate([vt_ref[h, pl.ds(prev, BAND), :], vt_ref[h, pl.ds(cur, BAND), :]], axis=0)
                else:
                    k2 = jnp.concatenate([gather(kr_ref, prevs, h), gather(kr_ref, starts, h)], axis=0)
                    v2 = jnp.concatenate([gather(vr_ref, prevs, h), gather(vr_ref, starts, h)], axis=0)
                s = _dot_nt(q, k2) + bias
                m_prev = gather(m_ref, starts, h)
                m_new = jnp.maximum(m_prev, jnp.max(s, axis=-1, keepdims=True))
                scatter(m_ref, starts, m_new, h)
                p = jnp.exp(s - jnp.concatenate([m_new, m_new], axis=1))
                alphas.append(jnp.exp(m_prev - m_new))
                pvs.append(_dot(p.astype(BF16), v2))
            scatter(l_ref, starts, jnp.where(head0, alphas[1], alphas[0]) * gather(l_ref, starts)
                    + jnp.where(head0, pvs[1], pvs[0]))
            scatter(acc_ref, starts, jnp.where(head0, alphas[0], alphas[1]) * gather(acc_ref, starts)
                    + jnp.where(head0, pvs[0], pvs[1]))
            return carry

        lax.fori_loop(0, seq // BAND, body, 0, unroll=8)

    for r in range(n_res):
        tok = slice(r * per_res, (r + 1) * per_res)
        o_ref[pl.ds(r, per_res, stride=n_res), :] = acc_ref[tok, :] / pltpu.roll(l_ref[tok, :], HEAD_DIM, axis=1)


def _attn_prompt(q, k, v_src, v_col, batch, seq):
    spec = pl.BlockSpec((seq, LANES), lambda b, j: (b, j))
    vc = v_col // LANES
    return pl.pallas_call(
        functools.partial(_attn_prompt_kernel, seq=seq),
        grid=(batch, ATT_HEADS // 2),
        in_specs=[spec, spec, pl.BlockSpec((seq, LANES), lambda b, j: (b, vc + j))],
        out_specs=spec,
        out_shape=jax.ShapeDtypeStruct((batch * seq, ATT_HEADS * HEAD_DIM), F32),
        scratch_shapes=[pltpu.VMEM((seq, LANES), F32),
                        pltpu.VMEM((2, seq, LANES), BF16), pltpu.VMEM((2, seq, LANES), BF16),
                        pltpu.VMEM((2, seq, LANES), BF16), pltpu.VMEM((2, seq, LANES), BF16),
                        pltpu.VMEM((2, seq, LANES), F32), pltpu.VMEM((seq, LANES), F32),
                        pltpu.VMEM((seq, LANES), F32)],
        compiler_params=_cparams("parallel", "parallel"),
        name="attn_prompt",
    )(q, k, v_src)


def _sample_counts(steps, buf):
    t = np.arange(steps)[:, None]

    def count(dist, live):
        cnt = np.zeros(dist.shape, np.float32)
        for dil in DILATIONS:
            cnt += (live & (dist >= 0) & (dist % dil == 0) & (dist <= BAND * dil)).astype(np.float32)
        return cnt

    lane = np.arange(LANES)[None, :]
    return (count(buf + t - np.arange(buf)[None, :], True),
            count(t - (lane - (LANES - steps)), lane >= LANES - steps))


def _attn_sample_kernel(q_ref, kn_ref, vn_ref, kc_ref, vc_ref, cc_ref, cn_ref, *rest, steps, aliased):
    if aliased:
        rest = rest[2:]
    o_ref, ko_ref, vo_ref = rest
    q = q_ref[0].astype(BF16)
    k_old, v_old = kc_ref[0, 0], vc_ref[0, 0]
    k_new, v_new = kn_ref[0], vn_ref[0]
    cc, cn = cc_ref[...][None], cn_ref[...][None]

    def scores(keys):
        return lax.dot_general(q, keys.astype(BF16), (((2,), (1,)), ((0,), (0,))), preferred_element_type=F32)

    def weighted(p, values):
        return lax.dot_general(p.astype(BF16), values.astype(BF16), (((2,), (2,)), ((0,), (0,))),
                               preferred_element_type=F32)

    s_old = jnp.where(cc > 0.0, scores(k_old), NEG)
    s_new = jnp.where(cn > 0.0, scores(k_new), NEG)
    m = jnp.maximum(jnp.max(s_old, axis=-1, keepdims=True), jnp.max(s_new, axis=-1, keepdims=True))
    p_old = cc * jnp.exp(s_old - m)
    p_new = cn * jnp.exp(s_new - m)
    den = jnp.sum(p_old, axis=-1, keepdims=True) + jnp.sum(p_new, axis=-1, keepdims=True)
    o_ref[0] = (weighted(p_old, v_old) + weighted(p_new, v_new)) / den

    buf = k_old.shape[-1]
    lane = lax.broadcasted_iota(jnp.int32, k_new.shape, 2)
    for old, new, out_ref in ((k_old, k_new, ko_ref), (v_old, v_new, vo_ref)):
        shifted = pltpu.roll(old, buf - steps, axis=2)
        out_ref[0, 0, :, :, 0:buf - LANES] = shifted[:, :, 0:buf - LANES]
        out_ref[0, 0, :, :, buf - LANES:buf] = jnp.where(lane >= LANES - steps, new, shifted[:, :, buf - LANES:buf])


def _attn_sample(q, k_new, v_new, cache_k, cache_v, layer, prev_out, steps, head_split):
    depth, batch, heads, hd, buf = cache_k.shape
    hb = heads // head_split
    cc, cn = (jnp.asarray(c) for c in _sample_counts(steps, buf))
    q_spec = pl.BlockSpec((1, hb, steps, hd), lambda b, j: (b, j, 0, 0))
    new_spec = pl.BlockSpec((1, hb, hd, LANES), lambda b, j: (b, j, 0, 0))
    cache_spec = pl.BlockSpec((1, 1, hb, hd, buf), lambda b, j: (layer, b, j, 0, 0))
    in_specs = [q_spec, new_spec, new_spec, cache_spec, cache_spec,
                pl.BlockSpec((steps, buf), lambda b, j: (0, 0)), pl.BlockSpec((steps, LANES), lambda b, j: (0, 0))]
    args = [q, k_new, v_new, cache_k, cache_v, cc, cn]
    aliases = {}
    if prev_out is not None:
        any_spec = pl.BlockSpec(memory_space=pl.ANY)
        in_specs += [any_spec, any_spec]
        args += list(prev_out)
        aliases = {7: 1, 8: 2}
    cache_sds = jax.ShapeDtypeStruct(cache_k.shape, F32)
    return pl.pallas_call(
        functools.partial(_attn_sample_kernel, steps=steps, aliased=prev_out is not None),
        grid=(batch, head_split),
        in_specs=in_specs,
        out_specs=[q_spec, cache_spec, cache_spec],
        out_shape=[jax.ShapeDtypeStruct(q.shape, F32), cache_sds, cache_sds],
        input_output_aliases=aliases,
        compiler_params=_cparams("arbitrary", "arbitrary"),
        name="attn_sample",
    )(*args)


def _outproj(oa_ref, ob_ref, oc_ref, wout_ref, x_ref):
    na, nb = oa_ref.shape[1], ob_ref.shape[1]
    mixed = (_dot(oa_ref[...].astype(BF16), wout_ref[0:na, :])
             + _dot(ob_ref[...].astype(BF16), wout_ref[na:na + nb, :])
             + _dot(oc_ref[...].astype(BF16), wout_ref[na + nb:, :]))
    return x_ref[...] + mixed


def _rmsnorm(x, g):
    return (x * lax.rsqrt(jnp.mean(x * x, axis=-1, keepdims=True) + EPS)) * g


def _mix_ffn_kernel(x_ref, oa_ref, ob_ref, oc_ref, wout_ref, g_ref, wg_ref, wu_ref, wd_ref, o_ref):
    x2 = _outproj(oa_ref, ob_ref, oc_ref, wout_ref, x_ref)
    h = _rmsnorm(x2, g_ref[...]).astype(BF16)
    act = _silu(_dot(h, wg_ref[...])) * _dot(h, wu_ref[...])
    o_ref[...] = x2 + _dot(act.astype(BF16), wd_ref[...])


def _mix_ffn(x, oa, ob, oc, wout, g, wg, wu, wd, tm):
    t = x.shape[0]
    d_ff = wg.shape[1]

    def rows(width):
        return pl.BlockSpec((tm, width), lambda i: (i, 0))

    def const(shape):
        return pl.BlockSpec(shape, lambda i: (0, 0), pipeline_mode=pl.Buffered(1))

    return pl.pallas_call(
        _mix_ffn_kernel,
        grid=(t // tm,),
        in_specs=[rows(D_MODEL), rows(oa.shape[1]), rows(ob.shape[1]), rows(oc.shape[1]),
                  const((D_MODEL, D_MODEL)), const((1, D_MODEL)),
                  const((D_MODEL, d_ff)), const((D_MODEL, d_ff)), const((d_ff, D_MODEL))],
        out_specs=rows(D_MODEL),
        out_shape=jax.ShapeDtypeStruct((t, D_MODEL), F32),
        compiler_params=_cparams("parallel"),
        name="mix_ffn",
    )(x, oa, ob, oc, wout, g, wg, wu, wd)


def _mix_router_kernel(x_ref, oa_ref, ob_ref, oc_ref, wout_ref, g_ref, wr_ref, x2_ref, h_ref, idx_ref, gate_ref):
    tm = x_ref.shape[0]
    x2 = _outproj(oa_ref, ob_ref, oc_ref, wout_ref, x_ref)
    x2_ref[...] = x2
    h = _rmsnorm(x2, g_ref[...])
    for s in range(D_MODEL // LANES):
        h_ref[pl.ds(s, tm, stride=D_MODEL // LANES), :] = h[:, s * LANES:(s + 1) * LANES]
    logits = jnp.dot(h, wr_ref[...], preferred_element_type=F32, precision=lax.Precision.HIGHEST)
    lane = lax.broadcasted_iota(jnp.int32, logits.shape, 1)
    logits = jnp.where(lane < N_EXPERTS, logits, -jnp.inf)
    v1 = jnp.max(logits, axis=-1, keepdims=True)
    i1 = jnp.min(jnp.where(logits == v1, lane, LANES), axis=-1, keepdims=True)
    rest = jnp.where(lane == i1, -jnp.inf, logits)
    v2 = jnp.max(rest, axis=-1, keepdims=True)
    i2 = jnp.min(jnp.where(rest == v2, lane, LANES), axis=-1, keepdims=True)
    e2 = jnp.exp(v2 - v1)
    idx_ref[...] = jnp.where(lane == 0, i1, jnp.where(lane == 1, i2, 0))
    gate_ref[...] = jnp.where(lane == 0, 1.0 / (1.0 + e2), jnp.where(lane == 1, e2 / (1.0 + e2), 0.0))


def _mix_router(x, oa, ob, oc, wout, g, wr, tm):
    t = x.shape[0]

    def rows(width):
        return pl.BlockSpec((tm, width), lambda i: (i, 0))

    def const(shape):
        return pl.BlockSpec(shape, lambda i: (0, 0))

    return pl.pallas_call(
        _mix_router_kernel,
        grid=(t // tm,),
        in_specs=[rows(D_MODEL), rows(oa.shape[1]), rows(ob.shape[1]), rows(oc.shape[1]),
                  const((D_MODEL, D_MODEL)), const((1, D_MODEL)), const((D_MODEL, LANES))],
        out_specs=[rows(D_MODEL), pl.BlockSpec((tm * ROW_TILES, LANES), lambda i: (i, 0)), rows(LANES), rows(LANES)],
        out_shape=[jax.ShapeDtypeStruct((t, D_MODEL), F32), jax.ShapeDtypeStruct((t * ROW_TILES, LANES), F32),
                   jax.ShapeDtypeStruct((t, LANES), jnp.int32), jax.ShapeDtypeStruct((t, LANES), F32)],
        compiler_params=_cparams("parallel"),
        name="mix_router",
    )(x, oa, ob, oc, wout, g, wr)


def _route(idx, tile, n_tiles):
    t = idx.shape[0]
    expert = jnp.concatenate([idx[:, 0], idx[:, 1]])
    onehot = (expert[:, None] == jnp.arange(N_EXPERTS, dtype=jnp.int32)[None, :]).astype(jnp.int32)
    csum = jnp.cumsum(onehot, axis=0)
    rank = jnp.take_along_axis(csum, expert[:, None], axis=1)[:, 0] - 1
    tiles_e = (csum[-1] + tile - 1) // tile
    tile_end = jnp.cumsum(tiles_e)
    dest = (tile_end - tiles_e)[expert] * tile + rank
    tile_ids = jnp.arange(n_tiles, dtype=jnp.int32)
    tile_expert = jnp.minimum(jnp.sum((tile_end[None, :] <= tile_ids[:, None]).astype(jnp.int32), axis=1),
                              N_EXPERTS - 1)
    rows = n_tiles * tile
    a = jnp.arange(2 * t, dtype=jnp.int32)
    src = jnp.zeros((rows,), jnp.int32).at[dest].set(a % t)
    used = jnp.zeros((rows,), jnp.int32).at[dest].set(1)
    spare = 2 * t + jnp.cumsum(1 - used) - 1
    dst = jnp.where(used == 1, jnp.zeros((rows,), jnp.int32).at[dest].set(a), spare).astype(jnp.int32)
    return tile_expert, src, dst


def _moe_expert_kernel(te_ref, src_ref, dst_ref, h_hbm, wg_ref, wu_ref, wd_ref, y_hbm,
                       xbuf, ybuf, gsem, ssem, *, tile):
    del te_ref
    j = pl.program_id(0)
    nj = pl.num_programs(0)
    slot = j % 2

    def gather_row(jj, sl, r):
        tok = src_ref[jj * tile + r]
        return pltpu.make_async_copy(h_hbm.at[pl.ds(pl.multiple_of(tok * ROW_TILES, ROW_TILES), ROW_TILES), :],
                                     xbuf.at[sl, pl.ds(pl.multiple_of(r * ROW_TILES, ROW_TILES), ROW_TILES), :],
                                     gsem.at[sl])

    def scatter_row(jj, sl, r):
        row = dst_ref[jj * tile + r]
        return pltpu.make_async_copy(ybuf.at[sl, pl.ds(pl.multiple_of(r * ROW_TILES, ROW_TILES), ROW_TILES), :],
                                     y_hbm.at[pl.ds(pl.multiple_of(row * ROW_TILES, ROW_TILES), ROW_TILES), :],
                                     ssem.at[sl])

    def each_row(fn):
        def body(r, carry):
            fn(r)
            return carry
        lax.fori_loop(0, tile, body, 0, unroll=8)

    @pl.when(j == 0)
    def _():
        each_row(lambda r: gather_row(0, 0, r).start())

    @pl.when(j + 1 < nj)
    def _():
        each_row(lambda r: gather_row(j + 1, 1 - slot, r).start())

    each_row(lambda r: gather_row(j, slot, r).wait())

    @pl.when(j >= 2)
    def _():
        each_row(lambda r: scatter_row(j - 2, slot, r).wait())

    x = jnp.concatenate([xbuf[slot, pl.ds(s, tile, stride=ROW_TILES), :] for s in range(ROW_TILES)], axis=1)
    x = x.astype(BF16)
    act = _silu(_dot(x, wg_ref[0])) * _dot(x, wu_ref[0])
    y = _dot(act.astype(BF16), wd_ref[0])
    for s in range(ROW_TILES):
        ybuf[slot, pl.ds(s, tile, stride=ROW_TILES), :] = y[:, s * LANES:(s + 1) * LANES]
    each_row(lambda r: scatter_row(j, slot, r).start())

    @pl.when(j == nj - 1)
    def _():
        @pl.when(nj >= 2)
        def _():
            each_row(lambda r: scatter_row(j - 1, 1 - slot, r).wait())
        each_row(lambda r: scatter_row(j, slot, r).wait())


def _moe_experts(h_rows, tile_expert, src, dst, wg, wu, wd, tile):
    n_tiles = tile_expert.shape[0]
    _, _, d_ff = wg.shape
    any_spec = pl.BlockSpec(memory_space=pl.ANY)
    grid_spec = pltpu.PrefetchScalarGridSpec(
        num_scalar_prefetch=3,
        grid=(n_tiles,),
        in_specs=[any_spec,
                  pl.BlockSpec((1, D_MODEL, d_ff), lambda j, te, s, d: (te[j], 0, 0)),
                  pl.BlockSpec((1, D_MODEL, d_ff), lambda j, te, s, d: (te[j], 0, 0)),
                  pl.BlockSpec((1, d_ff, D_MODEL), lambda j, te, s, d: (te[j], 0, 0))],
        out_specs=any_spec,
        scratch_shapes=[pltpu.VMEM((2, tile * ROW_TILES, LANES), F32), pltpu.VMEM((2, tile * ROW_TILES, LANES), F32),
                        pltpu.SemaphoreType.DMA((2,)), pltpu.SemaphoreType.DMA((2,))])
    return pl.pallas_call(
        functools.partial(_moe_expert_kernel, tile=tile),
        grid_spec=grid_spec,
        out_shape=jax.ShapeDtypeStruct((n_tiles * tile * ROW_TILES, LANES), F32),
        compiler_params=_cparams("arbitrary"),
        name="moe_experts",
    )(tile_expert, src, dst, h_rows, wg, wu, wd)


def _moe_combine_kernel(x2_ref, gate_ref, y0_ref, y1_ref, o_ref):
    tm = x2_ref.shape[0]
    gates = gate_ref[...]
    g0, g1 = gates[:, 0:1], gates[:, 1:2]
    for s in range(ROW_TILES):
        lanes = slice(s * LANES, (s + 1) * LANES)
        o_ref[:, lanes] = (x2_ref[:, lanes] + g0 * y0_ref[pl.ds(s, tm, stride=ROW_TILES), :]
                           + g1 * y1_ref[pl.ds(s, tm, stride=ROW_TILES), :])


def _moe_combine(x2, gates, y, tm):
    t = x2.shape[0]
    nb = t // tm
    return pl.pallas_call(
        _moe_combine_kernel,
        grid=(nb,),
        in_specs=[pl.BlockSpec((tm, D_MODEL), lambda i: (i, 0)), pl.BlockSpec((tm, LANES), lambda i: (i, 0)),
                  pl.BlockSpec((tm * ROW_TILES, LANES), lambda i: (i, 0)),
                  pl.BlockSpec((tm * ROW_TILES, LANES), lambda i: (nb + i, 0))],
        out_specs=pl.BlockSpec((tm, D_MODEL), lambda i: (i, 0)),
        out_shape=jax.ShapeDtypeStruct((t, D_MODEL), F32),
        compiler_params=_cparams("parallel"),
        name="moe_combine",
    )(x2, gates, y, y)


def _moe(x2, h_rows, idx, gates, wg, wu, wd, tile, tm):
    t = x2.shape[0]
    n_tiles = (2 * t) // tile + N_EXPERTS
    tile_expert, src, dst = _route(idx, tile, n_tiles)
    y = _moe_experts(h_rows, tile_expert, src, dst, wg, wu, wd, tile)
    return _moe_combine(x2, gates, y, tm)


def _block_diag_ones(n, group):
    idx = np.arange(n) // group
    return jnp.asarray((idx[:, None] == idx[None, :]).astype(np.float32), dtype=BF16)


def _rearrange_w_in(w):
    offs = np.cumsum([0, 128, 128, 256, GLA_RANK, 256, 256, 256, 256, 256, 512, 512, 512])
    a_q, a_k, a_v, a_lr, a_g, b_q, b_f, b_i, b_g, c_q, c_k, c_v = [
        w[:, offs[n]:offs[n + 1]] for n in range(12)]
    lr = jnp.pad(a_lr, ((0, 0), (0, LANES - GLA_RANK)))
    return jnp.concatenate([c_v, a_q, a_k, a_v, a_g, b_q, b_f, b_i, b_g, c_q, c_k, lr], axis=1).astype(BF16)


def _tile_rows(seq, cap):
    t = cap
    while seq % t:
        t //= 2
    return t


def kernel(x_prompt, x_sample, state_gla, state_hgrn, cache_k_win, cache_v_win, norm_mix_g, w_in,
           gla_w_gate_up, gla_b_gate, gla_norm_g, hgrn_lb_logits, hgrn_norm_g, attn_q_norm_g,
           attn_k_norm_g, w_out, norm_ffn_g, ffn_w_gate, ffn_w_up, ffn_w_down, moe_w_router,
           moe_w_gate, moe_w_up, moe_w_down):
    depth = w_in.shape[0]
    bp, lp, _ = x_prompt.shape
    bs, ls, _ = x_sample.shape
    buf = cache_k_win.shape[2]
    width = ATT_HEADS * HEAD_DIM
    assert lp % (BAND * max(DILATIONS)) == 0 and lp % CHUNK == 0 and ls % CHUNK != 0 and ls == SUBLANES
    assert buf >= BAND * max(DILATIONS) and lp % min(buf, lp) == 0

    lb_w = jax.nn.softmax(hgrn_lb_logits.astype(F32), axis=0)
    lower_bounds = jnp.cumsum(lb_w, axis=0) - lb_w[0]

    ones_att = _block_diag_ones(LANES, HEAD_DIM)
    ones_gla = _block_diag_ones(GLA_HEADS * GLA_DV, GLA_DV)
    ones_hgrn = _block_diag_ones(HGRN_HEADS * HGRN_DV, HGRN_DV)
    tm_p = _tile_rows(bp * lp, 256)
    tm_s = _tile_rows(bs * ls, 256)
    tables_p = _rope_tables(jnp.arange(lp, dtype=jnp.int32))
    tables_s = _rope_tables(jnp.tile(PAST_LEN + jnp.arange(ls, dtype=jnp.int32), tm_s // ls))
    ck = jnp.transpose(cache_k_win, (0, 1, 3, 4, 2))
    cv = jnp.transpose(cache_v_win, (0, 1, 3, 4, 2))

    def sample_heads(a):
        return a.reshape(bs, ls, ATT_HEADS, HEAD_DIM)

    def sample_tail(a):
        a = jnp.transpose(sample_heads(a), (0, 2, 3, 1))
        return jnp.pad(a, ((0, 0), (0, 0), (0, 0), (LANES - ls, 0)))

    xp = x_prompt.reshape(bp * lp, D_MODEL)
    xs = x_sample.reshape(bs * ls, D_MODEL)
    zero_gla = jnp.zeros((bp, GLA_HEADS, GLA_DK, GLA_DV), F32)
    zero_hgrn = jnp.zeros((bp, HGRN_HEADS, HGRN_DK, HGRN_DV), F32)
    gla_p, hgrn_p, kw_p, vw_p, gla_s, hgrn_s = [], [], [], [], [], []
    caches_s = None
    keep = min(buf, lp)

    for layer in range(depth):
        w_proj = _rearrange_w_in(w_in[layer])
        g_mix = norm_mix_g[layer].reshape(1, D_MODEL)
        wup = jnp.pad(gla_w_gate_up[layer], ((0, LANES - GLA_RANK), (0, 0))).astype(BF16)
        bias = gla_b_gate[layer].reshape(1, -1)
        gain_a = jnp.tile(gla_norm_g[layer], GLA_HEADS).reshape(1, -1)
        gain_b = jnp.tile(hgrn_norm_g[layer], HGRN_HEADS).reshape(1, -1)
        lb = lower_bounds[layer].reshape(1, -1)
        gq = jnp.tile(attn_q_norm_g[layer], 2).reshape(1, LANES)
        gk = jnp.tile(attn_k_norm_g[layer], 2).reshape(1, LANES)
        wout = w_out[layer].astype(BF16)
        g_ffn = norm_ffn_g[layer].reshape(1, D_MODEL)

        mixed = []
        for which, x, batch, seq, tm in (("p", xp, bp, lp, tm_p), ("s", xs, bs, ls, tm_s)):
            proj = _inproj(x, g_mix, w_proj, tm)
            if which == "p":
                qn, kn, k_win, v_win = _qkrope_window(proj, gq, gk, tables_p, ones_att, batch, seq, keep)
                s0_a, s0_b, block, chunk, seqs = zero_gla, zero_hgrn, min(seq, 4 * CHUNK), CHUNK, 1
            else:
                qn, kn = _qkrope(proj, gq, gk, tables_s, ones_att, tm, 1)
                s0_a, s0_b, block, chunk, seqs = state_gla[layer], state_hgrn[layer], seq, seq, math.gcd(batch, SAMPLE_SEQS)
            o_a, s_a = _gated_linear("gla", proj, batch, seq, s0_a, gain_a, ones_gla, (wup, bias), block, chunk, seqs)
            o_b, s_b = _gated_linear("hgrn", proj, batch, seq, s0_b, gain_b, ones_hgrn, (lb,), block, chunk, seqs)
            if which == "p":
                o_c = _attn_prompt(qn, kn, proj, COL_CV, batch, seq)
                gla_p.append(s_a)
                hgrn_p.append(s_b)
                kw_p.append(k_win)
                vw_p.append(v_win)
            else:
                o_c, ck_new, cv_new = _attn_sample(
                    jnp.transpose(sample_heads(qn), (0, 2, 1, 3)), sample_tail(kn),
                    sample_tail(proj[:, COL_CV:COL_CV + width]), ck, cv, layer, caches_s, seq, 2)
                o_c = jnp.transpose(o_c, (0, 2, 1, 3)).reshape(batch * seq, width)
                caches_s = (ck_new, cv_new)
                gla_s.append(s_a)
                hgrn_s.append(s_b)
            mixed.append((o_a, o_b, o_c))

        j = layer // 2
        new_x = []
        for (o_a, o_b, o_c), x, tm in ((mixed[0], xp, tm_p), (mixed[1], xs, tm_s)):
            if layer % 2 == 0:
                new_x.append(_mix_ffn(x, o_a, o_b, o_c, wout, g_ffn, ffn_w_gate[j].astype(BF16),
                                      ffn_w_up[j].astype(BF16), ffn_w_down[j].astype(BF16), tm))
            else:
                wr = jnp.pad(moe_w_router[j], ((0, 0), (0, LANES - N_EXPERTS)))
                x2, h_rows, idx, gates = _mix_router(x, o_a, o_b, o_c, wout, g_ffn, wr, tm)
                new_x.append(_moe(x2, h_rows, idx, gates, moe_w_gate[j].astype(BF16), moe_w_up[j].astype(BF16),
                                  moe_w_down[j].astype(BF16), MOE_TILE, tm))
        xp, xs = new_x

    return (xp.reshape(bp, lp, D_MODEL), xs.reshape(bs, ls, D_MODEL),
            jnp.stack(gla_p), jnp.stack(hgrn_p),
            jnp.transpose(jnp.stack(kw_p), (0, 1, 4, 2, 3)), jnp.transpose(jnp.stack(vw_p), (0, 1, 4, 2, 3)),
            jnp.stack(gla_s), jnp.stack(hgrn_s),
            jnp.transpose(caches_s[0], (0, 1, 4, 2, 3)), jnp.transpose(caches_s[1], (0, 1, 4, 2, 3)))
```

```python
import functools
import math

import numpy as np
import jax
import jax.numpy as jnp
from jax import lax
from jax.experimental import pallas as pl
from jax.experimental.pallas import tpu as pltpu

F32 = jnp.float32
BF16 = jnp.bfloat16

D_MODEL = 1024
HEAD_DIM = 64
GLA_HEADS = 4
GLA_DK = 32
GLA_DV = 64
GLA_RANK = 16
GLA_TAU = 16.0
HGRN_HEADS = 4
HGRN_DK = 64
HGRN_DV = 64
ATT_HEADS = 8
DILATIONS = (1, 4, 16)
BAND = 128
ROT_DIM = 16
ROPE_THETA = 500000.0
CHUNK = 64
N_EXPERTS = 8
EPS = 1e-6
PAST_LEN = 8192
NEG = -1e30

LANES = 128
SUBLANES = 8
VMEM_LIMIT = 56 * 1024 * 1024
ROW_TILES = D_MODEL // LANES
SAMPLE_SEQS = 16
MOE_TILE = 256

COL_CV = 0
COL_AQ, COL_AK, COL_AV, COL_AG = 512, 640, 768, 1024
COL_BQ, COL_BF, COL_BI, COL_BG = 1280, 1536, 1792, 2048
COL_CQ, COL_CK = 2304, 2816
COL_LR = 3328
N_PROJ = 3456


def _cparams(*sem):
    return pltpu.CompilerParams(dimension_semantics=sem, vmem_limit_bytes=VMEM_LIMIT)


def _sigmoid(x):
    return 1.0 / (1.0 + jnp.exp(-x))


def _silu(x):
    return x * _sigmoid(x)


def _log_sigmoid(x):
    return -(jnp.maximum(-x, 0.0) + jnp.log(1.0 + jnp.exp(-jnp.abs(x))))


def _dot(a, b):
    return jnp.dot(a, b, preferred_element_type=F32)


def _dot_nt(a, b):
    return lax.dot_general(a, b, (((1,), (1,)), ((), ())), preferred_element_type=F32)


def _dot_tn(a, b):
    return lax.dot_general(a, b, (((0,), (0,)), ((), ())), preferred_element_type=F32)


def _group_sum(x, ones_bd):
    hi = x.astype(BF16)
    lo = (x - hi.astype(F32)).astype(BF16)
    return _dot(hi, ones_bd) + _dot(lo, ones_bd)


def _inproj_kernel(x_ref, g_ref, w_ref, o_ref):
    x = x_ref[...]
    ms = jnp.mean(x * x, axis=-1, keepdims=True)
    h = (x * lax.rsqrt(ms + EPS)) * g_ref[...]
    o_ref[...] = _dot(h.astype(BF16), w_ref[...])


def _inproj(x, g, w, tm):
    t = x.shape[0]
    return pl.pallas_call(
        _inproj_kernel,
        grid=(t // tm,),
        in_specs=[pl.BlockSpec((tm, D_MODEL), lambda i: (i, 0)),
                  pl.BlockSpec((1, D_MODEL), lambda i: (0, 0)),
                  pl.BlockSpec((D_MODEL, N_PROJ), lambda i: (0, 0))],
        out_specs=pl.BlockSpec((tm, N_PROJ), lambda i: (i, 0)),
        out_shape=jax.ShapeDtypeStruct((t, N_PROJ), F32),
        compiler_params=_cparams("parallel"),
        name="inproj",
    )(x, g, w)


def _qkrope_kernel(q_ref, k_ref, *rest, window):
    if window:
        v_ref, gq_ref, gk_ref, cos_ref, sa_ref, sb_ref, ones_ref, qo_ref, ko_ref, kt_ref, vt_ref = rest
    else:
        gq_ref, gk_ref, cos_ref, sa_ref, sb_ref, ones_ref, qo_ref, ko_ref = rest
    cos, sa, sb = cos_ref[...], sa_ref[...], sb_ref[...]
    ones_bd = ones_ref[...]

    def norm_rope(x, g):
        ms = _group_sum(x * x, ones_bd) * (1.0 / HEAD_DIM)
        y = (x * lax.rsqrt(ms + EPS)) * g
        up = pltpu.roll(y, LANES - ROT_DIM // 2, axis=1)
        dn = pltpu.roll(y, ROT_DIM // 2, axis=1)
        return y * cos + up * sa + dn * sb

    qo_ref[...] = norm_rope(q_ref[...], gq_ref[...]) * (HEAD_DIM ** -0.5)
    k = norm_rope(k_ref[...], gk_ref[...])
    ko_ref[...] = k
    if window:
        @pl.when(pl.program_id(2) == pl.num_programs(2) - 1)
        def _():
            kt_ref[0] = k.T.reshape(kt_ref.shape[1:])
            vt_ref[0] = v_ref[...].T.reshape(vt_ref.shape[1:])


def _rope_tables(pos):
    half = ROT_DIM // 2
    inv_freq = ROPE_THETA ** (-jnp.arange(0, ROT_DIM, 2, dtype=F32) / ROT_DIM)
    ang = pos.astype(F32)[:, None] * inv_freq[None, :]
    cos, sin = jnp.cos(ang), jnp.sin(ang)
    n = pos.shape[0]
    rest = HEAD_DIM - ROT_DIM
    c = jnp.concatenate([cos, cos, jnp.ones((n, rest), F32)], axis=1)
    sa = jnp.concatenate([-sin, jnp.zeros((n, half + rest), F32)], axis=1)
    sb = jnp.concatenate([jnp.zeros((n, half), F32), sin, jnp.zeros((n, rest), F32)], axis=1)
    return tuple(jnp.tile(a, (1, 2)) for a in (c, sa, sb))


def _qkrope(proj, gq, gk, tables, ones_bd, tm, table_blocks):
    t = proj.shape[0]
    cq, ck = COL_CQ // LANES, COL_CK // LANES
    tab_spec = pl.BlockSpec((tm, LANES), lambda i, j: (i % table_blocks, 0))
    row_spec = pl.BlockSpec((1, LANES), lambda i, j: (0, 0))
    out_spec = pl.BlockSpec((tm, LANES), lambda i, j: (i, j))
    out_sds = jax.ShapeDtypeStruct((t, ATT_HEADS * HEAD_DIM), F32)
    return pl.pallas_call(
        functools.partial(_qkrope_kernel, window=False),
        grid=(t // tm, ATT_HEADS // 2),
        in_specs=[pl.BlockSpec((tm, LANES), lambda i, j: (i, cq + j)),
                  pl.BlockSpec((tm, LANES), lambda i, j: (i, ck + j)),
                  row_spec, row_spec, tab_spec, tab_spec, tab_spec,
                  pl.BlockSpec((LANES, LANES), lambda i, j: (0, 0))],
        out_specs=[out_spec, out_spec],
        out_shape=[out_sds, out_sds],
        compiler_params=_cparams("parallel", "parallel"),
        name="qkrope",
    )(proj, proj, gq, gk, *tables, ones_bd)


def _qkrope_window(proj, gq, gk, tables, ones_bd, batch, seq, window):
    cq, ck, cv = COL_CQ // LANES, COL_CK // LANES, COL_CV // LANES
    nt = seq // window

    def tok(col):
        return pl.BlockSpec((window, LANES), lambda b, j, s: (b * nt + s, col + j))

    tab_spec = pl.BlockSpec((window, LANES), lambda b, j, s: (s, 0))
    row_spec = pl.BlockSpec((1, LANES), lambda b, j, s: (0, 0))
    win_spec = pl.BlockSpec((1, 2, HEAD_DIM, window), lambda b, j, s: (b, j, 0, 0))
    out_sds = jax.ShapeDtypeStruct((batch * seq, ATT_HEADS * HEAD_DIM), F32)
    win_sds = jax.ShapeDtypeStruct((batch, ATT_HEADS, HEAD_DIM, window), F32)
    return pl.pallas_call(
        functools.partial(_qkrope_kernel, window=True),
        grid=(batch, ATT_HEADS // 2, nt),
        in_specs=[tok(cq), tok(ck), tok(cv), row_spec, row_spec, tab_spec, tab_spec, tab_spec,
                  pl.BlockSpec((LANES, LANES), lambda b, j, s: (0, 0))],
        out_specs=[tok(0), tok(0), win_spec, win_spec],
        out_shape=[out_sds, out_sds, win_sds, win_sds],
        compiler_params=_cparams("parallel", "parallel", "arbitrary"),
        name="qkrope_window",
    )(proj, proj, proj, gq, gk, *tables, ones_bd)


def _segment_scans(g, chunk):
    rows, hk = g.shape
    n_chunks = rows // chunk
    row = lax.broadcasted_iota(jnp.int32, g.shape, 0)

    def at(x, shift):
        return pltpu.roll(x, shift % rows, axis=0)

    zero = jnp.zeros_like(g)
    c = {1: g, 2: g + jnp.where(row % 2 == 1, at(g, 1), zero)}
    c[4] = c[2] + jnp.where(row % 4 == 2, at(c[2], 1), zero) + jnp.where(row % 4 == 3, at(c[2], 2), zero)
    d = {1: zero, 2: jnp.where(row % 2 == 0, at(g, -1), zero)}
    c4_end = jnp.where(row % 4 == 0, at(c[4], -3),
                       jnp.where(row % 4 == 1, at(c[4], -2),
                                 jnp.where(row % 4 == 2, at(c[4], -1), c[4])))
    d[4] = c4_end - c[4]
    tiles = chunk // SUBLANES
    c4_t = c[4].reshape(n_chunks * tiles, SUBLANES, hk)
    row_t = row.reshape(n_chunks * tiles, SUBLANES, hk)
    c8_t = c4_t + jnp.where(row_t % 8 >= 4, c4_t[:, 3:4, :], jnp.zeros_like(c4_t))
    ends = c8_t[:, SUBLANES - 1:SUBLANES, :]
    prefix = []
    for ci in range(n_chunks):
        acc = [jnp.zeros((1, 1, hk), F32)]
        for j in range(tiles):
            acc.append(acc[-1] + ends[ci * tiles + j:ci * tiles + j + 1])
        prefix.append(acc)

    def per_tile(pick):
        return jnp.concatenate([prefix[ci][pick(j)] for ci in range(n_chunks) for j in range(tiles)], axis=0)

    b_t = c8_t + per_tile(lambda j: j)
    width = 1
    while width * SUBLANES < chunk:
        m = width * SUBLANES
        c[m] = (b_t - per_tile(lambda j: (j // width) * width)).reshape(rows, hk)
        d[m] = (per_tile(lambda j: (j // width + 1) * width) - b_t).reshape(rows, hk)
        width *= 2
    b_rest = (per_tile(lambda j: tiles) - b_t).reshape(rows, hk)
    b_last = [prefix[ci][tiles].reshape(1, hk) for ci in range(n_chunks)]
    return c, d, b_t.reshape(rows, hk), b_rest, b_last


def _pair_masks(chunk, copies):
    t = lax.broadcasted_iota(jnp.int32, (copies * chunk, chunk), 0) % chunk
    s = lax.broadcasted_iota(jnp.int32, (copies * chunk, chunk), 1)
    masks = {0: t == s}
    m = 1
    while m < chunk:
        masks[m] = (t // (2 * m) == s // (2 * m)) & (t % (2 * m) >= m) & (s % (2 * m) < m)
        m *= 2
    return masks


def _gl_block(q, k, g, v, st, heads, dk, dv, chunk):
    rows = q.shape[0]
    hk, hv = heads * dk, heads * dv
    stacked_masks = _pair_masks(chunk, heads)
    levels = sorted(stacked_masks)
    c, d, b, b_rest, b_last = _segment_scans(g, chunk)
    k_bf = k.astype(BF16)
    q_lvl = {0: q.astype(BF16)}
    k_lvl = {0: k_bf, 1: k_bf}
    for m in levels[1:]:
        q_lvl[m] = (q * jnp.exp(c[m])).astype(BF16)
        if m > 1:
            k_lvl[m] = (k * jnp.exp(d[m])).astype(BF16)
    q_in = (q * jnp.exp(b)).astype(BF16)
    k_out = (k * jnp.exp(b_rest)).astype(BF16)
    v_bf = v.astype(BF16)
    lane_k = lax.broadcasted_iota(jnp.int32, (chunk, hk), 1) // dk
    lane_v = lax.broadcasted_iota(jnp.int32, (chunk, hv), 1) // dv
    same_head = (lax.broadcasted_iota(jnp.int32, (hv, hk), 0) // dv
                 == lax.broadcasted_iota(jnp.int32, (hv, hk), 1) // dk)
    zero_k = jnp.zeros((chunk, hk), BF16)
    zero_v = jnp.zeros((chunk, hv), BF16)
    o_intra, st_inc = [], []
    for ci in range(rows // chunk):
        r = slice(ci * chunk, (ci + 1) * chunk)
        a = jnp.zeros((heads * chunk, chunk), F32)
        for m in levels:
            q_heads = jnp.concatenate([jnp.where(lane_k == h, q_lvl[m][r], zero_k) for h in range(heads)], axis=0)
            a = jnp.where(stacked_masks[m], _dot_nt(q_heads, k_lvl[m][r]), a)
        a = a.astype(BF16)
        o = None
        for h in range(heads):
            part = _dot(a[h * chunk:(h + 1) * chunk], jnp.where(lane_v == h, v_bf[r], zero_v))
            o = part if o is None else o + part
        o_intra.append(o)
        st_inc.append(jnp.where(same_head, _dot_tn(v_bf[r], k_out[r]), 0.0))
    independent = isinstance(st, list)
    outs, finals = [], []
    for ci in range(rows // chunk):
        r = slice(ci * chunk, (ci + 1) * chunk)
        cur = st[ci] if independent else st
        outs.append(o_intra[ci] + _dot_nt(q_in[r], cur.astype(BF16)))
        nxt = cur * jnp.exp(b_last[ci]) + st_inc[ci]
        if independent:
            finals.append(nxt)
        else:
            st = nxt
    return (outs[0] if len(outs) == 1 else jnp.concatenate(outs, axis=0)), (finals if independent else st)


def _head_norm_gate(o, gain, gate, ones_bd, dv):
    ms = _group_sum(o * o, ones_bd) * (1.0 / dv)
    return ((o * lax.rsqrt(ms + EPS)) * gain) * _silu(gate)


def _gl_kernel(*refs, mode, heads, dk, dv, chunk):
    if mode == "gla":
        (q_ref, k_ref, v_ref, gate_ref, lr_ref, wup_ref, bias_ref, gain_ref, s0_ref, ones_ref,
         o_ref, sT_ref, st_ref) = refs
        q = q_ref[...] * (dk ** -0.5)
        k = k_ref[...]
        g = _log_sigmoid(_dot(lr_ref[...].astype(BF16), wup_ref[...]) + bias_ref[...]) * (1.0 / GLA_TAU)
    else:
        (q_ref, f_ref, v_ref, gate_ref, lb_ref, gain_ref, s0_ref, ones_ref,
         o_ref, sT_ref, st_ref) = refs
        lb = lb_ref[...]
        q = _silu(q_ref[...])
        z = f_ref[...]
        g = jnp.log(lb + (1.0 - lb) * _sigmoid(z))
        k = (1.0 - lb) * _sigmoid(-z)
    seqs = s0_ref.shape[0]
    if seqs > 1:
        o, finals = _gl_block(q, k, g, v_ref[...], [s0_ref[n] for n in range(seqs)], heads, dk, dv, chunk)
        for n in range(seqs):
            sT_ref[n] = finals[n]
    else:
        i = pl.program_id(1)

        @pl.when(i == 0)
        def _():
            st_ref[...] = s0_ref[0]

        o, st = _gl_block(q, k, g, v_ref[...], st_ref[...], heads, dk, dv, chunk)
        st_ref[...] = st

        @pl.when(i == pl.num_programs(1) - 1)
        def _():
            sT_ref[0] = st
    o_ref[...] = _head_norm_gate(o, gain_ref[...], gate_ref[...], ones_ref[...], dv)


def _embed_state(s):
    b, h, k, v = s.shape
    eye = jnp.eye(h, dtype=s.dtype)
    return jnp.einsum("bhkv,hg->bhvgk", s, eye).reshape(b, h * v, h * k)


def _extract_state(st, h, k, v):
    b = st.shape[0]
    st = st.reshape(b, h, v, h, k)
    return jnp.stack([st[:, i, :, i, :] for i in range(h)], axis=1).transpose(0, 1, 3, 2)


def _gated_linear(mode, proj, batch, seq, s0, gain, ones_bd, extra, block_tokens, chunk, seqs=1):
    assert seqs == 1 or (block_tokens == seq == chunk and batch % seqs == 0)
    heads, dk, dv = (GLA_HEADS, GLA_DK, GLA_DV) if mode == "gla" else (HGRN_HEADS, HGRN_DK, HGRN_DV)
    hk, hv = heads * dk, heads * dv
    nb = seq // block_tokens
    tb = block_tokens * seqs

    def cols(width, col):
        return pl.BlockSpec((tb, width), lambda b, i: (b * nb + i, col // width))

    def const(shape):
        return pl.BlockSpec(shape, lambda b, i: (0,) * len(shape))

    state_spec = pl.BlockSpec((seqs, hv, hk), lambda b, i: (b, 0, 0))
    if mode == "gla":
        wup, bias = extra
        in_specs = [cols(hk, COL_AQ), cols(hk, COL_AK), cols(hv, COL_AV), cols(hv, COL_AG), cols(LANES, COL_LR),
                    const((LANES, hk)), const((1, hk)), const((1, hv)), state_spec, const((hv, hv))]
        args = (proj, proj, proj, proj, proj, wup, bias, gain, _embed_state(s0), ones_bd)
    else:
        (lb,) = extra
        in_specs = [cols(hk, COL_BQ), cols(hk, COL_BF), cols(hv, COL_BI), cols(hv, COL_BG),
                    const((1, hk)), const((1, hv)), state_spec, const((hv, hv))]
        args = (proj, proj, proj, proj, lb, gain, _embed_state(s0), ones_bd)
    o, st = pl.pallas_call(
        functools.partial(_gl_kernel, mode=mode, heads=heads, dk=dk, dv=dv, chunk=chunk),
        grid=(batch // seqs, nb),
        in_specs=in_specs,
        out_specs=[pl.BlockSpec((tb, hv), lambda b, i: (b * nb + i, 0)), state_spec],
        out_shape=[jax.ShapeDtypeStruct((batch * seq, hv), F32),
                   jax.ShapeDtypeStruct((batch, hv, hk), F32)],
        scratch_shapes=[pltpu.VMEM((hv, hk), F32)],
        compiler_params=_cparams("parallel", "arbitrary"),
        name=mode + "_scan",
    )(*args)
    return o, _extract_state(st, heads, dk, dv)


def _attn_prompt_kernel(q_ref, k_ref, v_ref, o_ref, qr_ref, kt_ref, kr_ref, vt_ref, vr_ref,
                        m_ref, l_ref, acc_ref, *, seq):
    n_res = max(DILATIONS)
    per_res = seq // n_res
    head0 = lax.broadcasted_iota(jnp.int32, (per_res, LANES), 1) < HEAD_DIM
    for r in range(n_res):
        tok = slice(r * per_res, (r + 1) * per_res)
        strided = pl.ds(r, per_res, stride=n_res)
        k_t, v_t = k_ref[tok, :], v_ref[tok, :]
        k_r, v_r = k_ref[strided, :], v_ref[strided, :]
        qr_ref[tok, :] = q_ref[strided, :]
        for h in range(2):
            mine = head0 if h == 0 else ~head0
            kt_ref[h, tok, :] = jnp.where(mine, k_t, 0.0).astype(BF16)
            kr_ref[h, tok, :] = jnp.where(mine, k_r, 0.0).astype(BF16)
            vt_ref[h, tok, :] = jnp.where(mine, v_t, 1.0).astype(BF16)
            vr_ref[h, tok, :] = jnp.where(mine, v_r, 1.0).astype(BF16)
    m_ref[...] = jnp.full(m_ref.shape, NEG, F32)
    l_ref[...] = jnp.zeros(l_ref.shape, F32)
    acc_ref[...] = jnp.zeros(acc_ref.shape, F32)

    head0 = lax.broadcasted_iota(jnp.int32, (BAND, LANES), 1) < HEAD_DIM
    qi = lax.broadcasted_iota(jnp.int32, (BAND, 2 * BAND), 0)
    ki = lax.broadcasted_iota(jnp.int32, (BAND, 2 * BAND), 1)

    for dil in DILATIONS:
        n_runs = n_res // dil
        run = SUBLANES * dil
        keys_by_token = dil == 1

        def seq_index(u, n_runs=n_runs, run=run):
            return n_runs * (u % run) + u // run

        kj = ki if keys_by_token else seq_index(ki % BAND) + BAND * (ki // BAND)
        rel = BAND + seq_index(qi) - kj
        in_band = (rel >= 0) & (rel <= BAND)
        bias_prev = jnp.where(in_band, 0.0, NEG).astype(F32)
        bias_first = jnp.where(in_band & (ki >= BAND), 0.0, NEG).astype(F32)

        def body(idx, carry, dil=dil, n_runs=n_runs, run=run, keys_by_token=keys_by_token,
                 bias_prev=bias_prev, bias_first=bias_first):
            res = idx % dil
            blk = idx // dil
            starts = [pl.multiple_of((res + dil * c) * per_res + run * blk, run) for c in range(n_runs)]
            prevs = [pl.multiple_of((res + dil * c) * per_res + run * jnp.maximum(blk - 1, 0), run)
                     for c in range(n_runs)]

            def gather(ref, offs, *lead):
                parts = [ref[lead + (pl.ds(o, run), slice(None))] for o in offs]
                return parts[0] if len(parts) == 1 else jnp.concatenate(parts, axis=0)

            def scatter(ref, offs, val, *lead):
                for c, o in enumerate(offs):
                    ref[lead + (pl.ds(o, run), slice(None))] = val[c * run:(c + 1) * run, :]

            q = gather(qr_ref, starts).astype(BF16)
            bias = jnp.where(blk > 0, bias_prev, bias_first)
            alphas, pvs = [], []
            for h in range(2):
                if keys_by_token:
                    cur = pl.multiple_of(blk * BAND, BAND)
                    prev = pl.multiple_of(jnp.maximum(blk - 1, 0) * BAND, BAND)
                    k2 = jnp.concatenate([kt_ref[h, pl.ds(prev, BAND), :], kt_ref[h, pl.ds(cur, BAND), :]], axis=0)
                    v2 = jnp.concatenate([vt_ref[h, pl.ds(prev, BAND), :], vt_ref[h, pl.ds(cur, BAND), :]], axis=0)
                else:
                    k2 = jnp.concatenate([gather(kr_ref, prevs, h), gather(kr_ref, starts, h)], axis=0)
                    v2 = jnp.concatenate([gather(vr_ref, prevs, h), gather(vr_ref, starts, h)], axis=0)
                s = _dot_nt(q, k2) + bias
                m_prev = gather(m_ref, starts, h)
                m_new = jnp.maximum(m_prev, jnp.max(s, axis=-1, keepdims=True))
                scatter(m_ref, starts, m_new, h)
                p = jnp.exp(s - jnp.concatenate([m_new, m_new], axis=1))
                alphas.append(jnp.exp(m_prev - m_new))
                pvs.append(_dot(p.astype(BF16), v2))
            scatter(l_ref, starts, jnp.where(head0, alphas[1], alphas[0]) * gather(l_ref, starts)
                    + jnp.where(head0, pvs[1], pvs[0]))
            scatter(acc_ref, starts, jnp.where(head0, alphas[0], alphas[1]) * gather(acc_ref, starts)
                    + jnp.where(head0, pvs[0], pvs[1]))
            return carry

        lax.fori_loop(0, seq // BAND, body, 0, unroll=8)

    for r in range(n_res):
        tok = slice(r * per_res, (r + 1) * per_res)
        o_ref[pl.ds(r, per_res, stride=n_res), :] = acc_ref[tok, :] / pltpu.roll(l_ref[tok, :], HEAD_DIM, axis=1)


def _attn_prompt(q, k, v_src, v_col, batch, seq):
    spec = pl.BlockSpec((seq, LANES), lambda b, j: (b, j))
    vc = v_col // LANES
    return pl.pallas_call(
        functools.partial(_attn_prompt_kernel, seq=seq),
        grid=(batch, ATT_HEADS // 2),
        in_specs=[spec, spec, pl.BlockSpec((seq, LANES), lambda b, j: (b, vc + j))],
        out_specs=spec,
        out_shape=jax.ShapeDtypeStruct((batch * seq, ATT_HEADS * HEAD_DIM), F32),
        scratch_shapes=[pltpu.VMEM((seq, LANES), F32),
                        pltpu.VMEM((2, seq, LANES), BF16), pltpu.VMEM((2, seq, LANES), BF16),
                        pltpu.VMEM((2, seq, LANES), BF16), pltpu.VMEM((2, seq, LANES), BF16),
                        pltpu.VMEM((2, seq, LANES), F32), pltpu.VMEM((seq, LANES), F32),
                        pltpu.VMEM((seq, LANES), F32)],
        compiler_params=_cparams("parallel", "parallel"),
        name="attn_prompt",
    )(q, k, v_src)


def _sample_counts(steps, buf):
    t = np.arange(steps)[:, None]

    def count(dist, live):
        cnt = np.zeros(dist.shape, np.float32)
        for dil in DILATIONS:
            cnt += (live & (dist >= 0) & (dist % dil == 0) & (dist <= BAND * dil)).astype(np.float32)
        return cnt

    lane = np.arange(LANES)[None, :]
    return (count(buf + t - np.arange(buf)[None, :], True),
            count(t - (lane - (LANES - steps)), lane >= LANES - steps))


def _attn_sample_kernel(q_ref, kn_ref, vn_ref, kc_ref, vc_ref, cc_ref, cn_ref, *rest, steps, aliased):
    if aliased:
        rest = rest[2:]
    o_ref, ko_ref, vo_ref = rest
    q = q_ref[0].astype(BF16)
    k_old, v_old = kc_ref[0, 0], vc_ref[0, 0]
    k_new, v_new = kn_ref[0], vn_ref[0]
    cc, cn = cc_ref[...][None], cn_ref[...][None]

    def scores(keys):
        return lax.dot_general(q, keys.astype(BF16), (((2,), (1,)), ((0,), (0,))), preferred_element_type=F32)

    def weighted(p, values):
        return lax.dot_general(p.astype(BF16), values.astype(BF16), (((2,), (2,)), ((0,), (0,))),
                               preferred_element_type=F32)

    s_old = jnp.where(cc > 0.0, scores(k_old), NEG)
    s_new = jnp.where(cn > 0.0, scores(k_new), NEG)
    m = jnp.maximum(jnp.max(s_old, axis=-1, keepdims=True), jnp.max(s_new, axis=-1, keepdims=True))
    p_old = cc * jnp.exp(s_old - m)
    p_new = cn * jnp.exp(s_new - m)
    den = jnp.sum(p_old, axis=-1, keepdims=True) + jnp.sum(p_new, axis=-1, keepdims=True)
    o_ref[0] = (weighted(p_old, v_old) + weighted(p_new, v_new)) / den

    buf = k_old.shape[-1]
    lane = lax.broadcasted_iota(jnp.int32, k_new.shape, 2)
    for old, new, out_ref in ((k_old, k_new, ko_ref), (v_old, v_new, vo_ref)):
        shifted = pltpu.roll(old, buf - steps, axis=2)
        out_ref[0, 0, :, :, 0:buf - LANES] = shifted[:, :, 0:buf - LANES]
        out_ref[0, 0, :, :, buf - LANES:buf] = jnp.where(lane >= LANES - steps, new, shifted[:, :, buf - LANES:buf])


def _attn_sample(q, k_new, v_new, cache_k, cache_v, layer, prev_out, steps, head_split):
    depth, batch, heads, hd, buf = cache_k.shape
    hb = heads // head_split
    cc, cn = (jnp.asarray(c) for c in _sample_counts(steps, buf))
    q_spec = pl.BlockSpec((1, hb, steps, hd), lambda b, j: (b, j, 0, 0))
    new_spec = pl.BlockSpec((1, hb, hd, LANES), lambda b, j: (b, j, 0, 0))
    cache_spec = pl.BlockSpec((1, 1, hb, hd, buf), lambda b, j: (layer, b, j, 0, 0))
    in_specs = [q_spec, new_spec, new_spec, cache_spec, cache_spec,
                pl.BlockSpec((steps, buf), lambda b, j: (0, 0)), pl.BlockSpec((steps, LANES), lambda b, j: (0, 0))]
    args = [q, k_new, v_new, cache_k, cache_v, cc, cn]
    aliases = {}
    if prev_out is not None:
        any_spec = pl.BlockSpec(memory_space=pl.ANY)
        in_specs += [any_spec, any_spec]
        args += list(prev_out)
        aliases = {7: 1, 8: 2}
    cache_sds = jax.ShapeDtypeStruct(cache_k.shape, F32)
    return pl.pallas_call(
        functools.partial(_attn_sample_kernel, steps=steps, aliased=prev_out is not None),
        grid=(batch, head_split),
        in_specs=in_specs,
        out_specs=[q_spec, cache_spec, cache_spec],
        out_shape=[jax.ShapeDtypeStruct(q.shape, F32), cache_sds, cache_sds],
        input_output_aliases=aliases,
        compiler_params=_cparams("arbitrary", "arbitrary"),
        name="attn_sample",
    )(*args)


def _outproj(oa_ref, ob_ref, oc_ref, wout_ref, x_ref):
    na, nb = oa_ref.shape[1], ob_ref.shape[1]
    mixed = (_dot(oa_ref[...].astype(BF16), wout_ref[0:na, :])
             + _dot(ob_ref[...].astype(BF16), wout_ref[na:na + nb, :])
             + _dot(oc_ref[...].astype(BF16), wout_ref[na + nb:, :]))
    return x_ref[...] + mixed


def _rmsnorm(x, g):
    return (x * lax.rsqrt(jnp.mean(x * x, axis=-1, keepdims=True) + EPS)) * g


def _mix_ffn_kernel(x_ref, oa_ref, ob_ref, oc_ref, wout_ref, g_ref, wg_ref, wu_ref, wd_ref, o_ref):
    x2 = _outproj(oa_ref, ob_ref, oc_ref, wout_ref, x_ref)
    h = _rmsnorm(x2, g_ref[...]).astype(BF16)
    act = _silu(_dot(h, wg_ref[...])) * _dot(h, wu_ref[...])
    o_ref[...] = x2 + _dot(act.astype(BF16), wd_ref[...])


def _mix_ffn(x, oa, ob, oc, wout, g, wg, wu, wd, tm):
    t = x.shape[0]
    d_ff = wg.shape[1]

    def rows(width):
        return pl.BlockSpec((tm, width), lambda i: (i, 0))

    def const(shape):
        return pl.BlockSpec(shape, lambda i: (0, 0), pipeline_mode=pl.Buffered(1))

    return pl.pallas_call(
        _mix_ffn_kernel,
        grid=(t // tm,),
        in_specs=[rows(D_MODEL), rows(oa.shape[1]), rows(ob.shape[1]), rows(oc.shape[1]),
                  const((D_MODEL, D_MODEL)), const((1, D_MODEL)),
                  const((D_MODEL, d_ff)), const((D_MODEL, d_ff)), const((d_ff, D_MODEL))],
        out_specs=rows(D_MODEL),
        out_shape=jax.ShapeDtypeStruct((t, D_MODEL), F32),
        compiler_params=_cparams("parallel"),
        name="mix_ffn",
    )(x, oa, ob, oc, wout, g, wg, wu, wd)


def _mix_router_kernel(x_ref, oa_ref, ob_ref, oc_ref, wout_ref, g_ref, wr_ref, x2_ref, h_ref, idx_ref, gate_ref):
    tm = x_ref.shape[0]
    x2 = _outproj(oa_ref, ob_ref, oc_ref, wout_ref, x_ref)
    x2_ref[...] = x2
    h = _rmsnorm(x2, g_ref[...])
    for s in range(D_MODEL // LANES):
        h_ref[pl.ds(s, tm, stride=D_MODEL // LANES), :] = h[:, s * LANES:(s + 1) * LANES]
    h_hi = h.astype(BF16)
    h_lo = (h - h_hi.astype(F32)).astype(BF16)
    by_hi = _dot(h_hi, wr_ref[...])
    logits = by_hi[:, 0:LANES] + by_hi[:, LANES:2 * LANES] + _dot(h_lo, wr_ref[:, 0:LANES])
    lane = lax.broadcasted_iota(jnp.int32, logits.shape, 1)
    logits = jnp.where(lane < N_EXPERTS, logits, -jnp.inf)
    v1 = jnp.max(logits, axis=-1, keepdims=True)
    i1 = jnp.min(jnp.where(logits == v1, lane, LANES), axis=-1, keepdims=True)
    rest = jnp.where(lane == i1, -jnp.inf, logits)
    v2 = jnp.max(rest, axis=-1, keepdims=True)
    i2 = jnp.min(jnp.where(rest == v2, lane, LANES), axis=-1, keepdims=True)
    e2 = jnp.exp(v2 - v1)
    idx_ref[...] = jnp.where(lane == 0, i1, jnp.where(lane == 1, i2, 0))
    gate_ref[...] = jnp.where(lane == 0, 1.0 / (1.0 + e2), jnp.where(lane == 1, e2 / (1.0 + e2), 0.0))


def _mix_router(x, oa, ob, oc, wout, g, wr, tm):
    t = x.shape[0]

    def rows(width):
        return pl.BlockSpec((tm, width), lambda i: (i, 0))

    def const(shape):
        return pl.BlockSpec(shape, lambda i: (0, 0))

    return pl.pallas_call(
        _mix_router_kernel,
        grid=(t // tm,),
        in_specs=[rows(D_MODEL), rows(oa.shape[1]), rows(ob.shape[1]), rows(oc.shape[1]),
                  const((D_MODEL, D_MODEL)), const((1, D_MODEL)), const((D_MODEL, 2 * LANES))],
        out_specs=[rows(D_MODEL), pl.BlockSpec((tm * ROW_TILES, LANES), lambda i: (i, 0)), rows(LANES), rows(LANES)],
        out_shape=[jax.ShapeDtypeStruct((t, D_MODEL), F32), jax.ShapeDtypeStruct((t * ROW_TILES, LANES), F32),
                   jax.ShapeDtypeStruct((t, LANES), jnp.int32), jax.ShapeDtypeStruct((t, LANES), F32)],
        compiler_params=_cparams("parallel"),
        name="mix_router",
    )(x, oa, ob, oc, wout, g, wr)


def _route(idx, tile):
    t = idx.shape[0]
    n_tiles = (2 * t) // tile
    n_items = n_tiles + N_EXPERTS - 1
    expert = jnp.concatenate([idx[:, 0], idx[:, 1]])
    onehot = (expert[:, None] == jnp.arange(N_EXPERTS, dtype=jnp.int32)[None, :]).astype(jnp.int32)
    csum = jnp.cumsum(onehot, axis=0)
    rank = jnp.take_along_axis(csum, expert[:, None], axis=1)[:, 0] - 1
    group_end = jnp.cumsum(csum[-1])
    group_start = group_end - csum[-1]
    pos = (group_start[expert] + rank).astype(jnp.int32)
    tile_lo = jnp.arange(n_tiles, dtype=jnp.int32)[:, None] * tile
    present = (group_start[None, :] < tile_lo + tile) & (group_end[None, :] > tile_lo)
    seen = jnp.cumsum(present.reshape(-1).astype(jnp.int32))
    k = jnp.arange(n_items, dtype=jnp.int32)
    flat = jnp.sum((seen[None, :] <= jnp.minimum(k, seen[-1] - 1)[:, None]).astype(jnp.int32), axis=1)
    item_tile, item_expert = flat // N_EXPERTS, flat % N_EXPERTS
    lo = jnp.clip(group_start[item_expert] - item_tile * tile, 0, tile)
    hi = jnp.where(k < seen[-1], jnp.clip(group_end[item_expert] - item_tile * tile, 0, tile), lo)
    first = jnp.concatenate([jnp.ones((1,), jnp.int32), (item_tile[1:] != item_tile[:-1]).astype(jnp.int32)])
    return pos, tuple(a.astype(jnp.int32) for a in (item_tile, item_expert, lo, hi, first))


def _moe_dispatch_kernel(pos_ref, h_hbm, x_hbm, sem, *, batch, t):
    s = pl.program_id(0)
    ns = pl.num_programs(0)

    def row_copy(step, r):
        a = step * batch + r
        tok = jnp.where(a >= t, a - t, a)
        return pltpu.make_async_copy(h_hbm.at[pl.ds(pl.multiple_of(tok * ROW_TILES, ROW_TILES), ROW_TILES), :],
                                     x_hbm.at[pl.ds(pl.multiple_of(pos_ref[a] * ROW_TILES, ROW_TILES), ROW_TILES), :],
                                     sem.at[step % 2])

    def each_row(fn):
        def body(r, carry):
            fn(r)
            return carry
        lax.fori_loop(0, batch, body, 0, unroll=8)

    each_row(lambda r: row_copy(s, r).start())

    @pl.when(s >= 1)
    def _():
        each_row(lambda r: row_copy(s - 1, r).wait())

    @pl.when(s == ns - 1)
    def _():
        each_row(lambda r: row_copy(s, r).wait())


def _moe_dispatch(h_rows, pos, batch):
    a_n = pos.shape[0]
    t = a_n // 2
    any_spec = pl.BlockSpec(memory_space=pl.ANY)
    grid_spec = pltpu.PrefetchScalarGridSpec(
        num_scalar_prefetch=1, grid=(a_n // batch,), in_specs=[any_spec], out_specs=any_spec,
        scratch_shapes=[pltpu.SemaphoreType.DMA((2,))])
    return pl.pallas_call(
        functools.partial(_moe_dispatch_kernel, batch=batch, t=t),
        grid_spec=grid_spec,
        out_shape=jax.ShapeDtypeStruct((a_n * ROW_TILES, LANES), F32),
        compiler_params=_cparams("arbitrary"),
        name="moe_dispatch",
    )(pos, h_rows)


def _moe_expert_kernel(it_ref, ie_ref, lo_ref, hi_ref, first_ref, x_ref, wg_ref, wu_ref, wd_ref, y_ref, *, tile):
    del it_ref, ie_ref
    k = pl.program_id(0)
    x = jnp.concatenate([x_ref[pl.ds(s, tile, stride=ROW_TILES), :] for s in range(ROW_TILES)], axis=1)
    x = x.astype(BF16)
    act = _silu(_dot(x, wg_ref[0])) * _dot(x, wu_ref[0])
    y = _dot(act.astype(BF16), wd_ref[0])
    row = lax.broadcasted_iota(jnp.int32, (tile, LANES), 0)
    mine = (row >= lo_ref[k]) & (row < hi_ref[k])

    @pl.when(first_ref[k] == 1)
    def _():
        for s in range(ROW_TILES):
            y_ref[pl.ds(s, tile, stride=ROW_TILES), :] = jnp.where(mine, y[:, s * LANES:(s + 1) * LANES], 0.0)

    @pl.when(first_ref[k] != 1)
    def _():
        for s in range(ROW_TILES):
            rows = pl.ds(s, tile, stride=ROW_TILES)
            y_ref[rows, :] = jnp.where(mine, y[:, s * LANES:(s + 1) * LANES], y_ref[rows, :])


def _moe_experts(x_rows, items, wg, wu, wd, tile):
    n_items = items[0].shape[0]
    _, _, d_ff = wg.shape
    rows_spec = pl.BlockSpec((tile * ROW_TILES, LANES), lambda k, it, ie, lo, hi, fi: (it[k], 0))

    def w_spec(shape):
        return pl.BlockSpec((1,) + shape, lambda k, it, ie, lo, hi, fi: (ie[k], 0, 0))

    grid_spec = pltpu.PrefetchScalarGridSpec(
        num_scalar_prefetch=5, grid=(n_items,),
        in_specs=[rows_spec, w_spec((D_MODEL, d_ff)), w_spec((D_MODEL, d_ff)), w_spec((d_ff, D_MODEL))],
        out_specs=rows_spec)
    return pl.pallas_call(
        functools.partial(_moe_expert_kernel, tile=tile),
        grid_spec=grid_spec,
        out_shape=jax.ShapeDtypeStruct(x_rows.shape, F32),
        compiler_params=_cparams("arbitrary"),
        name="moe_experts",
    )(*items, x_rows, wg, wu, wd)


def _moe_combine_kernel(pos_ref, x2_ref, gate_ref, y_hbm, o_ref, ybuf, sem, *, t):
    tm = x2_ref.shape[0]
    i = pl.program_id(0)
    n = pl.num_programs(0)
    slot = i % 2

    def row_copy(step, sl, j, r):
        a = j * t + step * tm + r
        return pltpu.make_async_copy(y_hbm.at[pl.ds(pl.multiple_of(pos_ref[a] * ROW_TILES, ROW_TILES), ROW_TILES), :],
                                     ybuf.at[sl, j, pl.ds(pl.multiple_of(r * ROW_TILES, ROW_TILES), ROW_TILES), :],
                                     sem.at[sl])

    def each_row(fn):
        def body(r, carry):
            fn(0, r)
            fn(1, r)
            return carry
        lax.fori_loop(0, tm, body, 0, unroll=4)

    @pl.when(i == 0)
    def _():
        each_row(lambda j, r: row_copy(0, 0, j, r).start())

    @pl.when(i + 1 < n)
    def _():
        each_row(lambda j, r: row_copy(i + 1, 1 - slot, j, r).start())

    each_row(lambda j, r: row_copy(i, slot, j, r).wait())
    gates = gate_ref[...]
    g0, g1 = gates[:, 0:1], gates[:, 1:2]
    for s in range(ROW_TILES):
        lanes = slice(s * LANES, (s + 1) * LANES)
        rows = pl.ds(s, tm, stride=ROW_TILES)
        o_ref[:, lanes] = x2_ref[:, lanes] + g0 * ybuf[slot, 0, rows, :] + g1 * ybuf[slot, 1, rows, :]


def _moe_combine(x2, gates, y, pos, tm):
    t = x2.shape[0]
    grid_spec = pltpu.PrefetchScalarGridSpec(
        num_scalar_prefetch=1, grid=(t // tm,),
        in_specs=[pl.BlockSpec((tm, D_MODEL), lambda i, p: (i, 0)), pl.BlockSpec((tm, LANES), lambda i, p: (i, 0)),
                  pl.BlockSpec(memory_space=pl.ANY)],
        out_specs=pl.BlockSpec((tm, D_MODEL), lambda i, p: (i, 0)),
        scratch_shapes=[pltpu.VMEM((2, 2, tm * ROW_TILES, LANES), F32), pltpu.SemaphoreType.DMA((2,))])
    return pl.pallas_call(
        functools.partial(_moe_combine_kernel, t=t),
        grid_spec=grid_spec,
        out_shape=jax.ShapeDtypeStruct((t, D_MODEL), F32),
        compiler_params=_cparams("arbitrary"),
        name="moe_combine",
    )(pos, x2, gates, y)


def _moe(x2, h_rows, idx, gates, wg, wu, wd, tile, tm):
    tile = math.gcd(2 * x2.shape[0], tile)
    pos, items = _route(idx, tile)
    x_rows = _moe_dispatch(h_rows, pos, tile)
    y = _moe_experts(x_rows, items, wg, wu, wd, tile)
    return _moe_combine(x2, gates, y, pos, tm)


def _block_diag_ones(n, group):
    idx = np.arange(n) // group
    return jnp.asarray((idx[:, None] == idx[None, :]).astype(np.float32), dtype=BF16)


def _rearrange_w_in(w):
    offs = np.cumsum([0, 128, 128, 256, GLA_RANK, 256, 256, 256, 256, 256, 512, 512, 512])
    a_q, a_k, a_v, a_lr, a_g, b_q, b_f, b_i, b_g, c_q, c_k, c_v = [
        w[:, offs[n]:offs[n + 1]] for n in range(12)]
    lr = jnp.pad(a_lr, ((0, 0), (0, LANES - GLA_RANK)))
    return jnp.concatenate([c_v, a_q, a_k, a_v, a_g, b_q, b_f, b_i, b_g, c_q, c_k, lr], axis=1).astype(BF16)


def _tile_rows(seq, cap):
    t = cap
    while seq % t:
        t //= 2
    return t


def kernel(x_prompt, x_sample, state_gla, state_hgrn, cache_k_win, cache_v_win, norm_mix_g, w_in,
           gla_w_gate_up, gla_b_gate, gla_norm_g, hgrn_lb_logits, hgrn_norm_g, attn_q_norm_g,
           attn_k_norm_g, w_out, norm_ffn_g, ffn_w_gate, ffn_w_up, ffn_w_down, moe_w_router,
           moe_w_gate, moe_w_up, moe_w_down):
    depth = w_in.shape[0]
    bp, lp, _ = x_prompt.shape
    bs, ls, _ = x_sample.shape
    buf = cache_k_win.shape[2]
    width = ATT_HEADS * HEAD_DIM
    assert lp % (BAND * max(DILATIONS)) == 0 and lp % CHUNK == 0 and ls % CHUNK != 0 and ls == SUBLANES
    assert buf >= BAND * max(DILATIONS) and lp % min(buf, lp) == 0

    lb_w = jax.nn.softmax(hgrn_lb_logits.astype(F32), axis=0)
    lower_bounds = jnp.cumsum(lb_w, axis=0) - lb_w[0]

    ones_att = _block_diag_ones(LANES, HEAD_DIM)
    ones_gla = _block_diag_ones(GLA_HEADS * GLA_DV, GLA_DV)
    ones_hgrn = _block_diag_ones(HGRN_HEADS * HGRN_DV, HGRN_DV)
    tm_p = _tile_rows(bp * lp, 256)
    tm_s = _tile_rows(bs * ls, 256)
    tables_p = _rope_tables(jnp.arange(lp, dtype=jnp.int32))
    tables_s = _rope_tables(jnp.tile(PAST_LEN + jnp.arange(ls, dtype=jnp.int32), tm_s // ls))
    ck = jnp.transpose(cache_k_win, (0, 1, 3, 4, 2))
    cv = jnp.transpose(cache_v_win, (0, 1, 3, 4, 2))

    def sample_heads(a):
        return a.reshape(bs, ls, ATT_HEADS, HEAD_DIM)

    def sample_tail(a):
        a = jnp.transpose(sample_heads(a), (0, 2, 3, 1))
        return jnp.pad(a, ((0, 0), (0, 0), (0, 0), (LANES - ls, 0)))

    xp = x_prompt.reshape(bp * lp, D_MODEL)
    xs = x_sample.reshape(bs * ls, D_MODEL)
    zero_gla = jnp.zeros((bp, GLA_HEADS, GLA_DK, GLA_DV), F32)
    zero_hgrn = jnp.zeros((bp, HGRN_HEADS, HGRN_DK, HGRN_DV), F32)
    gla_p, hgrn_p, kw_p, vw_p, gla_s, hgrn_s = [], [], [], [], [], []
    caches_s = None
    keep = min(buf, lp)

    for layer in range(depth):
        w_proj = _rearrange_w_in(w_in[layer])
        g_mix = norm_mix_g[layer].reshape(1, D_MODEL)
        wup = jnp.pad(gla_w_gate_up[layer], ((0, LANES - GLA_RANK), (0, 0))).astype(BF16)
        bias = gla_b_gate[layer].reshape(1, -1)
        gain_a = jnp.tile(gla_norm_g[layer], GLA_HEADS).reshape(1, -1)
        gain_b = jnp.tile(hgrn_norm_g[layer], HGRN_HEADS).reshape(1, -1)
        lb = lower_bounds[layer].reshape(1, -1)
        gq = jnp.tile(attn_q_norm_g[layer], 2).reshape(1, LANES)
        gk = jnp.tile(attn_k_norm_g[layer], 2).reshape(1, LANES)
        wout = w_out[layer].astype(BF16)
        g_ffn = norm_ffn_g[layer].reshape(1, D_MODEL)

        mixed = []
        for which, x, batch, seq, tm in (("p", xp, bp, lp, tm_p), ("s", xs, bs, ls, tm_s)):
            proj = _inproj(x, g_mix, w_proj, tm)
            if which == "p":
                qn, kn, k_win, v_win = _qkrope_window(proj, gq, gk, tables_p, ones_att, batch, seq, keep)
                s0_a, s0_b, block, chunk, seqs = zero_gla, zero_hgrn, min(seq, 4 * CHUNK), CHUNK, 1
            else:
                qn, kn = _qkrope(proj, gq, gk, tables_s, ones_att, tm, 1)
                s0_a, s0_b, block, chunk, seqs = state_gla[layer], state_hgrn[layer], seq, seq, math.gcd(batch, SAMPLE_SEQS)
            o_a, s_a = _gated_linear("gla", proj, batch, seq, s0_a, gain_a, ones_gla, (wup, bias), block, chunk, seqs)
            o_b, s_b = _gated_linear("hgrn", proj, batch, seq, s0_b, gain_b, ones_hgrn, (lb,), block, chunk, seqs)
            if which == "p":
                o_c = _attn_prompt(qn, kn, proj, COL_CV, batch, seq)
                gla_p.append(s_a)
                hgrn_p.append(s_b)
                kw_p.append(k_win)
                vw_p.append(v_win)
            else:
                o_c, ck_new, cv_new = _attn_sample(
                    jnp.transpose(sample_heads(qn), (0, 2, 1, 3)), sample_tail(kn),
                    sample_tail(proj[:, COL_CV:COL_CV + width]), ck, cv, layer, caches_s, seq, 2)
                o_c = jnp.transpose(o_c, (0, 2, 1, 3)).reshape(batch * seq, width)
                caches_s = (ck_new, cv_new)
                gla_s.append(s_a)
                hgrn_s.append(s_b)
            mixed.append((o_a, o_b, o_c))

        j = layer // 2
        new_x = []
        for (o_a, o_b, o_c), x, tm in ((mixed[0], xp, tm_p), (mixed[1], xs, tm_s)):
            if layer % 2 == 0:
                new_x.append(_mix_ffn(x, o_a, o_b, o_c, wout, g_ffn, ffn_w_gate[j].astype(BF16),
                                      ffn_w_up[j].astype(BF16), ffn_w_down[j].astype(BF16), tm))
            else:
                wr = jnp.pad(moe_w_router[j], ((0, 0), (0, LANES - N_EXPERTS)))
                wr_hi = wr.astype(BF16)
                wr = jnp.concatenate([wr_hi, (wr - wr_hi.astype(F32)).astype(BF16)], axis=1)
                x2, h_rows, idx, gates = _mix_router(x, o_a, o_b, o_c, wout, g_ffn, wr, tm)
                new_x.append(_moe(x2, h_rows, idx, gates, moe_w_gate[j].astype(BF16), moe_w_up[j].astype(BF16),
                                  moe_w_down[j].astype(BF16), MOE_TILE, tm))
        xp, xs = new_x

    return (xp.reshape(bp, lp, D_MODEL), xs.reshape(bs, ls, D_MODEL),
            jnp.stack(gla_p), jnp.stack(hgrn_p),
            jnp.transpose(jnp.stack(kw_p), (0, 1, 4, 2, 3)), jnp.transpose(jnp.stack(vw_p), (0, 1, 4, 2, 3)),
            jnp.stack(gla_s), jnp.stack(hgrn_s),
            jnp.transpose(caches_s[0], (0, 1, 4, 2, 3)), jnp.transpose(caches_s[1], (0, 1, 4, 2, 3)))
```

```python
import functools
import math

import numpy as np
import jax
import jax.numpy as jnp
from jax import lax
from jax.experimental import pallas as pl
from jax.experimental.pallas import tpu as pltpu

F32 = jnp.float32
BF16 = jnp.bfloat16

D_MODEL = 1024
HEAD_DIM = 64
GLA_HEADS = 4
GLA_DK = 32
GLA_DV = 64
GLA_RANK = 16
GLA_TAU = 16.0
HGRN_HEADS = 4
HGRN_DK = 64
HGRN_DV = 64
ATT_HEADS = 8
DILATIONS = (1, 4, 16)
BAND = 128
ROT_DIM = 16
ROPE_THETA = 500000.0
CHUNK = 64
N_EXPERTS = 8
EPS = 1e-6
PAST_LEN = 8192
NEG = -1e30

LANES = 128
SUBLANES = 8
VMEM_LIMIT = 56 * 1024 * 1024
ROW_TILES = D_MODEL // LANES
SAMPLE_SEQS = 16
MOE_TILE = 256

COL_CV = 0
COL_AQ, COL_AK, COL_AV, COL_AG = 512, 640, 768, 1024
COL_BQ, COL_BF, COL_BI, COL_BG = 1280, 1536, 1792, 2048
COL_CQ, COL_CK = 2304, 2816
COL_LR = 3328
N_PROJ = 3456


def _cparams(*sem):
    return pltpu.CompilerParams(dimension_semantics=sem, vmem_limit_bytes=VMEM_LIMIT)


def _sigmoid(x):
    return 1.0 / (1.0 + jnp.exp(-x))


def _silu(x):
    return x * _sigmoid(x)


def _log_sigmoid(x):
    return -(jnp.maximum(-x, 0.0) + jnp.log(1.0 + jnp.exp(-jnp.abs(x))))


def _dot(a, b):
    return jnp.dot(a, b, preferred_element_type=F32)


def _dot_nt(a, b):
    return lax.dot_general(a, b, (((1,), (1,)), ((), ())), preferred_element_type=F32)


def _dot_tn(a, b):
    return lax.dot_general(a, b, (((0,), (0,)), ((), ())), preferred_element_type=F32)


def _group_sum(x, ones_bd):
    hi = x.astype(BF16)
    lo = (x - hi.astype(F32)).astype(BF16)
    return _dot(hi, ones_bd) + _dot(lo, ones_bd)


def _inproj_kernel(x_ref, g_ref, w_ref, o_ref):
    x = x_ref[...]
    ms = jnp.mean(x * x, axis=-1, keepdims=True)
    h = (x * lax.rsqrt(ms + EPS)) * g_ref[...]
    o_ref[...] = _dot(h.astype(BF16), w_ref[...])


def _inproj(x, g, w, tm):
    t = x.shape[0]
    return pl.pallas_call(
        _inproj_kernel,
        grid=(t // tm,),
        in_specs=[pl.BlockSpec((tm, D_MODEL), lambda i: (i, 0)),
                  pl.BlockSpec((1, D_MODEL), lambda i: (0, 0)),
                  pl.BlockSpec((D_MODEL, N_PROJ), lambda i: (0, 0))],
        out_specs=pl.BlockSpec((tm, N_PROJ), lambda i: (i, 0)),
        out_shape=jax.ShapeDtypeStruct((t, N_PROJ), F32),
        compiler_params=_cparams("parallel"),
        name="inproj",
    )(x, g, w)


def _qkrope_kernel(q_ref, k_ref, *rest, window):
    if window:
        v_ref, gq_ref, gk_ref, cos_ref, sa_ref, sb_ref, ones_ref, qo_ref, ko_ref, kt_ref, vt_ref = rest
    else:
        gq_ref, gk_ref, cos_ref, sa_ref, sb_ref, ones_ref, qo_ref, ko_ref = rest
    cos, sa, sb = cos_ref[...], sa_ref[...], sb_ref[...]
    ones_bd = ones_ref[...]

    def norm_rope(x, g):
        ms = _group_sum(x * x, ones_bd) * (1.0 / HEAD_DIM)
        y = (x * lax.rsqrt(ms + EPS)) * g
        up = pltpu.roll(y, LANES - ROT_DIM // 2, axis=1)
        dn = pltpu.roll(y, ROT_DIM // 2, axis=1)
        return y * cos + up * sa + dn * sb

    qo_ref[...] = norm_rope(q_ref[...], gq_ref[...]) * (HEAD_DIM ** -0.5)
    k = norm_rope(k_ref[...], gk_ref[...])
    ko_ref[...] = k
    if window:
        @pl.when(pl.program_id(2) == pl.num_programs(2) - 1)
        def _():
            kt_ref[0] = k.T.reshape(kt_ref.shape[1:])
            vt_ref[0] = v_ref[...].T.reshape(vt_ref.shape[1:])


def _rope_tables(pos):
    half = ROT_DIM // 2
    inv_freq = ROPE_THETA ** (-jnp.arange(0, ROT_DIM, 2, dtype=F32) / ROT_DIM)
    ang = pos.astype(F32)[:, None] * inv_freq[None, :]
    cos, sin = jnp.cos(ang), jnp.sin(ang)
    n = pos.shape[0]
    rest = HEAD_DIM - ROT_DIM
    c = jnp.concatenate([cos, cos, jnp.ones((n, rest), F32)], axis=1)
    sa = jnp.concatenate([-sin, jnp.zeros((n, half + rest), F32)], axis=1)
    sb = jnp.concatenate([jnp.zeros((n, half), F32), sin, jnp.zeros((n, rest), F32)], axis=1)
    return tuple(jnp.tile(a, (1, 2)) for a in (c, sa, sb))


def _qkrope(proj, gq, gk, tables, ones_bd, tm, table_blocks):
    t = proj.shape[0]
    cq, ck = COL_CQ // LANES, COL_CK // LANES
    tab_spec = pl.BlockSpec((tm, LANES), lambda i, j: (i % table_blocks, 0))
    row_spec = pl.BlockSpec((1, LANES), lambda i, j: (0, 0))
    out_spec = pl.BlockSpec((tm, LANES), lambda i, j: (i, j))
    out_sds = jax.ShapeDtypeStruct((t, ATT_HEADS * HEAD_DIM), F32)
    return pl.pallas_call(
        functools.partial(_qkrope_kernel, window=False),
        grid=(t // tm, ATT_HEADS // 2),
        in_specs=[pl.BlockSpec((tm, LANES), lambda i, j: (i, cq + j)),
                  pl.BlockSpec((tm, LANES), lambda i, j: (i, ck + j)),
                  row_spec, row_spec, tab_spec, tab_spec, tab_spec,
                  pl.BlockSpec((LANES, LANES), lambda i, j: (0, 0))],
        out_specs=[out_spec, out_spec],
        out_shape=[out_sds, out_sds],
        compiler_params=_cparams("parallel", "parallel"),
        name="qkrope",
    )(proj, proj, gq, gk, *tables, ones_bd)


def _qkrope_window(proj, gq, gk, tables, ones_bd, batch, seq, window):
    cq, ck, cv = COL_CQ // LANES, COL_CK // LANES, COL_CV // LANES
    nt = seq // window

    def tok(col):
        return pl.BlockSpec((window, LANES), lambda b, j, s: (b * nt + s, col + j))

    tab_spec = pl.BlockSpec((window, LANES), lambda b, j, s: (s, 0))
    row_spec = pl.BlockSpec((1, LANES), lambda b, j, s: (0, 0))
    win_spec = pl.BlockSpec((1, 2, HEAD_DIM, window), lambda b, j, s: (b, j, 0, 0))
    out_sds = jax.ShapeDtypeStruct((batch * seq, ATT_HEADS * HEAD_DIM), F32)
    win_sds = jax.ShapeDtypeStruct((batch, ATT_HEADS, HEAD_DIM, window), F32)
    return pl.pallas_call(
        functools.partial(_qkrope_kernel, window=True),
        grid=(batch, ATT_HEADS // 2, nt),
        in_specs=[tok(cq), tok(ck), tok(cv), row_spec, row_spec, tab_spec, tab_spec, tab_spec,
                  pl.BlockSpec((LANES, LANES), lambda b, j, s: (0, 0))],
        out_specs=[tok(0), tok(0), win_spec, win_spec],
        out_shape=[out_sds, out_sds, win_sds, win_sds],
        compiler_params=_cparams("parallel", "parallel", "arbitrary"),
        name="qkrope_window",
    )(proj, proj, proj, gq, gk, *tables, ones_bd)


def _segment_scans(g, chunk):
    rows, hk = g.shape
    n_chunks = rows // chunk
    row = lax.broadcasted_iota(jnp.int32, g.shape, 0)

    def at(x, shift):
        return pltpu.roll(x, shift % rows, axis=0)

    zero = jnp.zeros_like(g)
    c = {1: g, 2: g + jnp.where(row % 2 == 1, at(g, 1), zero)}
    c[4] = c[2] + jnp.where(row % 4 == 2, at(c[2], 1), zero) + jnp.where(row % 4 == 3, at(c[2], 2), zero)
    d = {1: zero, 2: jnp.where(row % 2 == 0, at(g, -1), zero)}
    c4_end = jnp.where(row % 4 == 0, at(c[4], -3),
                       jnp.where(row % 4 == 1, at(c[4], -2),
                                 jnp.where(row % 4 == 2, at(c[4], -1), c[4])))
    d[4] = c4_end - c[4]
    tiles = chunk // SUBLANES
    c4_t = c[4].reshape(n_chunks * tiles, SUBLANES, hk)
    row_t = row.reshape(n_chunks * tiles, SUBLANES, hk)
    c8_t = c4_t + jnp.where(row_t % 8 >= 4, c4_t[:, 3:4, :], jnp.zeros_like(c4_t))
    ends = c8_t[:, SUBLANES - 1:SUBLANES, :]
    prefix = []
    for ci in range(n_chunks):
        acc = [jnp.zeros((1, 1, hk), F32)]
        for j in range(tiles):
            acc.append(acc[-1] + ends[ci * tiles + j:ci * tiles + j + 1])
        prefix.append(acc)

    def per_tile(pick):
        return jnp.concatenate([prefix[ci][pick(j)] for ci in range(n_chunks) for j in range(tiles)], axis=0)

    b_t = c8_t + per_tile(lambda j: j)
    width = 1
    while width * SUBLANES < chunk:
        m = width * SUBLANES
        c[m] = (b_t - per_tile(lambda j: (j // width) * width)).reshape(rows, hk)
        d[m] = (per_tile(lambda j: (j // width + 1) * width) - b_t).reshape(rows, hk)
        width *= 2
    b_rest = (per_tile(lambda j: tiles) - b_t).reshape(rows, hk)
    b_last = [prefix[ci][tiles].reshape(1, hk) for ci in range(n_chunks)]
    return c, d, b_t.reshape(rows, hk), b_rest, b_last


def _pair_masks(chunk, copies):
    t = lax.broadcasted_iota(jnp.int32, (copies * chunk, chunk), 0) % chunk
    s = lax.broadcasted_iota(jnp.int32, (copies * chunk, chunk), 1)
    masks = {0: t == s}
    m = 1
    while m < chunk:
        masks[m] = (t // (2 * m) == s // (2 * m)) & (t % (2 * m) >= m) & (s % (2 * m) < m)
        m *= 2
    return masks


def _gl_block(q, k, g, v, st, heads, dk, dv, chunk):
    rows = q.shape[0]
    hk, hv = heads * dk, heads * dv
    stacked_masks = _pair_masks(chunk, heads)
    levels = sorted(stacked_masks)
    c, d, b, b_rest, b_last = _segment_scans(g, chunk)
    k_bf = k.astype(BF16)
    q_lvl = {0: q.astype(BF16)}
    k_lvl = {0: k_bf, 1: k_bf}
    for m in levels[1:]:
        q_lvl[m] = (q * jnp.exp(c[m])).astype(BF16)
        if m > 1:
            k_lvl[m] = (k * jnp.exp(d[m])).astype(BF16)
    q_in = (q * jnp.exp(b)).astype(BF16)
    k_out = (k * jnp.exp(b_rest)).astype(BF16)
    v_bf = v.astype(BF16)
    lane_k = lax.broadcasted_iota(jnp.int32, (chunk, hk), 1) // dk
    lane_v = lax.broadcasted_iota(jnp.int32, (chunk, hv), 1) // dv
    same_head = (lax.broadcasted_iota(jnp.int32, (hv, hk), 0) // dv
                 == lax.broadcasted_iota(jnp.int32, (hv, hk), 1) // dk)
    zero_k = jnp.zeros((chunk, hk), BF16)
    zero_v = jnp.zeros((chunk, hv), BF16)
    o_intra, st_inc = [], []
    for ci in range(rows // chunk):
        r = slice(ci * chunk, (ci + 1) * chunk)
        a = jnp.zeros((heads * chunk, chunk), F32)
        for m in levels:
            q_heads = jnp.concatenate([jnp.where(lane_k == h, q_lvl[m][r], zero_k) for h in range(heads)], axis=0)
            a = jnp.where(stacked_masks[m], _dot_nt(q_heads, k_lvl[m][r]), a)
        a = a.astype(BF16)
        o = None
        for h in range(heads):
            part = _dot(a[h * chunk:(h + 1) * chunk], jnp.where(lane_v == h, v_bf[r], zero_v))
            o = part if o is None else o + part
        o_intra.append(o)
        st_inc.append(jnp.where(same_head, _dot_tn(v_bf[r], k_out[r]), 0.0))
    independent = isinstance(st, list)
    outs, finals = [], []
    for ci in range(rows // chunk):
        r = slice(ci * chunk, (ci + 1) * chunk)
        cur = st[ci] if independent else st
        outs.append(o_intra[ci] + _dot_nt(q_in[r], cur.astype(BF16)))
        nxt = cur * jnp.exp(b_last[ci]) + st_inc[ci]
        if independent:
            finals.append(nxt)
        else:
            st = nxt
    return (outs[0] if len(outs) == 1 else jnp.concatenate(outs, axis=0)), (finals if independent else st)


def _head_norm_gate(o, gain, gate, ones_bd, dv):
    ms = _group_sum(o * o, ones_bd) * (1.0 / dv)
    return ((o * lax.rsqrt(ms + EPS)) * gain) * _silu(gate)


def _gl_kernel(*refs, mode, heads, dk, dv, chunk):
    if mode == "gla":
        (q_ref, k_ref, v_ref, gate_ref, lr_ref, wup_ref, bias_ref, gain_ref, s0_ref, ones_ref,
         o_ref, sT_ref, st_ref) = refs
        q = q_ref[...] * (dk ** -0.5)
        k = k_ref[...]
        g = _log_sigmoid(_dot(lr_ref[...].astype(BF16), wup_ref[...]) + bias_ref[...]) * (1.0 / GLA_TAU)
    else:
        (q_ref, f_ref, v_ref, gate_ref, lb_ref, gain_ref, s0_ref, ones_ref,
         o_ref, sT_ref, st_ref) = refs
        lb = lb_ref[...]
        q = _silu(q_ref[...])
        z = f_ref[...]
        g = jnp.log(lb + (1.0 - lb) * _sigmoid(z))
        k = (1.0 - lb) * _sigmoid(-z)
    seqs = s0_ref.shape[0]
    if seqs > 1:
        o, finals = _gl_block(q, k, g, v_ref[...], [s0_ref[n] for n in range(seqs)], heads, dk, dv, chunk)
        for n in range(seqs):
            sT_ref[n] = finals[n]
    else:
        i = pl.program_id(1)

        @pl.when(i == 0)
        def _():
            st_ref[...] = s0_ref[0]

        o, st = _gl_block(q, k, g, v_ref[...], st_ref[...], heads, dk, dv, chunk)
        st_ref[...] = st

        @pl.when(i == pl.num_programs(1) - 1)
        def _():
            sT_ref[0] = st
    o_ref[...] = _head_norm_gate(o, gain_ref[...], gate_ref[...], ones_ref[...], dv)


def _embed_state(s):
    b, h, k, v = s.shape
    eye = jnp.eye(h, dtype=s.dtype)
    return jnp.einsum("bhkv,hg->bhvgk", s, eye).reshape(b, h * v, h * k)


def _extract_state(st, h, k, v):
    b = st.shape[0]
    st = st.reshape(b, h, v, h, k)
    return jnp.stack([st[:, i, :, i, :] for i in range(h)], axis=1).transpose(0, 1, 3, 2)


def _gated_linear(mode, proj, batch, seq, s0, gain, ones_bd, extra, block_tokens, chunk, seqs=1):
    assert seqs == 1 or (block_tokens == seq == chunk and batch % seqs == 0)
    heads, dk, dv = (GLA_HEADS, GLA_DK, GLA_DV) if mode == "gla" else (HGRN_HEADS, HGRN_DK, HGRN_DV)
    hk, hv = heads * dk, heads * dv
    nb = seq // block_tokens
    tb = block_tokens * seqs

    def cols(width, col):
        return pl.BlockSpec((tb, width), lambda b, i: (b * nb + i, col // width))

    def const(shape):
        return pl.BlockSpec(shape, lambda b, i: (0,) * len(shape))

    state_spec = pl.BlockSpec((seqs, hv, hk), lambda b, i: (b, 0, 0))
    if mode == "gla":
        wup, bias = extra
        in_specs = [cols(hk, COL_AQ), cols(hk, COL_AK), cols(hv, COL_AV), cols(hv, COL_AG), cols(LANES, COL_LR),
                    const((LANES, hk)), const((1, hk)), const((1, hv)), state_spec, const((hv, hv))]
        args = (proj, proj, proj, proj, proj, wup, bias, gain, _embed_state(s0), ones_bd)
    else:
        (lb,) = extra
        in_specs = [cols(hk, COL_BQ), cols(hk, COL_BF), cols(hv, COL_BI), cols(hv, COL_BG),
                    const((1, hk)), const((1, hv)), state_spec, const((hv, hv))]
        args = (proj, proj, proj, proj, lb, gain, _embed_state(s0), ones_bd)
    o, st = pl.pallas_call(
        functools.partial(_gl_kernel, mode=mode, heads=heads, dk=dk, dv=dv, chunk=chunk),
        grid=(batch // seqs, nb),
        in_specs=in_specs,
        out_specs=[pl.BlockSpec((tb, hv), lambda b, i: (b * nb + i, 0)), state_spec],
        out_shape=[jax.ShapeDtypeStruct((batch * seq, hv), F32),
                   jax.ShapeDtypeStruct((batch, hv, hk), F32)],
        scratch_shapes=[pltpu.VMEM((hv, hk), F32)],
        compiler_params=_cparams("parallel", "arbitrary"),
        name=mode + "_scan",
    )(*args)
    return o, _extract_state(st, heads, dk, dv)


def _attn_prompt_kernel(q_ref, k_ref, v_ref, o_ref, qr_ref, kt_ref, kr_ref, vt_ref, vr_ref,
                        m_ref, l_ref, acc_ref, *, seq):
    n_res = max(DILATIONS)
    per_res = seq // n_res
    head0 = lax.broadcasted_iota(jnp.int32, (per_res, LANES), 1) < HEAD_DIM
    for r in range(n_res):
        tok = slice(r * per_res, (r + 1) * per_res)
        strided = pl.ds(r, per_res, stride=n_res)
        k_t, v_t = k_ref[tok, :], v_ref[tok, :]
        k_r, v_r = k_ref[strided, :], v_ref[strided, :]
        qr_ref[tok, :] = q_ref[strided, :]
        for h in range(2):
            mine = head0 if h == 0 else ~head0
            kt_ref[h, tok, :] = jnp.where(mine, k_t, 0.0).astype(BF16)
            kr_ref[h, tok, :] = jnp.where(mine, k_r, 0.0).astype(BF16)
            vt_ref[h, tok, :] = jnp.where(mine, v_t, 1.0).astype(BF16)
            vr_ref[h, tok, :] = jnp.where(mine, v_r, 1.0).astype(BF16)
    m_ref[...] = jnp.full(m_ref.shape, NEG, F32)
    l_ref[...] = jnp.zeros(l_ref.shape, F32)
    acc_ref[...] = jnp.zeros(acc_ref.shape, F32)

    head0 = lax.broadcasted_iota(jnp.int32, (BAND, LANES), 1) < HEAD_DIM
    qi = lax.broadcasted_iota(jnp.int32, (BAND, 2 * BAND), 0)
    ki = lax.broadcasted_iota(jnp.int32, (BAND, 2 * BAND), 1)

    for dil in DILATIONS:
        n_runs = n_res // dil
        run = SUBLANES * dil
        keys_by_token = dil == 1

        def seq_index(u, n_runs=n_runs, run=run):
            return n_runs * (u % run) + u // run

        kj = ki if keys_by_token else seq_index(ki % BAND) + BAND * (ki // BAND)
        rel = BAND + seq_index(qi) - kj
        in_band = (rel >= 0) & (rel <= BAND)
        bias_prev = jnp.where(in_band, 0.0, NEG).astype(F32)
        bias_first = jnp.where(in_band & (ki >= BAND), 0.0, NEG).astype(F32)

        def body(idx, carry, dil=dil, n_runs=n_runs, run=run, keys_by_token=keys_by_token,
                 bias_prev=bias_prev, bias_first=bias_first):
            res = idx % dil
            blk = idx // dil
            starts = [pl.multiple_of((res + dil * c) * per_res + run * blk, run) for c in range(n_runs)]
            prevs = [pl.multiple_of((res + dil * c) * per_res + run * jnp.maximum(blk - 1, 0), run)
                     for c in range(n_runs)]

            def gather(ref, offs, *lead):
                parts = [ref[lead + (pl.ds(o, run), slice(None))] for o in offs]
                return parts[0] if len(parts) == 1 else jnp.concatenate(parts, axis=0)

            def scatter(ref, offs, val, *lead):
                for c, o in enumerate(offs):
                    ref[lead + (pl.ds(o, run), slice(None))] = val[c * run:(c + 1) * run, :]

            q = gather(qr_ref, starts).astype(BF16)
            bias = jnp.where(blk > 0, bias_prev, bias_first)
            alphas, pvs = [], []
            for h in range(2):
                if keys_by_token:
                    cur = pl.multiple_of(blk * BAND, BAND)
                    prev = pl.multiple_of(jnp.maximum(blk - 1, 0) * BAND, BAND)
                    k2 = jnp.concatenate([kt_ref[h, pl.ds(prev, BAND), :], kt_ref[h, pl.ds(cur, BAND), :]], axis=0)
                    v2 = jnp.concatenate([vt_ref[h, pl.ds(prev, BAND), :], vt_ref[h, pl.ds(cur, BAND), :]], axis=0)
                else:
                    k2 = jnp.concatenate([gather(kr_ref, prevs, h), gather(kr_ref, starts, h)], axis=0)
                    v2 = jnp.concatenate([gather(vr_ref, prevs, h), gather(vr_ref, starts, h)], axis=0)
                s = _dot_nt(q, k2) + bias
                m_prev = gather(m_ref, starts, h)
                m_new = jnp.maximum(m_prev, jnp.max(s, axis=-1, keepdims=True))
                scatter(m_ref, starts, m_new, h)
                p = jnp.exp(s - jnp.concatenate([m_new, m_new], axis=1))
                alphas.append(jnp.exp(m_prev - m_new))
                pvs.append(_dot(p.astype(BF16), v2))
            scatter(l_ref, starts, jnp.where(head0, alphas[1], alphas[0]) * gather(l_ref, starts)
                    + jnp.where(head0, pvs[1], pvs[0]))
            scatter(acc_ref, starts, jnp.where(head0, alphas[0], alphas[1]) * gather(acc_ref, starts)
                    + jnp.where(head0, pvs[0], pvs[1]))
            return carry

        lax.fori_loop(0, seq // BAND, body, 0, unroll=8)

    for r in range(n_res):
        tok = slice(r * per_res, (r + 1) * per_res)
        o_ref[pl.ds(r, per_res, stride=n_res), :] = acc_ref[tok, :] / pltpu.roll(l_ref[tok, :], HEAD_DIM, axis=1)


def _attn_prompt(q, k, v_src, v_col, batch, seq):
    spec = pl.BlockSpec((seq, LANES), lambda b, j: (b, j))
    vc = v_col // LANES
    return pl.pallas_call(
        functools.partial(_attn_prompt_kernel, seq=seq),
        grid=(batch, ATT_HEADS // 2),
        in_specs=[spec, spec, pl.BlockSpec((seq, LANES), lambda b, j: (b, vc + j))],
        out_specs=spec,
        out_shape=jax.ShapeDtypeStruct((batch * seq, ATT_HEADS * HEAD_DIM), F32),
        scratch_shapes=[pltpu.VMEM((seq, LANES), F32),
                        pltpu.VMEM((2, seq, LANES), BF16), pltpu.VMEM((2, seq, LANES), BF16),
                        pltpu.VMEM((2, seq, LANES), BF16), pltpu.VMEM((2, seq, LANES), BF16),
                        pltpu.VMEM((2, seq, LANES), F32), pltpu.VMEM((seq, LANES), F32),
                        pltpu.VMEM((seq, LANES), F32)],
        compiler_params=_cparams("parallel", "parallel"),
        name="attn_prompt",
    )(q, k, v_src)


def _sample_counts(steps, buf):
    t = np.arange(steps)[:, None]

    def count(dist, live):
        cnt = np.zeros(dist.shape, np.float32)
        for dil in DILATIONS:
            cnt += (live & (dist >= 0) & (dist % dil == 0) & (dist <= BAND * dil)).astype(np.float32)
        return cnt

    lane = np.arange(LANES)[None, :]
    return (count(buf + t - np.arange(buf)[None, :], True),
            count(t - (lane - (LANES - steps)), lane >= LANES - steps))


def _attn_sample_kernel(q_ref, kn_ref, vn_ref, kc_ref, vc_ref, cc_ref, cn_ref, *rest, steps, aliased):
    if aliased:
        rest = rest[2:]
    o_ref, ko_ref, vo_ref = rest
    q = q_ref[0].astype(BF16)
    k_old, v_old = kc_ref[0, 0], vc_ref[0, 0]
    k_new, v_new = kn_ref[0], vn_ref[0]
    cc, cn = cc_ref[...][None], cn_ref[...][None]

    def scores(keys):
        return lax.dot_general(q, keys.astype(BF16), (((2,), (1,)), ((0,), (0,))), preferred_element_type=F32)

    def weighted(p, values):
        return lax.dot_general(p.astype(BF16), values.astype(BF16), (((2,), (2,)), ((0,), (0,))),
                               preferred_element_type=F32)

    s_old = jnp.where(cc > 0.0, scores(k_old), NEG)
    s_new = jnp.where(cn > 0.0, scores(k_new), NEG)
    m = jnp.maximum(jnp.max(s_old, axis=-1, keepdims=True), jnp.max(s_new, axis=-1, keepdims=True))
    p_old = cc * jnp.exp(s_old - m)
    p_new = cn * jnp.exp(s_new - m)
    den = jnp.sum(p_old, axis=-1, keepdims=True) + jnp.sum(p_new, axis=-1, keepdims=True)
    o_ref[0] = (weighted(p_old, v_old) + weighted(p_new, v_new)) / den

    buf = k_old.shape[-1]
    lane = lax.broadcasted_iota(jnp.int32, k_new.shape, 2)
    for old, new, out_ref in ((k_old, k_new, ko_ref), (v_old, v_new, vo_ref)):
        shifted = pltpu.roll(old, buf - steps, axis=2)
        out_ref[0, 0, :, :, 0:buf - LANES] = shifted[:, :, 0:buf - LANES]
        out_ref[0, 0, :, :, buf - LANES:buf] = jnp.where(lane >= LANES - steps, new, shifted[:, :, buf - LANES:buf])


def _attn_sample(q, k_new, v_new, cache_k, cache_v, layer, prev_out, steps, head_split):
    depth, batch, heads, hd, buf = cache_k.shape
    hb = heads // head_split
    cc, cn = (jnp.asarray(c) for c in _sample_counts(steps, buf))
    q_spec = pl.BlockSpec((1, hb, steps, hd), lambda b, j: (b, j, 0, 0))
    new_spec = pl.BlockSpec((1, hb, hd, LANES), lambda b, j: (b, j, 0, 0))
    cache_spec = pl.BlockSpec((1, 1, hb, hd, buf), lambda b, j: (layer, b, j, 0, 0))
    in_specs = [q_spec, new_spec, new_spec, cache_spec, cache_spec,
                pl.BlockSpec((steps, buf), lambda b, j: (0, 0)), pl.BlockSpec((steps, LANES), lambda b, j: (0, 0))]
    args = [q, k_new, v_new, cache_k, cache_v, cc, cn]
    aliases = {}
    if prev_out is not None:
        any_spec = pl.BlockSpec(memory_space=pl.ANY)
        in_specs += [any_spec, any_spec]
        args += list(prev_out)
        aliases = {7: 1, 8: 2}
    cache_sds = jax.ShapeDtypeStruct(cache_k.shape, F32)
    return pl.pallas_call(
        functools.partial(_attn_sample_kernel, steps=steps, aliased=prev_out is not None),
        grid=(batch, head_split),
        in_specs=in_specs,
        out_specs=[q_spec, cache_spec, cache_spec],
        out_shape=[jax.ShapeDtypeStruct(q.shape, F32), cache_sds, cache_sds],
        input_output_aliases=aliases,
        compiler_params=_cparams("arbitrary", "arbitrary"),
        name="attn_sample",
    )(*args)


def _outproj(oa_ref, ob_ref, oc_ref, wout_ref, x_ref):
    na, nb = oa_ref.shape[1], ob_ref.shape[1]
    mixed = (_dot(oa_ref[...].astype(BF16), wout_ref[0:na, :])
             + _dot(ob_ref[...].astype(BF16), wout_ref[na:na + nb, :])
             + _dot(oc_ref[...].astype(BF16), wout_ref[na + nb:, :]))
    return x_ref[...] + mixed


def _rmsnorm(x, g):
    return (x * lax.rsqrt(jnp.mean(x * x, axis=-1, keepdims=True) + EPS)) * g


def _mix_ffn_kernel(x_ref, oa_ref, ob_ref, oc_ref, wout_ref, g_ref, wg_ref, wu_ref, wd_ref, o_ref):
    x2 = _outproj(oa_ref, ob_ref, oc_ref, wout_ref, x_ref)
    h = _rmsnorm(x2, g_ref[...]).astype(BF16)
    act = _silu(_dot(h, wg_ref[...])) * _dot(h, wu_ref[...])
    o_ref[...] = x2 + _dot(act.astype(BF16), wd_ref[...])


def _mix_ffn(x, oa, ob, oc, wout, g, wg, wu, wd, tm):
    t = x.shape[0]
    d_ff = wg.shape[1]

    def rows(width):
        return pl.BlockSpec((tm, width), lambda i: (i, 0))

    def const(shape):
        return pl.BlockSpec(shape, lambda i: (0, 0), pipeline_mode=pl.Buffered(1))

    return pl.pallas_call(
        _mix_ffn_kernel,
        grid=(t // tm,),
        in_specs=[rows(D_MODEL), rows(oa.shape[1]), rows(ob.shape[1]), rows(oc.shape[1]),
                  const((D_MODEL, D_MODEL)), const((1, D_MODEL)),
                  const((D_MODEL, d_ff)), const((D_MODEL, d_ff)), const((d_ff, D_MODEL))],
        out_specs=rows(D_MODEL),
        out_shape=jax.ShapeDtypeStruct((t, D_MODEL), F32),
        compiler_params=_cparams("parallel"),
        name="mix_ffn",
    )(x, oa, ob, oc, wout, g, wg, wu, wd)


def _mix_router_kernel(x_ref, oa_ref, ob_ref, oc_ref, wout_ref, g_ref, wr_ref, x2_ref, h_ref, idx_ref, gate_ref):
    tm = x_ref.shape[0]
    x2 = _outproj(oa_ref, ob_ref, oc_ref, wout_ref, x_ref)
    x2_ref[...] = x2
    h = _rmsnorm(x2, g_ref[...])
    for s in range(D_MODEL // LANES):
        h_ref[pl.ds(s, tm, stride=D_MODEL // LANES), :] = h[:, s * LANES:(s + 1) * LANES]
    h_hi = h.astype(BF16)
    h_lo = (h - h_hi.astype(F32)).astype(BF16)
    by_hi = _dot(h_hi, wr_ref[...])
    logits = by_hi[:, 0:LANES] + by_hi[:, LANES:2 * LANES] + _dot(h_lo, wr_ref[:, 0:LANES])
    lane = lax.broadcasted_iota(jnp.int32, logits.shape, 1)
    logits = jnp.where(lane < N_EXPERTS, logits, -jnp.inf)
    v1 = jnp.max(logits, axis=-1, keepdims=True)
    i1 = jnp.min(jnp.where(logits == v1, lane, LANES), axis=-1, keepdims=True)
    rest = jnp.where(lane == i1, -jnp.inf, logits)
    v2 = jnp.max(rest, axis=-1, keepdims=True)
    i2 = jnp.min(jnp.where(rest == v2, lane, LANES), axis=-1, keepdims=True)
    e2 = jnp.exp(v2 - v1)
    idx_ref[...] = jnp.where(lane == 0, i1, jnp.where(lane == 1, i2, 0))
    gate_ref[...] = jnp.where(lane == 0, 1.0 / (1.0 + e2), jnp.where(lane == 1, e2 / (1.0 + e2), 0.0))


def _mix_router(x, oa, ob, oc, wout, g, wr, tm):
    t = x.shape[0]

    def rows(width):
        return pl.BlockSpec((tm, width), lambda i: (i, 0))

    def const(shape):
        return pl.BlockSpec(shape, lambda i: (0, 0))

    return pl.pallas_call(
        _mix_router_kernel,
        grid=(t // tm,),
        in_specs=[rows(D_MODEL), rows(oa.shape[1]), rows(ob.shape[1]), rows(oc.shape[1]),
                  const((D_MODEL, D_MODEL)), const((1, D_MODEL)), const((D_MODEL, 2 * LANES))],
        out_specs=[rows(D_MODEL), pl.BlockSpec((tm * ROW_TILES, LANES), lambda i: (i, 0)), rows(LANES), rows(LANES)],
        out_shape=[jax.ShapeDtypeStruct((t, D_MODEL), F32), jax.ShapeDtypeStruct((t * ROW_TILES, LANES), F32),
                   jax.ShapeDtypeStruct((t, LANES), jnp.int32), jax.ShapeDtypeStruct((t, LANES), F32)],
        compiler_params=_cparams("parallel"),
        name="mix_router",
    )(x, oa, ob, oc, wout, g, wr)


def _route(idx, tile):
    t = idx.shape[0]
    n_tiles = (2 * t) // tile
    n_items = n_tiles + N_EXPERTS - 1
    expert = jnp.concatenate([idx[:, 0], idx[:, 1]])
    onehot = (expert[:, None] == jnp.arange(N_EXPERTS, dtype=jnp.int32)[None, :]).astype(jnp.int32)
    csum = jnp.cumsum(onehot, axis=0)
    rank = jnp.take_along_axis(csum, expert[:, None], axis=1)[:, 0] - 1
    group_end = jnp.cumsum(csum[-1])
    group_start = group_end - csum[-1]
    pos = (group_start[expert] + rank).astype(jnp.int32)
    tile_lo = jnp.arange(n_tiles, dtype=jnp.int32)[:, None] * tile
    present = (group_start[None, :] < tile_lo + tile) & (group_end[None, :] > tile_lo)
    seen = jnp.cumsum(present.reshape(-1).astype(jnp.int32))
    k = jnp.arange(n_items, dtype=jnp.int32)
    flat = jnp.sum((seen[None, :] <= jnp.minimum(k, seen[-1] - 1)[:, None]).astype(jnp.int32), axis=1)
    item_tile, item_expert = flat // N_EXPERTS, flat % N_EXPERTS
    lo = jnp.clip(group_start[item_expert] - item_tile * tile, 0, tile)
    hi = jnp.where(k < seen[-1], jnp.clip(group_end[item_expert] - item_tile * tile, 0, tile), lo)
    first = jnp.concatenate([jnp.ones((1,), jnp.int32), (item_tile[1:] != item_tile[:-1]).astype(jnp.int32)])
    return pos, tuple(a.astype(jnp.int32) for a in (item_tile, item_expert, lo, hi, first))


def _moe_dispatch_kernel(pos_ref, h_ref, x_hbm, buf, sem, *, t):
    tm = h_ref.shape[0] // ROW_TILES
    i = pl.program_id(0)
    n = pl.num_programs(0)
    slot = i % 2

    def row_copy(step, sl, j, r):
        a = j * t + step * tm + r
        return pltpu.make_async_copy(buf.at[sl, pl.ds(pl.multiple_of(r * ROW_TILES, ROW_TILES), ROW_TILES), :],
                                     x_hbm.at[pl.ds(pl.multiple_of(pos_ref[a] * ROW_TILES, ROW_TILES), ROW_TILES), :],
                                     sem.at[sl])

    def each_row(fn):
        def body(r, carry):
            fn(0, r)
            fn(1, r)
            return carry
        lax.fori_loop(0, tm, body, 0, unroll=4)

    @pl.when(i >= 2)
    def _():
        each_row(lambda j, r: row_copy(i - 2, slot, j, r).wait())

    buf[slot] = h_ref[...]
    each_row(lambda j, r: row_copy(i, slot, j, r).start())

    @pl.when(i == n - 1)
    def _():
        @pl.when(n >= 2)
        def _():
            each_row(lambda j, r: row_copy(i - 1, 1 - slot, j, r).wait())
        each_row(lambda j, r: row_copy(i, slot, j, r).wait())


def _moe_dispatch(h_rows, pos, tm):
    a_n = pos.shape[0]
    t = a_n // 2
    grid_spec = pltpu.PrefetchScalarGridSpec(
        num_scalar_prefetch=1, grid=(t // tm,),
        in_specs=[pl.BlockSpec((tm * ROW_TILES, LANES), lambda i, p: (i, 0))],
        out_specs=pl.BlockSpec(memory_space=pl.ANY),
        scratch_shapes=[pltpu.VMEM((2, tm * ROW_TILES, LANES), F32), pltpu.SemaphoreType.DMA((2,))])
    return pl.pallas_call(
        functools.partial(_moe_dispatch_kernel, t=t),
        grid_spec=grid_spec,
        out_shape=jax.ShapeDtypeStruct((a_n * ROW_TILES, LANES), F32),
        compiler_params=_cparams("arbitrary"),
        name="moe_dispatch",
    )(pos, h_rows)


def _moe_expert_kernel(it_ref, ie_ref, lo_ref, hi_ref, first_ref, x_ref, wg_ref, wu_ref, wd_ref, y_ref, *, tile):
    del it_ref, ie_ref
    k = pl.program_id(0)
    x = jnp.concatenate([x_ref[pl.ds(s, tile, stride=ROW_TILES), :] for s in range(ROW_TILES)], axis=1)
    x = x.astype(BF16)
    act = _silu(_dot(x, wg_ref[0])) * _dot(x, wu_ref[0])
    y = _dot(act.astype(BF16), wd_ref[0])
    row = lax.broadcasted_iota(jnp.int32, (tile, LANES), 0)
    mine = (row >= lo_ref[k]) & (row < hi_ref[k])

    @pl.when(first_ref[k] == 1)
    def _():
        for s in range(ROW_TILES):
            y_ref[pl.ds(s, tile, stride=ROW_TILES), :] = jnp.where(mine, y[:, s * LANES:(s + 1) * LANES], 0.0)

    @pl.when(first_ref[k] != 1)
    def _():
        for s in range(ROW_TILES):
            rows = pl.ds(s, tile, stride=ROW_TILES)
            y_ref[rows, :] = jnp.where(mine, y[:, s * LANES:(s + 1) * LANES], y_ref[rows, :])


def _moe_experts(x_rows, items, wg, wu, wd, tile):
    n_items = items[0].shape[0]
    _, _, d_ff = wg.shape
    rows_spec = pl.BlockSpec((tile * ROW_TILES, LANES), lambda k, it, ie, lo, hi, fi: (it[k], 0))

    def w_spec(shape):
        return pl.BlockSpec((1,) + shape, lambda k, it, ie, lo, hi, fi: (ie[k], 0, 0))

    grid_spec = pltpu.PrefetchScalarGridSpec(
        num_scalar_prefetch=5, grid=(n_items,),
        in_specs=[rows_spec, w_spec((D_MODEL, d_ff)), w_spec((D_MODEL, d_ff)), w_spec((d_ff, D_MODEL))],
        out_specs=rows_spec)
    return pl.pallas_call(
        functools.partial(_moe_expert_kernel, tile=tile),
        grid_spec=grid_spec,
        out_shape=jax.ShapeDtypeStruct(x_rows.shape, F32),
        compiler_params=_cparams("arbitrary"),
        name="moe_experts",
    )(*items, x_rows, wg, wu, wd)


def _moe_combine_kernel(pos_ref, x2_ref, gate_ref, y_hbm, o_ref, ybuf, sem, *, t):
    tm = x2_ref.shape[0]
    i = pl.program_id(0)
    n = pl.num_programs(0)
    slot = i % 2

    def row_copy(step, sl, j, r):
        a = j * t + step * tm + r
        return pltpu.make_async_copy(y_hbm.at[pl.ds(pl.multiple_of(pos_ref[a] * ROW_TILES, ROW_TILES), ROW_TILES), :],
                                     ybuf.at[sl, j, pl.ds(pl.multiple_of(r * ROW_TILES, ROW_TILES), ROW_TILES), :],
                                     sem.at[sl])

    def each_row(fn):
        def body(r, carry):
            fn(0, r)
            fn(1, r)
            return carry
        lax.fori_loop(0, tm, body, 0, unroll=4)

    @pl.when(i == 0)
    def _():
        each_row(lambda j, r: row_copy(0, 0, j, r).start())

    @pl.when(i + 1 < n)
    def _():
        each_row(lambda j, r: row_copy(i + 1, 1 - slot, j, r).start())

    each_row(lambda j, r: row_copy(i, slot, j, r).wait())
    gates = gate_ref[...]
    g0, g1 = gates[:, 0:1], gates[:, 1:2]
    for s in range(ROW_TILES):
        lanes = slice(s * LANES, (s + 1) * LANES)
        rows = pl.ds(s, tm, stride=ROW_TILES)
        o_ref[:, lanes] = x2_ref[:, lanes] + g0 * ybuf[slot, 0, rows, :] + g1 * ybuf[slot, 1, rows, :]


def _moe_combine(x2, gates, y, pos, tm):
    t = x2.shape[0]
    grid_spec = pltpu.PrefetchScalarGridSpec(
        num_scalar_prefetch=1, grid=(t // tm,),
        in_specs=[pl.BlockSpec((tm, D_MODEL), lambda i, p: (i, 0)), pl.BlockSpec((tm, LANES), lambda i, p: (i, 0)),
                  pl.BlockSpec(memory_space=pl.ANY)],
        out_specs=pl.BlockSpec((tm, D_MODEL), lambda i, p: (i, 0)),
        scratch_shapes=[pltpu.VMEM((2, 2, tm * ROW_TILES, LANES), F32), pltpu.SemaphoreType.DMA((2,))])
    return pl.pallas_call(
        functools.partial(_moe_combine_kernel, t=t),
        grid_spec=grid_spec,
        out_shape=jax.ShapeDtypeStruct((t, D_MODEL), F32),
        compiler_params=_cparams("arbitrary"),
        name="moe_combine",
    )(pos, x2, gates, y)


def _moe(x2, h_rows, idx, gates, wg, wu, wd, tile, tm):
    tile = math.gcd(2 * x2.shape[0], tile)
    pos, items = _route(idx, tile)
    x_rows = _moe_dispatch(h_rows, pos, tm)
    y = _moe_experts(x_rows, items, wg, wu, wd, tile)
    return _moe_combine(x2, gates, y, pos, tm)


def _block_diag_ones(n, group):
    idx = np.arange(n) // group
    return jnp.asarray((idx[:, None] == idx[None, :]).astype(np.float32), dtype=BF16)


def _rearrange_w_in(w):
    offs = np.cumsum([0, 128, 128, 256, GLA_RANK, 256, 256, 256, 256, 256, 512, 512, 512])
    a_q, a_k, a_v, a_lr, a_g, b_q, b_f, b_i, b_g, c_q, c_k, c_v = [
        w[:, offs[n]:offs[n + 1]] for n in range(12)]
    lr = jnp.pad(a_lr, ((0, 0), (0, LANES - GLA_RANK)))
    return jnp.concatenate([c_v, a_q, a_k, a_v, a_g, b_q, b_f, b_i, b_g, c_q, c_k, lr], axis=1).astype(BF16)


def _tile_rows(seq, cap):
    t = cap
    while seq % t:
        t //= 2
    return t


def kernel(x_prompt, x_sample, state_gla, state_hgrn, cache_k_win, cache_v_win, norm_mix_g, w_in,
           gla_w_gate_up, gla_b_gate, gla_norm_g, hgrn_lb_logits, hgrn_norm_g, attn_q_norm_g,
           attn_k_norm_g, w_out, norm_ffn_g, ffn_w_gate, ffn_w_up, ffn_w_down, moe_w_router,
           moe_w_gate, moe_w_up, moe_w_down):
    depth = w_in.shape[0]
    bp, lp, _ = x_prompt.shape
    bs, ls, _ = x_sample.shape
    buf = cache_k_win.shape[2]
    width = ATT_HEADS * HEAD_DIM
    assert lp % (BAND * max(DILATIONS)) == 0 and lp % CHUNK == 0 and ls % CHUNK != 0 and ls == SUBLANES
    assert buf >= BAND * max(DILATIONS) and lp % min(buf, lp) == 0

    lb_w = jax.nn.softmax(hgrn_lb_logits.astype(F32), axis=0)
    lower_bounds = jnp.cumsum(lb_w, axis=0) - lb_w[0]

    ones_att = _block_diag_ones(LANES, HEAD_DIM)
    ones_gla = _block_diag_ones(GLA_HEADS * GLA_DV, GLA_DV)
    ones_hgrn = _block_diag_ones(HGRN_HEADS * HGRN_DV, HGRN_DV)
    tm_p = _tile_rows(bp * lp, 256)
    tm_s = _tile_rows(bs * ls, 256)
    tables_p = _rope_tables(jnp.arange(lp, dtype=jnp.int32))
    tables_s = _rope_tables(jnp.tile(PAST_LEN + jnp.arange(ls, dtype=jnp.int32), tm_s // ls))
    ck = jnp.transpose(cache_k_win, (0, 1, 3, 4, 2))
    cv = jnp.transpose(cache_v_win, (0, 1, 3, 4, 2))

    def sample_heads(a):
        return a.reshape(bs, ls, ATT_HEADS, HEAD_DIM)

    def sample_tail(a):
        a = jnp.transpose(sample_heads(a), (0, 2, 3, 1))
        return jnp.pad(a, ((0, 0), (0, 0), (0, 0), (LANES - ls, 0)))

    xp = x_prompt.reshape(bp * lp, D_MODEL)
    xs = x_sample.reshape(bs * ls, D_MODEL)
    zero_gla = jnp.zeros((bp, GLA_HEADS, GLA_DK, GLA_DV), F32)
    zero_hgrn = jnp.zeros((bp, HGRN_HEADS, HGRN_DK, HGRN_DV), F32)
    gla_p, hgrn_p, kw_p, vw_p, gla_s, hgrn_s = [], [], [], [], [], []
    caches_s = None
    keep = min(buf, lp)

    for layer in range(depth):
        w_proj = _rearrange_w_in(w_in[layer])
        g_mix = norm_mix_g[layer].reshape(1, D_MODEL)
        wup = jnp.pad(gla_w_gate_up[layer], ((0, LANES - GLA_RANK), (0, 0))).astype(BF16)
        bias = gla_b_gate[layer].reshape(1, -1)
        gain_a = jnp.tile(gla_norm_g[layer], GLA_HEADS).reshape(1, -1)
        gain_b = jnp.tile(hgrn_norm_g[layer], HGRN_HEADS).reshape(1, -1)
        lb = lower_bounds[layer].reshape(1, -1)
        gq = jnp.tile(attn_q_norm_g[layer], 2).reshape(1, LANES)
        gk = jnp.tile(attn_k_norm_g[layer], 2).reshape(1, LANES)
        wout = w_out[layer].astype(BF16)
        g_ffn = norm_ffn_g[layer].reshape(1, D_MODEL)

        mixed = []
        for which, x, batch, seq, tm in (("p", xp, bp, lp, tm_p), ("s", xs, bs, ls, tm_s)):
            proj = _inproj(x, g_mix, w_proj, tm)
            if which == "p":
                qn, kn, k_win, v_win = _qkrope_window(proj, gq, gk, tables_p, ones_att, batch, seq, keep)
                s0_a, s0_b, block, chunk, seqs = zero_gla, zero_hgrn, min(seq, 4 * CHUNK), CHUNK, 1
            else:
                qn, kn = _qkrope(proj, gq, gk, tables_s, ones_att, tm, 1)
                s0_a, s0_b, block, chunk, seqs = state_gla[layer], state_hgrn[layer], seq, seq, math.gcd(batch, SAMPLE_SEQS)
            o_a, s_a = _gated_linear("gla", proj, batch, seq, s0_a, gain_a, ones_gla, (wup, bias), block, chunk, seqs)
            o_b, s_b = _gated_linear("hgrn", proj, batch, seq, s0_b, gain_b, ones_hgrn, (lb,), block, chunk, seqs)
            if which == "p":
                o_c = _attn_prompt(qn, kn, proj, COL_CV, batch, seq)
                gla_p.append(s_a)
                hgrn_p.append(s_b)
                kw_p.append(k_win)
                vw_p.append(v_win)
            else:
                o_c, ck_new, cv_new = _attn_sample(
                    jnp.transpose(sample_heads(qn), (0, 2, 1, 3)), sample_tail(kn),
                    sample_tail(proj[:, COL_CV:COL_CV + width]), ck, cv, layer, caches_s, seq, 2)
                o_c = jnp.transpose(o_c, (0, 2, 1, 3)).reshape(batch * seq, width)
                caches_s = (ck_new, cv_new)
                gla_s.append(s_a)
                hgrn_s.append(s_b)
            mixed.append((o_a, o_b, o_c))

        j = layer // 2
        new_x = []
        for (o_a, o_b, o_c), x, tm in ((mixed[0], xp, tm_p), (mixed[1], xs, tm_s)):
            if layer % 2 == 0:
                new_x.append(_mix_ffn(x, o_a, o_b, o_c, wout, g_ffn, ffn_w_gate[j].astype(BF16),
                                      ffn_w_up[j].astype(BF16), ffn_w_down[j].astype(BF16), tm))
            else:
                wr = jnp.pad(moe_w_router[j], ((0, 0), (0, LANES - N_EXPERTS)))
                wr_hi = wr.astype(BF16)
                wr = jnp.concatenate([wr_hi, (wr - wr_hi.astype(F32)).astype(BF16)], axis=1)
                x2, h_rows, idx, gates = _mix_router(x, o_a, o_b, o_c, wout, g_ffn, wr, tm)
                new_x.append(_moe(x2, h_rows, idx, gates, moe_w_gate[j].astype(BF16), moe_w_up[j].astype(BF16),
                                  moe_w_down[j].astype(BF16), MOE_TILE, tm))
        xp, xs = new_x

    return (xp.reshape(bp, lp, D_MODEL), xs.reshape(bs, ls, D_MODEL),
            jnp.stack(gla_p), jnp.stack(hgrn_p),
            jnp.transpose(jnp.stack(kw_p), (0, 1, 4, 2, 3)), jnp.transpose(jnp.stack(vw_p), (0, 1, 4, 2, 3)),
            jnp.stack(gla_s), jnp.stack(hgrn_s),
            jnp.transpose(caches_s[0], (0, 1, 4, 2, 3)), jnp.transpose(caches_s[1], (0, 1, 4, 2, 3)))
```

```python
import functools
import math

import numpy as np
import jax
import jax.numpy as jnp
from jax import lax
from jax.experimental import pallas as pl
from jax.experimental.pallas import tpu as pltpu

F32 = jnp.float32
BF16 = jnp.bfloat16

D_MODEL = 1024
HEAD_DIM = 64
GLA_HEADS = 4
GLA_DK = 32
GLA_DV = 64
GLA_RANK = 16
GLA_TAU = 16.0
HGRN_HEADS = 4
HGRN_DK = 64
HGRN_DV = 64
ATT_HEADS = 8
DILATIONS = (1, 4, 16)
BAND = 128
ROT_DIM = 16
ROPE_THETA = 500000.0
CHUNK = 64
N_EXPERTS = 8
EPS = 1e-6
PAST_LEN = 8192
NEG = -1e30

LANES = 128
SUBLANES = 8
VMEM_LIMIT = 56 * 1024 * 1024
ROW_TILES = D_MODEL // LANES
SAMPLE_SEQS = 16
MOE_TILE = 256

COL_CV = 0
COL_AQ, COL_AK, COL_AV, COL_AG = 512, 640, 768, 1024
COL_BQ, COL_BF, COL_BI, COL_BG = 1280, 1536, 1792, 2048
COL_CQ, COL_CK = 2304, 2816
COL_LR = 3328
N_PROJ = 3456


def _cparams(*sem):
    return pltpu.CompilerParams(dimension_semantics=sem, vmem_limit_bytes=VMEM_LIMIT)


def _sigmoid(x):
    return 1.0 / (1.0 + jnp.exp(-x))


def _silu(x):
    return x * _sigmoid(x)


def _log_sigmoid(x):
    return -(jnp.maximum(-x, 0.0) + jnp.log(1.0 + jnp.exp(-jnp.abs(x))))


def _dot(a, b):
    return jnp.dot(a, b, preferred_element_type=F32)


def _dot_nt(a, b):
    return lax.dot_general(a, b, (((1,), (1,)), ((), ())), preferred_element_type=F32)


def _dot_tn(a, b):
    return lax.dot_general(a, b, (((0,), (0,)), ((), ())), preferred_element_type=F32)


def _group_sum(x, ones_bd):
    hi = x.astype(BF16)
    lo = (x - hi.astype(F32)).astype(BF16)
    return _dot(hi, ones_bd) + _dot(lo, ones_bd)


def _inproj_kernel(x_ref, g_ref, w_ref, o_ref):
    x = x_ref[...]
    ms = jnp.mean(x * x, axis=-1, keepdims=True)
    h = (x * lax.rsqrt(ms + EPS)) * g_ref[...]
    o_ref[...] = _dot(h.astype(BF16), w_ref[...])


def _inproj(x, g, w, tm):
    t = x.shape[0]
    return pl.pallas_call(
        _inproj_kernel,
        grid=(t // tm,),
        in_specs=[pl.BlockSpec((tm, D_MODEL), lambda i: (i, 0)),
                  pl.BlockSpec((1, D_MODEL), lambda i: (0, 0)),
                  pl.BlockSpec((D_MODEL, N_PROJ), lambda i: (0, 0))],
        out_specs=pl.BlockSpec((tm, N_PROJ), lambda i: (i, 0)),
        out_shape=jax.ShapeDtypeStruct((t, N_PROJ), F32),
        compiler_params=_cparams("parallel"),
        name="inproj",
    )(x, g, w)


def _qkrope_kernel(q_ref, k_ref, *rest, window):
    if window:
        v_ref, gq_ref, gk_ref, cos_ref, sa_ref, sb_ref, ones_ref, qo_ref, ko_ref, kt_ref, vt_ref = rest
    else:
        gq_ref, gk_ref, cos_ref, sa_ref, sb_ref, ones_ref, qo_ref, ko_ref = rest
    cos, sa, sb = cos_ref[...], sa_ref[...], sb_ref[...]
    ones_bd = ones_ref[...]

    def norm_rope(x, g):
        ms = _group_sum(x * x, ones_bd) * (1.0 / HEAD_DIM)
        y = (x * lax.rsqrt(ms + EPS)) * g
        up = pltpu.roll(y, LANES - ROT_DIM // 2, axis=1)
        dn = pltpu.roll(y, ROT_DIM // 2, axis=1)
        return y * cos + up * sa + dn * sb

    qo_ref[...] = norm_rope(q_ref[...], gq_ref[...]) * (HEAD_DIM ** -0.5)
    k = norm_rope(k_ref[...], gk_ref[...])
    ko_ref[...] = k
    if window:
        @pl.when(pl.program_id(0) == pl.num_programs(0) - 1)
        def _():
            kt_ref[0] = k.T.reshape(kt_ref.shape[1:])
            vt_ref[0] = v_ref[...].T.reshape(vt_ref.shape[1:])


def _rope_tables(pos):
    half = ROT_DIM // 2
    inv_freq = ROPE_THETA ** (-jnp.arange(0, ROT_DIM, 2, dtype=F32) / ROT_DIM)
    ang = pos.astype(F32)[:, None] * inv_freq[None, :]
    cos, sin = jnp.cos(ang), jnp.sin(ang)
    n = pos.shape[0]
    rest = HEAD_DIM - ROT_DIM
    c = jnp.concatenate([cos, cos, jnp.ones((n, rest), F32)], axis=1)
    sa = jnp.concatenate([-sin, jnp.zeros((n, half + rest), F32)], axis=1)
    sb = jnp.concatenate([jnp.zeros((n, half), F32), sin, jnp.zeros((n, rest), F32)], axis=1)
    return tuple(jnp.tile(a, (1, 2)) for a in (c, sa, sb))


def _qkrope(proj, gq, gk, tables, ones_bd, tm, table_blocks):
    t = proj.shape[0]
    cq, ck = COL_CQ // LANES, COL_CK // LANES
    tab_spec = pl.BlockSpec((tm, LANES), lambda i, j: (i % table_blocks, 0))
    row_spec = pl.BlockSpec((1, LANES), lambda i, j: (0, 0))
    out_spec = pl.BlockSpec((tm, LANES), lambda i, j: (i, j))
    out_sds = jax.ShapeDtypeStruct((t, ATT_HEADS * HEAD_DIM), F32)
    return pl.pallas_call(
        functools.partial(_qkrope_kernel, window=False),
        grid=(t // tm, ATT_HEADS // 2),
        in_specs=[pl.BlockSpec((tm, LANES), lambda i, j: (i, cq + j)),
                  pl.BlockSpec((tm, LANES), lambda i, j: (i, ck + j)),
                  row_spec, row_spec, tab_spec, tab_spec, tab_spec,
                  pl.BlockSpec((LANES, LANES), lambda i, j: (0, 0))],
        out_specs=[out_spec, out_spec],
        out_shape=[out_sds, out_sds],
        compiler_params=_cparams("parallel", "parallel"),
        name="qkrope",
    )(proj, proj, gq, gk, *tables, ones_bd)


def _qkrope_window(proj, gq, gk, tables, ones_bd, batch, seq, window):
    cq, ck, cv = COL_CQ // LANES, COL_CK // LANES, COL_CV // LANES
    nt = seq // window

    def tok(col):
        return pl.BlockSpec((window, LANES), lambda s, b, j: (b * nt + s, col + j))

    def win_index(s, b, j):
        last = (s == nt - 1).astype(jnp.int32)
        return (b * last, j * last, 0, 0)

    tab_spec = pl.BlockSpec((window, LANES), lambda s, b, j: (s, 0))
    row_spec = pl.BlockSpec((1, LANES), lambda s, b, j: (0, 0))
    win_spec = pl.BlockSpec((1, 2, HEAD_DIM, window), win_index)
    out_sds = jax.ShapeDtypeStruct((batch * seq, ATT_HEADS * HEAD_DIM), F32)
    win_sds = jax.ShapeDtypeStruct((batch, ATT_HEADS, HEAD_DIM, window), F32)
    return pl.pallas_call(
        functools.partial(_qkrope_kernel, window=True),
        grid=(nt, batch, ATT_HEADS // 2),
        in_specs=[tok(cq), tok(ck), tok(cv), row_spec, row_spec, tab_spec, tab_spec, tab_spec,
                  pl.BlockSpec((LANES, LANES), lambda s, b, j: (0, 0))],
        out_specs=[tok(0), tok(0), win_spec, win_spec],
        out_shape=[out_sds, out_sds, win_sds, win_sds],
        compiler_params=_cparams("arbitrary", "arbitrary", "arbitrary"),
        name="qkrope_window",
    )(proj, proj, proj, gq, gk, *tables, ones_bd)


def _segment_scans(g, chunk):
    rows, hk = g.shape
    n_chunks = rows // chunk
    row = lax.broadcasted_iota(jnp.int32, g.shape, 0)

    def at(x, shift):
        return pltpu.roll(x, shift % rows, axis=0)

    zero = jnp.zeros_like(g)
    c = {1: g, 2: g + jnp.where(row % 2 == 1, at(g, 1), zero)}
    c[4] = c[2] + jnp.where(row % 4 == 2, at(c[2], 1), zero) + jnp.where(row % 4 == 3, at(c[2], 2), zero)
    d = {1: zero, 2: jnp.where(row % 2 == 0, at(g, -1), zero)}
    c4_end = jnp.where(row % 4 == 0, at(c[4], -3),
                       jnp.where(row % 4 == 1, at(c[4], -2),
                                 jnp.where(row % 4 == 2, at(c[4], -1), c[4])))
    d[4] = c4_end - c[4]
    tiles = chunk // SUBLANES
    c4_t = c[4].reshape(n_chunks * tiles, SUBLANES, hk)
    row_t = row.reshape(n_chunks * tiles, SUBLANES, hk)
    c8_t = c4_t + jnp.where(row_t % 8 >= 4, c4_t[:, 3:4, :], jnp.zeros_like(c4_t))
    ends = c8_t[:, SUBLANES - 1:SUBLANES, :]
    prefix = []
    for ci in range(n_chunks):
        acc = [jnp.zeros((1, 1, hk), F32)]
        for j in range(tiles):
            acc.append(acc[-1] + ends[ci * tiles + j:ci * tiles + j + 1])
        prefix.append(acc)

    def per_tile(pick):
        return jnp.concatenate([prefix[ci][pick(j)] for ci in range(n_chunks) for j in range(tiles)], axis=0)

    b_t = c8_t + per_tile(lambda j: j)
    width = 1
    while width * SUBLANES < chunk:
        m = width * SUBLANES
        c[m] = (b_t - per_tile(lambda j: (j // width) * width)).reshape(rows, hk)
        d[m] = (per_tile(lambda j: (j // width + 1) * width) - b_t).reshape(rows, hk)
        width *= 2
    b_rest = (per_tile(lambda j: tiles) - b_t).reshape(rows, hk)
    b_last = [prefix[ci][tiles].reshape(1, hk) for ci in range(n_chunks)]
    return c, d, b_t.reshape(rows, hk), b_rest, b_last


def _pair_masks(chunk, copies):
    t = lax.broadcasted_iota(jnp.int32, (copies * chunk, chunk), 0) % chunk
    s = lax.broadcasted_iota(jnp.int32, (copies * chunk, chunk), 1)
    masks = {0: t == s}
    m = 1
    while m < chunk:
        masks[m] = (t // (2 * m) == s // (2 * m)) & (t % (2 * m) >= m) & (s % (2 * m) < m)
        m *= 2
    return masks


def _gl_block(q, k, g, v, st, heads, dk, dv, chunk):
    rows = q.shape[0]
    hk, hv = heads * dk, heads * dv
    stacked_masks = _pair_masks(chunk, heads)
    levels = sorted(stacked_masks)
    c, d, b, b_rest, b_last = _segment_scans(g, chunk)
    k_bf = k.astype(BF16)
    q_lvl = {0: q.astype(BF16)}
    k_lvl = {0: k_bf, 1: k_bf}
    for m in levels[1:]:
        q_lvl[m] = (q * jnp.exp(c[m])).astype(BF16)
        if m > 1:
            k_lvl[m] = (k * jnp.exp(d[m])).astype(BF16)
    q_in = (q * jnp.exp(b)).astype(BF16)
    k_out = (k * jnp.exp(b_rest)).astype(BF16)
    v_bf = v.astype(BF16)
    lane_k = lax.broadcasted_iota(jnp.int32, (chunk, hk), 1) // dk
    lane_v = lax.broadcasted_iota(jnp.int32, (chunk, hv), 1) // dv
    same_head = (lax.broadcasted_iota(jnp.int32, (hv, hk), 0) // dv
                 == lax.broadcasted_iota(jnp.int32, (hv, hk), 1) // dk)
    zero_k = jnp.zeros((chunk, hk), BF16)
    zero_v = jnp.zeros((chunk, hv), BF16)
    o_intra, st_inc = [], []
    for ci in range(rows // chunk):
        r = slice(ci * chunk, (ci + 1) * chunk)
        a = jnp.zeros((heads * chunk, chunk), F32)
        for m in levels:
            q_heads = jnp.concatenate([jnp.where(lane_k == h, q_lvl[m][r], zero_k) for h in range(heads)], axis=0)
            a = jnp.where(stacked_masks[m], _dot_nt(q_heads, k_lvl[m][r]), a)
        a = a.astype(BF16)
        o = None
        for h in range(heads):
            part = _dot(a[h * chunk:(h + 1) * chunk], jnp.where(lane_v == h, v_bf[r], zero_v))
            o = part if o is None else o + part
        o_intra.append(o)
        st_inc.append(jnp.where(same_head, _dot_tn(v_bf[r], k_out[r]), 0.0))
    independent = isinstance(st, list)
    outs, finals = [], []
    for ci in range(rows // chunk):
        r = slice(ci * chunk, (ci + 1) * chunk)
        cur = st[ci] if independent else st
        outs.append(o_intra[ci] + _dot_nt(q_in[r], cur.astype(BF16)))
        nxt = cur * jnp.exp(b_last[ci]) + st_inc[ci]
        if independent:
            finals.append(nxt)
        else:
            st = nxt
    return (outs[0] if len(outs) == 1 else jnp.concatenate(outs, axis=0)), (finals if independent else st)


def _head_norm_gate(o, gain, gate, ones_bd, dv):
    ms = _group_sum(o * o, ones_bd) * (1.0 / dv)
    return ((o * lax.rsqrt(ms + EPS)) * gain) * _silu(gate)


def _gl_kernel(*refs, mode, heads, dk, dv, chunk):
    if mode == "gla":
        (q_ref, k_ref, v_ref, gate_ref, lr_ref, wup_ref, bias_ref, gain_ref, s0_ref, ones_ref,
         o_ref, sT_ref, st_ref) = refs
        q = q_ref[...] * (dk ** -0.5)
        k = k_ref[...]
        g = _log_sigmoid(_dot(lr_ref[...].astype(BF16), wup_ref[...]) + bias_ref[...]) * (1.0 / GLA_TAU)
    else:
        (q_ref, f_ref, v_ref, gate_ref, lb_ref, gain_ref, s0_ref, ones_ref,
         o_ref, sT_ref, st_ref) = refs
        lb = lb_ref[...]
        q = _silu(q_ref[...])
        z = f_ref[...]
        g = jnp.log(lb + (1.0 - lb) * _sigmoid(z))
        k = (1.0 - lb) * _sigmoid(-z)
    seqs = s0_ref.shape[0]
    if seqs > 1:
        o, finals = _gl_block(q, k, g, v_ref[...], [s0_ref[n] for n in range(seqs)], heads, dk, dv, chunk)
        for n in range(seqs):
            sT_ref[n] = finals[n]
    else:
        i = pl.program_id(1)

        @pl.when(i == 0)
        def _():
            st_ref[...] = s0_ref[0]

        o, st = _gl_block(q, k, g, v_ref[...], st_ref[...], heads, dk, dv, chunk)
        st_ref[...] = st

        @pl.when(i == pl.num_programs(1) - 1)
        def _():
            sT_ref[0] = st
    o_ref[...] = _head_norm_gate(o, gain_ref[...], gate_ref[...], ones_ref[...], dv)


def _embed_state(s):
    b, h, k, v = s.shape
    eye = jnp.eye(h, dtype=s.dtype)
    return jnp.einsum("bhkv,hg->bhvgk", s, eye).reshape(b, h * v, h * k)


def _extract_state(st, h, k, v):
    b = st.shape[0]
    st = st.reshape(b, h, v, h, k)
    return jnp.stack([st[:, i, :, i, :] for i in range(h)], axis=1).transpose(0, 1, 3, 2)


def _gated_linear(mode, proj, batch, seq, s0, gain, ones_bd, extra, block_tokens, chunk, seqs=1):
    assert seqs == 1 or (block_tokens == seq == chunk and batch % seqs == 0)
    heads, dk, dv = (GLA_HEADS, GLA_DK, GLA_DV) if mode == "gla" else (HGRN_HEADS, HGRN_DK, HGRN_DV)
    hk, hv = heads * dk, heads * dv
    nb = seq // block_tokens
    tb = block_tokens * seqs

    def cols(width, col):
        return pl.BlockSpec((tb, width), lambda b, i: (b * nb + i, col // width))

    def const(shape):
        return pl.BlockSpec(shape, lambda b, i: (0,) * len(shape))

    state_spec = pl.BlockSpec((seqs, hv, hk), lambda b, i: (b, 0, 0))
    st0 = jnp.zeros((batch, hv, hk), F32) if s0 is None else _embed_state(s0)
    if mode == "gla":
        wup, bias = extra
        in_specs = [cols(hk, COL_AQ), cols(hk, COL_AK), cols(hv, COL_AV), cols(hv, COL_AG), cols(LANES, COL_LR),
                    const((LANES, hk)), const((1, hk)), const((1, hv)), state_spec, const((hv, hv))]
        args = (proj, proj, proj, proj, proj, wup, bias, gain, st0, ones_bd)
    else:
        (lb,) = extra
        in_specs = [cols(hk, COL_BQ), cols(hk, COL_BF), cols(hv, COL_BI), cols(hv, COL_BG),
                    const((1, hk)), const((1, hv)), state_spec, const((hv, hv))]
        args = (proj, proj, proj, proj, lb, gain, st0, ones_bd)
    o, st = pl.pallas_call(
        functools.partial(_gl_kernel, mode=mode, heads=heads, dk=dk, dv=dv, chunk=chunk),
        grid=(batch // seqs, nb),
        in_specs=in_specs,
        out_specs=[pl.BlockSpec((tb, hv), lambda b, i: (b * nb + i, 0)), state_spec],
        out_shape=[jax.ShapeDtypeStruct((batch * seq, hv), F32),
                   jax.ShapeDtypeStruct((batch, hv, hk), F32)],
        scratch_shapes=[pltpu.VMEM((hv, hk), F32)],
        compiler_params=_cparams("parallel", "arbitrary"),
        name=mode + "_scan",
    )(*args)
    return o, _extract_state(st, heads, dk, dv)


def _attn_prompt_kernel(q_ref, k_ref, v_ref, o_ref, qr_ref, kt_ref, kr_ref, vt_ref, vr_ref,
                        m_ref, l_ref, acc_ref, *, seq):
    n_res = max(DILATIONS)
    per_res = seq // n_res
    head0 = lax.broadcasted_iota(jnp.int32, (per_res, LANES), 1) < HEAD_DIM
    for r in range(n_res):
        tok = slice(r * per_res, (r + 1) * per_res)
        strided = pl.ds(r, per_res, stride=n_res)
        k_t, v_t = k_ref[tok, :], v_ref[tok, :]
        k_r, v_r = k_ref[strided, :], v_ref[strided, :]
        qr_ref[tok, :] = q_ref[strided, :]
        for h in range(2):
            mine = head0 if h == 0 else ~head0
            kt_ref[h, tok, :] = jnp.where(mine, k_t, 0.0).astype(BF16)
            kr_ref[h, tok, :] = jnp.where(mine, k_r, 0.0).astype(BF16)
            vt_ref[h, tok, :] = jnp.where(mine, v_t, 1.0).astype(BF16)
            vr_ref[h, tok, :] = jnp.where(mine, v_r, 1.0).astype(BF16)
    m_ref[...] = jnp.full(m_ref.shape, NEG, F32)
    l_ref[...] = jnp.zeros(l_ref.shape, F32)
    acc_ref[...] = jnp.zeros(acc_ref.shape, F32)

    head0 = lax.broadcasted_iota(jnp.int32, (BAND, LANES), 1) < HEAD_DIM
    qi = lax.broadcasted_iota(jnp.int32, (BAND, 2 * BAND), 0)
    ki = lax.broadcasted_iota(jnp.int32, (BAND, 2 * BAND), 1)

    for dil in DILATIONS:
        n_runs = n_res // dil
        run = SUBLANES * dil
        keys_by_token = dil == 1

        def seq_index(u, n_runs=n_runs, run=run):
            return n_runs * (u % run) + u // run

        kj = ki if keys_by_token else seq_index(ki % BAND) + BAND * (ki // BAND)
        rel = BAND + seq_index(qi) - kj
        in_band = (rel >= 0) & (rel <= BAND)
        bias_prev = jnp.where(in_band, 0.0, NEG).astype(F32)
        bias_first = jnp.where(in_band & (ki >= BAND), 0.0, NEG).astype(F32)

        def body(idx, carry, dil=dil, n_runs=n_runs, run=run, keys_by_token=keys_by_token,
                 bias_prev=bias_prev, bias_first=bias_first):
            res = idx % dil
            blk = idx // dil
            starts = [pl.multiple_of((res + dil * c) * per_res + run * blk, run) for c in range(n_runs)]
            prevs = [pl.multiple_of((res + dil * c) * per_res + run * jnp.maximum(blk - 1, 0), run)
                     for c in range(n_runs)]

            def gather(ref, offs, *lead):
                parts = [ref[lead + (pl.ds(o, run), slice(None))] for o in offs]
                return parts[0] if len(parts) == 1 else jnp.concatenate(parts, axis=0)

            def scatter(ref, offs, val, *lead):
                for c, o in enumerate(offs):
                    ref[lead + (pl.ds(o, run), slice(None))] = val[c * run:(c + 1) * run, :]

            q = gather(qr_ref, starts).astype(BF16)
            bias = jnp.where(blk > 0, bias_prev, bias_first)
            alphas, pvs = [], []
            for h in range(2):
                if keys_by_token:
                    cur = pl.multiple_of(blk * BAND, BAND)
                    prev = pl.multiple_of(jnp.maximum(blk - 1, 0) * BAND, BAND)
                    k2 = jnp.concatenate([kt_ref[h, pl.ds(prev, BAND), :], kt_ref[h, pl.ds(cur, BAND), :]], axis=0)
                    v2 = jnp.concatenate([vt_ref[h, pl.ds(prev, BAND), :], vt_ref[h, pl.ds(cur, BAND), :]], axis=0)
                else:
                    k2 = jnp.concatenate([gather(kr_ref, prevs, h), gather(kr_ref, starts, h)], axis=0)
                    v2 = jnp.concatenate([gather(vr_ref, prevs, h), gather(vr_ref, starts, h)], axis=0)
                s = _dot_nt(q, k2) + bias
                m_prev = gather(m_ref, starts, h)
                m_new = jnp.maximum(m_prev, jnp.max(s, axis=-1, keepdims=True))
                scatter(m_ref, starts, m_new, h)
                p = jnp.exp(s - jnp.concatenate([m_new, m_new], axis=1))
                alphas.append(jnp.exp(m_prev - m_new))
                pvs.append(_dot(p.astype(BF16), v2))
            scatter(l_ref, starts, jnp.where(head0, alphas[1], alphas[0]) * gather(l_ref, starts)
                    + jnp.where(head0, pvs[1], pvs[0]))
            scatter(acc_ref, starts, jnp.where(head0, alphas[0], alphas[1]) * gather(acc_ref, starts)
                    + jnp.where(head0, pvs[0], pvs[1]))
            return carry

        lax.fori_loop(0, seq // BAND, body, 0, unroll=8)

    for r in range(n_res):
        tok = slice(r * per_res, (r + 1) * per_res)
        o_ref[pl.ds(r, per_res, stride=n_res), :] = acc_ref[tok, :] / pltpu.roll(l_ref[tok, :], HEAD_DIM, axis=1)


def _attn_prompt(q, k, v_src, v_col, batch, seq):
    spec = pl.BlockSpec((seq, LANES), lambda b, j: (b, j))
    vc = v_col // LANES
    return pl.pallas_call(
        functools.partial(_attn_prompt_kernel, seq=seq),
        grid=(batch, ATT_HEADS // 2),
        in_specs=[spec, spec, pl.BlockSpec((seq, LANES), lambda b, j: (b, vc + j))],
        out_specs=spec,
        out_shape=jax.ShapeDtypeStruct((batch * seq, ATT_HEADS * HEAD_DIM), F32),
        scratch_shapes=[pltpu.VMEM((seq, LANES), F32),
                        pltpu.VMEM((2, seq, LANES), BF16), pltpu.VMEM((2, seq, LANES), BF16),
                        pltpu.VMEM((2, seq, LANES), BF16), pltpu.VMEM((2, seq, LANES), BF16),
                        pltpu.VMEM((2, seq, LANES), F32), pltpu.VMEM((seq, LANES), F32),
                        pltpu.VMEM((seq, LANES), F32)],
        compiler_params=_cparams("parallel", "parallel"),
        name="attn_prompt",
    )(q, k, v_src)


def _sample_counts(steps, buf):
    t = np.arange(steps)[:, None]

    def count(dist, live):
        cnt = np.zeros(dist.shape, np.float32)
        for dil in DILATIONS:
            cnt += (live & (dist >= 0) & (dist % dil == 0) & (dist <= BAND * dil)).astype(np.float32)
        return cnt

    lane = np.arange(LANES)[None, :]
    return (count(buf + t - np.arange(buf)[None, :], True),
            count(t - (lane - (LANES - steps)), lane >= LANES - steps))


def _attn_sample_kernel(q_ref, kn_ref, vn_ref, kc_ref, vc_ref, cc_ref, cn_ref, *rest, steps, aliased):
    if aliased:
        rest = rest[2:]
    o_ref, ko_ref, vo_ref = rest
    q = q_ref[0].astype(BF16)
    k_old, v_old = kc_ref[0, 0], vc_ref[0, 0]
    k_new, v_new = kn_ref[0], vn_ref[0]
    cc, cn = cc_ref[...][None], cn_ref[...][None]

    def scores(keys):
        return lax.dot_general(q, keys.astype(BF16), (((2,), (1,)), ((0,), (0,))), preferred_element_type=F32)

    def weighted(p, values):
        return lax.dot_general(p.astype(BF16), values.astype(BF16), (((2,), (2,)), ((0,), (0,))),
                               preferred_element_type=F32)

    s_old = jnp.where(cc > 0.0, scores(k_old), NEG)
    s_new = jnp.where(cn > 0.0, scores(k_new), NEG)
    m = jnp.maximum(jnp.max(s_old, axis=-1, keepdims=True), jnp.max(s_new, axis=-1, keepdims=True))
    p_old = cc * jnp.exp(s_old - m)
    p_new = cn * jnp.exp(s_new - m)
    den = jnp.sum(p_old, axis=-1, keepdims=True) + jnp.sum(p_new, axis=-1, keepdims=True)
    o_ref[0] = (weighted(p_old, v_old) + weighted(p_new, v_new)) / den

    buf = k_old.shape[-1]
    lane = lax.broadcasted_iota(jnp.int32, k_new.shape, 2)
    for old, new, out_ref in ((k_old, k_new, ko_ref), (v_old, v_new, vo_ref)):
        shifted = pltpu.roll(old, buf - steps, axis=2)
        out_ref[0, 0, :, :, 0:buf - LANES] = shifted[:, :, 0:buf - LANES]
        out_ref[0, 0, :, :, buf - LANES:buf] = jnp.where(lane >= LANES - steps, new, shifted[:, :, buf - LANES:buf])


def _attn_sample(q, k_new, v_new, cache_k, cache_v, layer, prev_out, steps, head_split):
    depth, batch, heads, hd, buf = cache_k.shape
    hb = heads // head_split
    cc, cn = (jnp.asarray(c) for c in _sample_counts(steps, buf))
    q_spec = pl.BlockSpec((1, hb, steps, hd), lambda b, j: (b, j, 0, 0))
    new_spec = pl.BlockSpec((1, hb, hd, LANES), lambda b, j: (b, j, 0, 0))
    cache_spec = pl.BlockSpec((1, 1, hb, hd, buf), lambda b, j: (layer, b, j, 0, 0))
    in_specs = [q_spec, new_spec, new_spec, cache_spec, cache_spec,
                pl.BlockSpec((steps, buf), lambda b, j: (0, 0)), pl.BlockSpec((steps, LANES), lambda b, j: (0, 0))]
    args = [q, k_new, v_new, cache_k, cache_v, cc, cn]
    aliases = {}
    if prev_out is not None:
        any_spec = pl.BlockSpec(memory_space=pl.ANY)
        in_specs += [any_spec, any_spec]
        args += list(prev_out)
        aliases = {7: 1, 8: 2}
    cache_sds = jax.ShapeDtypeStruct(cache_k.shape, F32)
    return pl.pallas_call(
        functools.partial(_attn_sample_kernel, steps=steps, aliased=prev_out is not None),
        grid=(batch, head_split),
        in_specs=in_specs,
        out_specs=[q_spec, cache_spec, cache_spec],
        out_shape=[jax.ShapeDtypeStruct(q.shape, F32), cache_sds, cache_sds],
        input_output_aliases=aliases,
        compiler_params=_cparams("arbitrary", "arbitrary"),
        name="attn_sample",
    )(*args)


def _outproj(oa_ref, ob_ref, oc_ref, wout_ref, x_ref):
    na, nb = oa_ref.shape[1], ob_ref.shape[1]
    mixed = (_dot(oa_ref[...].astype(BF16), wout_ref[0:na, :])
             + _dot(ob_ref[...].astype(BF16), wout_ref[na:na + nb, :])
             + _dot(oc_ref[...].astype(BF16), wout_ref[na + nb:, :]))
    return x_ref[...] + mixed


def _rmsnorm(x, g):
    return (x * lax.rsqrt(jnp.mean(x * x, axis=-1, keepdims=True) + EPS)) * g


def _mix_ffn_kernel(x_ref, oa_ref, ob_ref, oc_ref, wout_ref, g_ref, wg_ref, wu_ref, wd_ref, o_ref):
    x2 = _outproj(oa_ref, ob_ref, oc_ref, wout_ref, x_ref)
    h = _rmsnorm(x2, g_ref[...]).astype(BF16)
    act = _silu(_dot(h, wg_ref[...])) * _dot(h, wu_ref[...])
    o_ref[...] = x2 + _dot(act.astype(BF16), wd_ref[...])


def _mix_ffn(x, oa, ob, oc, wout, g, wg, wu, wd, tm):
    t = x.shape[0]
    d_ff = wg.shape[1]

    def rows(width):
        return pl.BlockSpec((tm, width), lambda i: (i, 0))

    def const(shape):
        return pl.BlockSpec(shape, lambda i: (0, 0), pipeline_mode=pl.Buffered(1))

    return pl.pallas_call(
        _mix_ffn_kernel,
        grid=(t // tm,),
        in_specs=[rows(D_MODEL), rows(oa.shape[1]), rows(ob.shape[1]), rows(oc.shape[1]),
                  const((D_MODEL, D_MODEL)), const((1, D_MODEL)),
                  const((D_MODEL, d_ff)), const((D_MODEL, d_ff)), const((d_ff, D_MODEL))],
        out_specs=rows(D_MODEL),
        out_shape=jax.ShapeDtypeStruct((t, D_MODEL), F32),
        compiler_params=_cparams("parallel"),
        name="mix_ffn",
    )(x, oa, ob, oc, wout, g, wg, wu, wd)


def _mix_router_kernel(x_ref, oa_ref, ob_ref, oc_ref, wout_ref, g_ref, wr_ref, x2_ref, h_ref, idx_ref, gate_ref):
    tm = x_ref.shape[0]
    x2 = _outproj(oa_ref, ob_ref, oc_ref, wout_ref, x_ref)
    x2_ref[...] = x2
    h = _rmsnorm(x2, g_ref[...])
    for s in range(D_MODEL // LANES):
        h_ref[pl.ds(s, tm, stride=D_MODEL // LANES), :] = h[:, s * LANES:(s + 1) * LANES]
    h_hi = h.astype(BF16)
    h_lo = (h - h_hi.astype(F32)).astype(BF16)
    by_hi = _dot(h_hi, wr_ref[...])
    logits = by_hi[:, 0:LANES] + by_hi[:, LANES:2 * LANES] + _dot(h_lo, wr_ref[:, 0:LANES])
    lane = lax.broadcasted_iota(jnp.int32, logits.shape, 1)
    logits = jnp.where(lane < N_EXPERTS, logits, -jnp.inf)
    v1 = jnp.max(logits, axis=-1, keepdims=True)
    i1 = jnp.min(jnp.where(logits == v1, lane, LANES), axis=-1, keepdims=True)
    rest = jnp.where(lane == i1, -jnp.inf, logits)
    v2 = jnp.max(rest, axis=-1, keepdims=True)
    i2 = jnp.min(jnp.where(rest == v2, lane, LANES), axis=-1, keepdims=True)
    e2 = jnp.exp(v2 - v1)
    idx_ref[...] = jnp.where(lane == 0, i1, jnp.where(lane == 1, i2, 0))
    gate_ref[...] = jnp.where(lane == 0, 1.0 / (1.0 + e2), jnp.where(lane == 1, e2 / (1.0 + e2), 0.0))


def _mix_router(x, oa, ob, oc, wout, g, wr, tm):
    t = x.shape[0]

    def rows(width):
        return pl.BlockSpec((tm, width), lambda i: (i, 0))

    def const(shape):
        return pl.BlockSpec(shape, lambda i: (0, 0))

    return pl.pallas_call(
        _mix_router_kernel,
        grid=(t // tm,),
        in_specs=[rows(D_MODEL), rows(oa.shape[1]), rows(ob.shape[1]), rows(oc.shape[1]),
                  const((D_MODEL, D_MODEL)), const((1, D_MODEL)), const((D_MODEL, 2 * LANES))],
        out_specs=[rows(D_MODEL), pl.BlockSpec((tm * ROW_TILES, LANES), lambda i: (i, 0)), rows(LANES), rows(LANES)],
        out_shape=[jax.ShapeDtypeStruct((t, D_MODEL), F32), jax.ShapeDtypeStruct((t * ROW_TILES, LANES), F32),
                   jax.ShapeDtypeStruct((t, LANES), jnp.int32), jax.ShapeDtypeStruct((t, LANES), F32)],
        compiler_params=_cparams("parallel"),
        name="mix_router",
    )(x, oa, ob, oc, wout, g, wr)


def _route(idx, tile):
    t = idx.shape[0]
    n_tiles = (2 * t) // tile
    n_items = n_tiles + N_EXPERTS - 1
    expert = jnp.concatenate([idx[:, 0], idx[:, 1]])
    onehot = (expert[:, None] == jnp.arange(N_EXPERTS, dtype=jnp.int32)[None, :]).astype(jnp.int32)
    csum = jnp.cumsum(onehot, axis=0)
    rank = jnp.take_along_axis(csum, expert[:, None], axis=1)[:, 0] - 1
    group_end = jnp.cumsum(csum[-1])
    group_start = group_end - csum[-1]
    pos = (group_start[expert] + rank).astype(jnp.int32)
    tile_lo = jnp.arange(n_tiles, dtype=jnp.int32)[:, None] * tile
    present = (group_start[None, :] < tile_lo + tile) & (group_end[None, :] > tile_lo)
    seen = jnp.cumsum(present.reshape(-1).astype(jnp.int32))
    k = jnp.arange(n_items, dtype=jnp.int32)
    flat = jnp.sum((seen[None, :] <= jnp.minimum(k, seen[-1] - 1)[:, None]).astype(jnp.int32), axis=1)
    item_tile, item_expert = flat // N_EXPERTS, flat % N_EXPERTS
    lo = jnp.clip(group_start[item_expert] - item_tile * tile, 0, tile)
    hi = jnp.where(k < seen[-1], jnp.clip(group_end[item_expert] - item_tile * tile, 0, tile), lo)
    first = jnp.concatenate([jnp.ones((1,), jnp.int32), (item_tile[1:] != item_tile[:-1]).astype(jnp.int32)])
    return pos, tuple(a.astype(jnp.int32) for a in (item_tile, item_expert, lo, hi, first))


def _moe_dispatch_kernel(pos_ref, h_ref, x_hbm, buf, sem, *, t):
    tm = h_ref.shape[0] // ROW_TILES
    i = pl.program_id(0)
    n = pl.num_programs(0)
    slot = i % 2

    def row_copy(step, sl, j, r):
        a = j * t + step * tm + r
        return pltpu.make_async_copy(buf.at[sl, pl.ds(pl.multiple_of(r * ROW_TILES, ROW_TILES), ROW_TILES), :],
                                     x_hbm.at[pl.ds(pl.multiple_of(pos_ref[a] * ROW_TILES, ROW_TILES), ROW_TILES), :],
                                     sem.at[sl])

    def each_row(fn):
        def body(r, carry):
            fn(0, r)
            fn(1, r)
            return carry
        lax.fori_loop(0, tm, body, 0, unroll=4)

    @pl.when(i >= 2)
    def _():
        each_row(lambda j, r: row_copy(i - 2, slot, j, r).wait())

    buf[slot] = h_ref[...]
    each_row(lambda j, r: row_copy(i, slot, j, r).start())

    @pl.when(i == n - 1)
    def _():
        @pl.when(n >= 2)
        def _():
            each_row(lambda j, r: row_copy(i - 1, 1 - slot, j, r).wait())
        each_row(lambda j, r: row_copy(i, slot, j, r).wait())


def _moe_dispatch(h_rows, pos, tm):
    a_n = pos.shape[0]
    t = a_n // 2
    grid_spec = pltpu.PrefetchScalarGridSpec(
        num_scalar_prefetch=1, grid=(t // tm,),
        in_specs=[pl.BlockSpec((tm * ROW_TILES, LANES), lambda i, p: (i, 0))],
        out_specs=pl.BlockSpec(memory_space=pl.ANY),
        scratch_shapes=[pltpu.VMEM((2, tm * ROW_TILES, LANES), F32), pltpu.SemaphoreType.DMA((2,))])
    return pl.pallas_call(
        functools.partial(_moe_dispatch_kernel, t=t),
        grid_spec=grid_spec,
        out_shape=jax.ShapeDtypeStruct((a_n * ROW_TILES, LANES), F32),
        compiler_params=_cparams("arbitrary"),
        name="moe_dispatch",
    )(pos, h_rows)


def _moe_expert_kernel(it_ref, ie_ref, lo_ref, hi_ref, first_ref, x_ref, wg_ref, wu_ref, wd_ref, y_ref, *, tile):
    del it_ref, ie_ref
    k = pl.program_id(0)
    x = jnp.concatenate([x_ref[pl.ds(s, tile, stride=ROW_TILES), :] for s in range(ROW_TILES)], axis=1)
    x = x.astype(BF16)
    act = _silu(_dot(x, wg_ref[0])) * _dot(x, wu_ref[0])
    y = _dot(act.astype(BF16), wd_ref[0])
    row = lax.broadcasted_iota(jnp.int32, (tile, LANES), 0)
    mine = (row >= lo_ref[k]) & (row < hi_ref[k])

    @pl.when(first_ref[k] == 1)
    def _():
        for s in range(ROW_TILES):
            y_ref[pl.ds(s, tile, stride=ROW_TILES), :] = jnp.where(mine, y[:, s * LANES:(s + 1) * LANES], 0.0)

    @pl.when(first_ref[k] != 1)
    def _():
        for s in range(ROW_TILES):
            rows = pl.ds(s, tile, stride=ROW_TILES)
            y_ref[rows, :] = jnp.where(mine, y[:, s * LANES:(s + 1) * LANES], y_ref[rows, :])


def _moe_experts(x_rows, items, wg, wu, wd, tile):
    n_items = items[0].shape[0]
    _, _, d_ff = wg.shape
    rows_spec = pl.BlockSpec((tile * ROW_TILES, LANES), lambda k, it, ie, lo, hi, fi: (it[k], 0))

    def w_spec(shape):
        return pl.BlockSpec((1,) + shape, lambda k, it, ie, lo, hi, fi: (ie[k], 0, 0))

    grid_spec = pltpu.PrefetchScalarGridSpec(
        num_scalar_prefetch=5, grid=(n_items,),
        in_specs=[rows_spec, w_spec((D_MODEL, d_ff)), w_spec((D_MODEL, d_ff)), w_spec((d_ff, D_MODEL))],
        out_specs=rows_spec)
    return pl.pallas_call(
        functools.partial(_moe_expert_kernel, tile=tile),
        grid_spec=grid_spec,
        out_shape=jax.ShapeDtypeStruct(x_rows.shape, F32),
        compiler_params=_cparams("arbitrary"),
        name="moe_experts",
    )(*items, x_rows, wg, wu, wd)


def _moe_combine_kernel(pos_ref, x2_ref, gate_ref, y_hbm, o_ref, ybuf, sem, *, t):
    tm = x2_ref.shape[0]
    i = pl.program_id(0)
    n = pl.num_programs(0)
    slot = i % 2

    def row_copy(step, sl, j, r):
        a = j * t + step * tm + r
        return pltpu.make_async_copy(y_hbm.at[pl.ds(pl.multiple_of(pos_ref[a] * ROW_TILES, ROW_TILES), ROW_TILES), :],
                                     ybuf.at[sl, j, pl.ds(pl.multiple_of(r * ROW_TILES, ROW_TILES), ROW_TILES), :],
                                     sem.at[sl])

    def each_row(fn):
        def body(r, carry):
            fn(0, r)
            fn(1, r)
            return carry
        lax.fori_loop(0, tm, body, 0, unroll=4)

    @pl.when(i == 0)
    def _():
        each_row(lambda j, r: row_copy(0, 0, j, r).start())

    @pl.when(i + 1 < n)
    def _():
        each_row(lambda j, r: row_copy(i + 1, 1 - slot, j, r).start())

    each_row(lambda j, r: row_copy(i, slot, j, r).wait())
    gates = gate_ref[...]
    g0, g1 = gates[:, 0:1], gates[:, 1:2]
    for s in range(ROW_TILES):
        lanes = slice(s * LANES, (s + 1) * LANES)
        rows = pl.ds(s, tm, stride=ROW_TILES)
        o_ref[:, lanes] = x2_ref[:, lanes] + g0 * ybuf[slot, 0, rows, :] + g1 * ybuf[slot, 1, rows, :]


def _moe_combine(x2, gates, y, pos, tm):
    t = x2.shape[0]
    grid_spec = pltpu.PrefetchScalarGridSpec(
        num_scalar_prefetch=1, grid=(t // tm,),
        in_specs=[pl.BlockSpec((tm, D_MODEL), lambda i, p: (i, 0)), pl.BlockSpec((tm, LANES), lambda i, p: (i, 0)),
                  pl.BlockSpec(memory_space=pl.ANY)],
        out_specs=pl.BlockSpec((tm, D_MODEL), lambda i, p: (i, 0)),
        scratch_shapes=[pltpu.VMEM((2, 2, tm * ROW_TILES, LANES), F32), pltpu.SemaphoreType.DMA((2,))])
    return pl.pallas_call(
        functools.partial(_moe_combine_kernel, t=t),
        grid_spec=grid_spec,
        out_shape=jax.ShapeDtypeStruct((t, D_MODEL), F32),
        compiler_params=_cparams("arbitrary"),
        name="moe_combine",
    )(pos, x2, gates, y)


def _moe(x2, h_rows, idx, gates, wg, wu, wd, tile, tm):
    tile = math.gcd(2 * x2.shape[0], tile)
    pos, items = _route(idx, tile)
    x_rows = _moe_dispatch(h_rows, pos, tm)
    y = _moe_experts(x_rows, items, wg, wu, wd, tile)
    return _moe_combine(x2, gates, y, pos, tm)


def _block_diag_ones(n, group):
    idx = np.arange(n) // group
    return jnp.asarray((idx[:, None] == idx[None, :]).astype(np.float32), dtype=BF16)


def _rearrange_w_in(w):
    offs = np.cumsum([0, 128, 128, 256, GLA_RANK, 256, 256, 256, 256, 256, 512, 512, 512])
    a_q, a_k, a_v, a_lr, a_g, b_q, b_f, b_i, b_g, c_q, c_k, c_v = [
        w[:, offs[n]:offs[n + 1]] for n in range(12)]
    lr = jnp.pad(a_lr, ((0, 0), (0, LANES - GLA_RANK)))
    return jnp.concatenate([c_v, a_q, a_k, a_v, a_g, b_q, b_f, b_i, b_g, c_q, c_k, lr], axis=1).astype(BF16)


def _tile_rows(seq, cap):
    t = cap
    while seq % t:
        t //= 2
    return t


def kernel(x_prompt, x_sample, state_gla, state_hgrn, cache_k_win, cache_v_win, norm_mix_g, w_in,
           gla_w_gate_up, gla_b_gate, gla_norm_g, hgrn_lb_logits, hgrn_norm_g, attn_q_norm_g,
           attn_k_norm_g, w_out, norm_ffn_g, ffn_w_gate, ffn_w_up, ffn_w_down, moe_w_router,
           moe_w_gate, moe_w_up, moe_w_down):
    depth = w_in.shape[0]
    bp, lp, _ = x_prompt.shape
    bs, ls, _ = x_sample.shape
    buf = cache_k_win.shape[2]
    width = ATT_HEADS * HEAD_DIM
    assert lp % (BAND * max(DILATIONS)) == 0 and lp % CHUNK == 0 and ls % CHUNK != 0 and ls == SUBLANES
    assert buf >= BAND * max(DILATIONS) and lp % min(buf, lp) == 0

    lb_w = jax.nn.softmax(hgrn_lb_logits.astype(F32), axis=0)
    lower_bounds = jnp.cumsum(lb_w, axis=0) - lb_w[0]

    ones_att = _block_diag_ones(LANES, HEAD_DIM)
    ones_gla = _block_diag_ones(GLA_HEADS * GLA_DV, GLA_DV)
    ones_hgrn = _block_diag_ones(HGRN_HEADS * HGRN_DV, HGRN_DV)
    tm_p = _tile_rows(bp * lp, 256)
    tm_s = _tile_rows(bs * ls, 256)
    tables_p = _rope_tables(jnp.arange(lp, dtype=jnp.int32))
    tables_s = _rope_tables(jnp.tile(PAST_LEN + jnp.arange(ls, dtype=jnp.int32), tm_s // ls))
    ck = jnp.transpose(cache_k_win, (0, 1, 3, 4, 2))
    cv = jnp.transpose(cache_v_win, (0, 1, 3, 4, 2))

    def sample_heads(a):
        return a.reshape(bs, ls, ATT_HEADS, HEAD_DIM)

    def sample_tail(a):
        a = jnp.transpose(sample_heads(a), (0, 2, 3, 1))
        return jnp.pad(a, ((0, 0), (0, 0), (0, 0), (LANES - ls, 0)))

    xp = x_prompt.reshape(bp * lp, D_MODEL)
    xs = x_sample.reshape(bs * ls, D_MODEL)
    gla_p, hgrn_p, kw_p, vw_p, gla_s, hgrn_s = [], [], [], [], [], []
    caches_s = None
    keep = min(buf, lp)

    for layer in range(depth):
        w_proj = _rearrange_w_in(w_in[layer])
        g_mix = norm_mix_g[layer].reshape(1, D_MODEL)
        wup = jnp.pad(gla_w_gate_up[layer], ((0, LANES - GLA_RANK), (0, 0))).astype(BF16)
        bias = gla_b_gate[layer].reshape(1, -1)
        gain_a = jnp.tile(gla_norm_g[layer], GLA_HEADS).reshape(1, -1)
        gain_b = jnp.tile(hgrn_norm_g[layer], HGRN_HEADS).reshape(1, -1)
        lb = lower_bounds[layer].reshape(1, -1)
        gq = jnp.tile(attn_q_norm_g[layer], 2).reshape(1, LANES)
        gk = jnp.tile(attn_k_norm_g[layer], 2).reshape(1, LANES)
        wout = w_out[layer].astype(BF16)
        g_ffn = norm_ffn_g[layer].reshape(1, D_MODEL)

        mixed = []
        for which, x, batch, seq, tm in (("p", xp, bp, lp, tm_p), ("s", xs, bs, ls, tm_s)):
            proj = _inproj(x, g_mix, w_proj, _tile_rows(x.shape[0], 2 * tm))
            if which == "p":
                qn, kn, k_win, v_win = _qkrope_window(proj, gq, gk, tables_p, ones_att, batch, seq, keep)
                s0_a, s0_b, block, chunk, seqs = None, None, min(seq, 4 * CHUNK), CHUNK, 1
            else:
                qn, kn = _qkrope(proj, gq, gk, tables_s, ones_att, tm, 1)
                s0_a, s0_b, block, chunk, seqs = state_gla[layer], state_hgrn[layer], seq, seq, math.gcd(batch, SAMPLE_SEQS)
            o_a, s_a = _gated_linear("gla", proj, batch, seq, s0_a, gain_a, ones_gla, (wup, bias), block, chunk, seqs)
            o_b, s_b = _gated_linear("hgrn", proj, batch, seq, s0_b, gain_b, ones_hgrn, (lb,), block, chunk, seqs)
            if which == "p":
                o_c = _attn_prompt(qn, kn, proj, COL_CV, batch, seq)
                gla_p.append(s_a)
                hgrn_p.append(s_b)
                kw_p.append(k_win)
                vw_p.append(v_win)
            else:
                o_c, ck_new, cv_new = _attn_sample(
                    jnp.transpose(sample_heads(qn), (0, 2, 1, 3)), sample_tail(kn),
                    sample_tail(proj[:, COL_CV:COL_CV + width]), ck, cv, layer, caches_s, seq, 2)
                o_c = jnp.transpose(o_c, (0, 2, 1, 3)).reshape(batch * seq, width)
                caches_s = (ck_new, cv_new)
                gla_s.append(s_a)
                hgrn_s.append(s_b)
            mixed.append((o_a, o_b, o_c))

        j = layer // 2
        new_x = []
        for (o_a, o_b, o_c), x, tm in ((mixed[0], xp, tm_p), (mixed[1], xs, tm_s)):
            if layer % 2 == 0:
                new_x.append(_mix_ffn(x, o_a, o_b, o_c, wout, g_ffn, ffn_w_gate[j].astype(BF16),
                                      ffn_w_up[j].astype(BF16), ffn_w_down[j].astype(BF16),
                                      _tile_rows(x.shape[0], 2 * tm)))
            else:
                wr = jnp.pad(moe_w_router[j], ((0, 0), (0, LANES - N_EXPERTS)))
                wr_hi = wr.astype(BF16)
                wr = jnp.concatenate([wr_hi, (wr - wr_hi.astype(F32)).astype(BF16)], axis=1)
                x2, h_rows, idx, gates = _mix_router(x, o_a, o_b, o_c, wout, g_ffn, wr, tm)
                new_x.append(_moe(x2, h_rows, idx, gates, moe_w_gate[j].astype(BF16), moe_w_up[j].astype(BF16),
                                  moe_w_down[j].astype(BF16), MOE_TILE, tm))
        xp, xs = new_x

    return (xp.reshape(bp, lp, D_MODEL), xs.reshape(bs, ls, D_MODEL),
            jnp.stack(gla_p), jnp.stack(hgrn_p),
            jnp.transpose(jnp.stack(kw_p), (0, 1, 4, 2, 3)), jnp.transpose(jnp.stack(vw_p), (0, 1, 4, 2, 3)),
            jnp.stack(gla_s), jnp.stack(hgrn_s),
            jnp.transpose(caches_s[0], (0, 1, 4, 2, 3)), jnp.transpose(caches_s[1], (0, 1, 4, 2, 3)))
```

```python
import functools
import math

import numpy as np
import jax
import jax.numpy as jnp
from jax import lax
from jax.experimental import pallas as pl
from jax.experimental.pallas import tpu as pltpu

F32 = jnp.float32
BF16 = jnp.bfloat16

D_MODEL = 1024
HEAD_DIM = 64
GLA_HEADS = 4
GLA_DK = 32
GLA_DV = 64
GLA_RANK = 16
GLA_TAU = 16.0
HGRN_HEADS = 4
HGRN_DK = 64
HGRN_DV = 64
ATT_HEADS = 8
DILATIONS = (1, 4, 16)
BAND = 128
ROT_DIM = 16
ROPE_THETA = 500000.0
CHUNK = 64
N_EXPERTS = 8
EPS = 1e-6
PAST_LEN = 8192
NEG = -1e30

LANES = 128
SUBLANES = 8
VMEM_LIMIT = 56 * 1024 * 1024
ROW_TILES = D_MODEL // LANES
SAMPLE_SEQS = 16
MOE_TILE = 256

COL_CV = 0
COL_AQ, COL_AK, COL_AV, COL_AG = 512, 640, 768, 1024
COL_BQ, COL_BF, COL_BI, COL_BG = 1280, 1536, 1792, 2048
COL_CQ, COL_CK = 2304, 2816
COL_LR = 3328
N_PROJ = 3456


def _cparams(*sem):
    return pltpu.CompilerParams(dimension_semantics=sem, vmem_limit_bytes=VMEM_LIMIT)


def _sigmoid(x):
    return 1.0 / (1.0 + jnp.exp(-x))


def _silu(x):
    return x * _sigmoid(x)


def _log_sigmoid(x):
    return -(jnp.maximum(-x, 0.0) + jnp.log(1.0 + jnp.exp(-jnp.abs(x))))


def _dot(a, b):
    return jnp.dot(a, b, preferred_element_type=F32)


def _dot_nt(a, b):
    return lax.dot_general(a, b, (((1,), (1,)), ((), ())), preferred_element_type=F32)


def _dot_tn(a, b):
    return lax.dot_general(a, b, (((0,), (0,)), ((), ())), preferred_element_type=F32)


def _group_sum(x, ones_bd):
    hi = x.astype(BF16)
    lo = (x - hi.astype(F32)).astype(BF16)
    return _dot(hi, ones_bd) + _dot(lo, ones_bd)


def _inproj_kernel(x_ref, g_ref, w_ref, o_ref):
    x = x_ref[...]
    ms = jnp.mean(x * x, axis=-1, keepdims=True)
    h = (x * lax.rsqrt(ms + EPS)) * g_ref[...]
    o_ref[...] = _dot(h.astype(BF16), w_ref[...])


def _inproj(x, g, w, tm):
    t = x.shape[0]
    return pl.pallas_call(
        _inproj_kernel,
        grid=(t // tm,),
        in_specs=[pl.BlockSpec((tm, D_MODEL), lambda i: (i, 0)),
                  pl.BlockSpec((1, D_MODEL), lambda i: (0, 0)),
                  pl.BlockSpec((D_MODEL, N_PROJ), lambda i: (0, 0))],
        out_specs=pl.BlockSpec((tm, N_PROJ), lambda i: (i, 0)),
        out_shape=jax.ShapeDtypeStruct((t, N_PROJ), F32),
        compiler_params=_cparams("parallel"),
        name="inproj",
    )(x, g, w)


def _qkrope_kernel(q_ref, k_ref, *rest, window):
    if window:
        v_ref, gq_ref, gk_ref, cos_ref, sa_ref, sb_ref, ones_ref, qo_ref, ko_ref, kt_ref, vt_ref = rest
    else:
        gq_ref, gk_ref, cos_ref, sa_ref, sb_ref, ones_ref, qo_ref, ko_ref = rest
    cos, sa, sb = cos_ref[...], sa_ref[...], sb_ref[...]
    ones_bd = ones_ref[...]

    def norm_rope(x, g):
        ms = _group_sum(x * x, ones_bd) * (1.0 / HEAD_DIM)
        y = (x * lax.rsqrt(ms + EPS)) * g
        up = pltpu.roll(y, LANES - ROT_DIM // 2, axis=1)
        dn = pltpu.roll(y, ROT_DIM // 2, axis=1)
        return y * cos + up * sa + dn * sb

    qo_ref[...] = norm_rope(q_ref[...], gq_ref[...]) * (HEAD_DIM ** -0.5)
    k = norm_rope(k_ref[...], gk_ref[...])
    ko_ref[...] = k
    if window:
        @pl.when(pl.program_id(0) == pl.num_programs(0) - 1)
        def _():
            kt_ref[0] = k.T.reshape(kt_ref.shape[1:])
            vt_ref[0] = v_ref[...].T.reshape(vt_ref.shape[1:])


def _rope_tables(pos):
    half = ROT_DIM // 2
    inv_freq = ROPE_THETA ** (-jnp.arange(0, ROT_DIM, 2, dtype=F32) / ROT_DIM)
    ang = pos.astype(F32)[:, None] * inv_freq[None, :]
    cos, sin = jnp.cos(ang), jnp.sin(ang)
    n = pos.shape[0]
    rest = HEAD_DIM - ROT_DIM
    c = jnp.concatenate([cos, cos, jnp.ones((n, rest), F32)], axis=1)
    sa = jnp.concatenate([-sin, jnp.zeros((n, half + rest), F32)], axis=1)
    sb = jnp.concatenate([jnp.zeros((n, half), F32), sin, jnp.zeros((n, rest), F32)], axis=1)
    return tuple(jnp.tile(a, (1, 2)) for a in (c, sa, sb))


def _qkrope(proj, gq, gk, tables, ones_bd, tm, table_blocks):
    t = proj.shape[0]
    cq, ck = COL_CQ // LANES, COL_CK // LANES
    tab_spec = pl.BlockSpec((tm, LANES), lambda i, j: (i % table_blocks, 0))
    row_spec = pl.BlockSpec((1, LANES), lambda i, j: (0, 0))
    out_spec = pl.BlockSpec((tm, LANES), lambda i, j: (i, j))
    out_sds = jax.ShapeDtypeStruct((t, ATT_HEADS * HEAD_DIM), F32)
    return pl.pallas_call(
        functools.partial(_qkrope_kernel, window=False),
        grid=(t // tm, ATT_HEADS // 2),
        in_specs=[pl.BlockSpec((tm, LANES), lambda i, j: (i, cq + j)),
                  pl.BlockSpec((tm, LANES), lambda i, j: (i, ck + j)),
                  row_spec, row_spec, tab_spec, tab_spec, tab_spec,
                  pl.BlockSpec((LANES, LANES), lambda i, j: (0, 0))],
        out_specs=[out_spec, out_spec],
        out_shape=[out_sds, out_sds],
        compiler_params=_cparams("parallel", "parallel"),
        name="qkrope",
    )(proj, proj, gq, gk, *tables, ones_bd)


def _qkrope_window(proj, gq, gk, tables, ones_bd, batch, seq, window):
    cq, ck, cv = COL_CQ // LANES, COL_CK // LANES, COL_CV // LANES
    nt = seq // window

    def tok(col):
        return pl.BlockSpec((window, LANES), lambda s, b, j: (b * nt + s, col + j))

    def win_index(s, b, j):
        last = (s == nt - 1).astype(jnp.int32)
        return (b * last, j * last, 0, 0)

    tab_spec = pl.BlockSpec((window, LANES), lambda s, b, j: (s, 0))
    row_spec = pl.BlockSpec((1, LANES), lambda s, b, j: (0, 0))
    win_spec = pl.BlockSpec((1, 2, HEAD_DIM, window), win_index)
    out_sds = jax.ShapeDtypeStruct((batch * seq, ATT_HEADS * HEAD_DIM), F32)
    win_sds = jax.ShapeDtypeStruct((batch, ATT_HEADS, HEAD_DIM, window), F32)
    return pl.pallas_call(
        functools.partial(_qkrope_kernel, window=True),
        grid=(nt, batch, ATT_HEADS // 2),
        in_specs=[tok(cq), tok(ck), tok(cv), row_spec, row_spec, tab_spec, tab_spec, tab_spec,
                  pl.BlockSpec((LANES, LANES), lambda s, b, j: (0, 0))],
        out_specs=[tok(0), tok(0), win_spec, win_spec],
        out_shape=[out_sds, out_sds, win_sds, win_sds],
        compiler_params=_cparams("arbitrary", "arbitrary", "arbitrary"),
        name="qkrope_window",
    )(proj, proj, proj, gq, gk, *tables, ones_bd)


def _segment_scans(g, chunk):
    rows, hk = g.shape
    n_chunks = rows // chunk
    row = lax.broadcasted_iota(jnp.int32, g.shape, 0)

    def at(x, shift):
        return pltpu.roll(x, shift % rows, axis=0)

    zero = jnp.zeros_like(g)
    c = {1: g, 2: g + jnp.where(row % 2 == 1, at(g, 1), zero)}
    c[4] = c[2] + jnp.where(row % 4 == 2, at(c[2], 1), zero) + jnp.where(row % 4 == 3, at(c[2], 2), zero)
    d = {1: zero, 2: jnp.where(row % 2 == 0, at(g, -1), zero)}
    c4_end = jnp.where(row % 4 == 0, at(c[4], -3),
                       jnp.where(row % 4 == 1, at(c[4], -2),
                                 jnp.where(row % 4 == 2, at(c[4], -1), c[4])))
    d[4] = c4_end - c[4]
    tiles = chunk // SUBLANES
    c4_t = c[4].reshape(n_chunks * tiles, SUBLANES, hk)
    row_t = row.reshape(n_chunks * tiles, SUBLANES, hk)
    c8_t = c4_t + jnp.where(row_t % 8 >= 4, c4_t[:, 3:4, :], jnp.zeros_like(c4_t))
    ends = c8_t[:, SUBLANES - 1:SUBLANES, :]
    prefix = []
    for ci in range(n_chunks):
        acc = [jnp.zeros((1, 1, hk), F32)]
        for j in range(tiles):
            acc.append(acc[-1] + ends[ci * tiles + j:ci * tiles + j + 1])
        prefix.append(acc)

    def per_tile(pick):
        return jnp.concatenate([prefix[ci][pick(j)] for ci in range(n_chunks) for j in range(tiles)], axis=0)

    b_t = c8_t + per_tile(lambda j: j)
    width = 1
    while width * SUBLANES < chunk:
        m = width * SUBLANES
        c[m] = (b_t - per_tile(lambda j: (j // width) * width)).reshape(rows, hk)
        d[m] = (per_tile(lambda j: (j // width + 1) * width) - b_t).reshape(rows, hk)
        width *= 2
    b_rest = (per_tile(lambda j: tiles) - b_t).reshape(rows, hk)
    b_last = [prefix[ci][tiles].reshape(1, hk) for ci in range(n_chunks)]
    return c, d, b_t.reshape(rows, hk), b_rest, b_last


def _pair_masks(chunk, copies):
    t = lax.broadcasted_iota(jnp.int32, (copies * chunk, chunk), 0) % chunk
    s = lax.broadcasted_iota(jnp.int32, (copies * chunk, chunk), 1)
    masks = {0: t == s}
    m = 1
    while m < chunk:
        masks[m] = (t // (2 * m) == s // (2 * m)) & (t % (2 * m) >= m) & (s % (2 * m) < m)
        m *= 2
    return masks


def _gl_block(q, k, g, v, st, heads, dk, dv, chunk):
    rows = q.shape[0]
    hk, hv = heads * dk, heads * dv
    stacked_masks = _pair_masks(chunk, heads)
    levels = sorted(stacked_masks)
    c, d, b, b_rest, b_last = _segment_scans(g, chunk)
    k_bf = k.astype(BF16)
    q_lvl = {0: q.astype(BF16)}
    k_lvl = {0: k_bf, 1: k_bf}
    for m in levels[1:]:
        q_lvl[m] = (q * jnp.exp(c[m])).astype(BF16)
        if m > 1:
            k_lvl[m] = (k * jnp.exp(d[m])).astype(BF16)
    q_in = (q * jnp.exp(b)).astype(BF16)
    k_out = (k * jnp.exp(b_rest)).astype(BF16)
    v_bf = v.astype(BF16)
    lane_k = lax.broadcasted_iota(jnp.int32, (chunk, hk), 1) // dk
    lane_v = lax.broadcasted_iota(jnp.int32, (chunk, hv), 1) // dv
    same_head = (lax.broadcasted_iota(jnp.int32, (hv, hk), 0) // dv
                 == lax.broadcasted_iota(jnp.int32, (hv, hk), 1) // dk)
    zero_k = jnp.zeros((chunk, hk), BF16)
    zero_v = jnp.zeros((chunk, hv), BF16)
    o_intra, st_inc = [], []
    for ci in range(rows // chunk):
        r = slice(ci * chunk, (ci + 1) * chunk)
        a = jnp.zeros((heads * chunk, chunk), F32)
        for m in levels:
            q_heads = jnp.concatenate([jnp.where(lane_k == h, q_lvl[m][r], zero_k) for h in range(heads)], axis=0)
            a = jnp.where(stacked_masks[m], _dot_nt(q_heads, k_lvl[m][r]), a)
        a = a.astype(BF16)
        o = None
        for h in range(heads):
            part = _dot(a[h * chunk:(h + 1) * chunk], jnp.where(lane_v == h, v_bf[r], zero_v))
            o = part if o is None else o + part
        o_intra.append(o)
        st_inc.append(jnp.where(same_head, _dot_tn(v_bf[r], k_out[r]), 0.0))
    independent = isinstance(st, list)
    outs, finals = [], []
    for ci in range(rows // chunk):
        r = slice(ci * chunk, (ci + 1) * chunk)
        cur = st[ci] if independent else st
        outs.append(o_intra[ci] + _dot_nt(q_in[r], cur.astype(BF16)))
        nxt = cur * jnp.exp(b_last[ci]) + st_inc[ci]
        if independent:
            finals.append(nxt)
        else:
            st = nxt
    return (outs[0] if len(outs) == 1 else jnp.concatenate(outs, axis=0)), (finals if independent else st)


def _head_norm_gate(o, gain, gate, ones_bd, dv):
    ms = _group_sum(o * o, ones_bd) * (1.0 / dv)
    return ((o * lax.rsqrt(ms + EPS)) * gain) * _silu(gate)


def _gl_kernel(*refs, mode, heads, dk, dv, chunk, has_init):
    refs = list(refs)
    n_in = 9 if mode == "gla" else 7
    s0_ref = refs.pop(n_in) if has_init else None
    if mode == "gla":
        (q_ref, k_ref, v_ref, gate_ref, lr_ref, wup_ref, bias_ref, gain_ref, ones_ref,
         o_ref, sT_ref, st_ref) = refs
        q = q_ref[...] * (dk ** -0.5)
        k = k_ref[...]
        g = _log_sigmoid(_dot(lr_ref[...].astype(BF16), wup_ref[...]) + bias_ref[...]) * (1.0 / GLA_TAU)
    else:
        (q_ref, f_ref, v_ref, gate_ref, lb_ref, gain_ref, ones_ref,
         o_ref, sT_ref, st_ref) = refs
        lb = lb_ref[...]
        q = _silu(q_ref[...])
        z = f_ref[...]
        g = jnp.log(lb + (1.0 - lb) * _sigmoid(z))
        k = (1.0 - lb) * _sigmoid(-z)
    hk, hv = heads * dk, heads * dv
    eye = (lax.broadcasted_iota(jnp.int32, (hv, dv), 0) % dv
           == lax.broadcasted_iota(jnp.int32, (hv, dv), 1)).astype(BF16)
    same_head = (lax.broadcasted_iota(jnp.int32, (hv, hk), 0) // dv
                 == lax.broadcasted_iota(jnp.int32, (hv, hk), 1) // dk)

    def split3(x):
        hi = x.astype(BF16)
        r = x - hi.astype(F32)
        mid = r.astype(BF16)
        return hi, mid, (r - mid.astype(F32)).astype(BF16)

    def state_in(s):
        parts = split3(s.reshape(hk, dv))
        full = _dot_nt(eye, parts[0]) + _dot_nt(eye, parts[1]) + _dot_nt(eye, parts[2])
        return jnp.where(same_head, full, 0.0)

    def state_out(st):
        parts = split3(st)
        return (_dot_tn(parts[0], eye) + _dot_tn(parts[1], eye) + _dot_tn(parts[2], eye)).reshape(heads, dk, dv)

    seqs = sT_ref.shape[0]
    if seqs > 1:
        o, finals = _gl_block(q, k, g, v_ref[...], [state_in(s0_ref[n]) for n in range(seqs)], heads, dk, dv, chunk)
        for n in range(seqs):
            sT_ref[n] = state_out(finals[n])
    else:
        i = pl.program_id(1)

        @pl.when(i == 0)
        def _():
            st_ref[...] = jnp.zeros(st_ref.shape, F32) if s0_ref is None else state_in(s0_ref[0])

        o, st = _gl_block(q, k, g, v_ref[...], st_ref[...], heads, dk, dv, chunk)
        st_ref[...] = st

        @pl.when(i == pl.num_programs(1) - 1)
        def _():
            sT_ref[0] = state_out(st)
    o_ref[...] = _head_norm_gate(o, gain_ref[...], gate_ref[...], ones_ref[...], dv)


def _gated_linear(mode, proj, batch, seq, s0, gain, ones_bd, extra, block_tokens, chunk, seqs=1):
    assert seqs == 1 or (block_tokens == seq == chunk and batch % seqs == 0)
    heads, dk, dv = (GLA_HEADS, GLA_DK, GLA_DV) if mode == "gla" else (HGRN_HEADS, HGRN_DK, HGRN_DV)
    hk, hv = heads * dk, heads * dv
    nb = seq // block_tokens
    tb = block_tokens * seqs

    def cols(width, col):
        return pl.BlockSpec((tb, width), lambda b, i: (b * nb + i, col // width))

    def const(shape):
        return pl.BlockSpec(shape, lambda b, i: (0,) * len(shape))

    state_spec = pl.BlockSpec((seqs, heads, dk, dv), lambda b, i: (b, 0, 0, 0))
    if mode == "gla":
        wup, bias = extra
        in_specs = [cols(hk, COL_AQ), cols(hk, COL_AK), cols(hv, COL_AV), cols(hv, COL_AG), cols(LANES, COL_LR),
                    const((LANES, hk)), const((1, hk)), const((1, hv)), const((hv, hv))]
        args = (proj, proj, proj, proj, proj, wup, bias, gain, ones_bd)
    else:
        (lb,) = extra
        in_specs = [cols(hk, COL_BQ), cols(hk, COL_BF), cols(hv, COL_BI), cols(hv, COL_BG),
                    const((1, hk)), const((1, hv)), const((hv, hv))]
        args = (proj, proj, proj, proj, lb, gain, ones_bd)
    if s0 is not None:
        in_specs, args = in_specs + [state_spec], args + (s0,)
    return pl.pallas_call(
        functools.partial(_gl_kernel, mode=mode, heads=heads, dk=dk, dv=dv, chunk=chunk, has_init=s0 is not None),
        grid=(batch // seqs, nb),
        in_specs=in_specs,
        out_specs=[pl.BlockSpec((tb, hv), lambda b, i: (b * nb + i, 0)), state_spec],
        out_shape=[jax.ShapeDtypeStruct((batch * seq, hv), F32),
                   jax.ShapeDtypeStruct((batch, heads, dk, dv), F32)],
        scratch_shapes=[pltpu.VMEM((hv, hk), F32)],
        compiler_params=_cparams("parallel", "arbitrary"),
        name=mode + "_scan",
    )(*args)


def _attn_prompt_kernel(q_ref, k_ref, v_ref, o_ref, qr_ref, kt_ref, kr_ref, vt_ref, vr_ref,
                        m_ref, l_ref, acc_ref, *, seq):
    n_res = max(DILATIONS)
    per_res = seq // n_res
    head0 = lax.broadcasted_iota(jnp.int32, (per_res, LANES), 1) < HEAD_DIM
    for r in range(n_res):
        tok = slice(r * per_res, (r + 1) * per_res)
        strided = pl.ds(r, per_res, stride=n_res)
        k_t, v_t = k_ref[tok, :], v_ref[tok, :]
        k_r, v_r = k_ref[strided, :], v_ref[strided, :]
        qr_ref[tok, :] = q_ref[strided, :]
        for h in range(2):
            mine = head0 if h == 0 else ~head0
            kt_ref[h, tok, :] = jnp.where(mine, k_t, 0.0).astype(BF16)
            kr_ref[h, tok, :] = jnp.where(mine, k_r, 0.0).astype(BF16)
            vt_ref[h, tok, :] = jnp.where(mine, v_t, 1.0).astype(BF16)
            vr_ref[h, tok, :] = jnp.where(mine, v_r, 1.0).astype(BF16)
    m_ref[...] = jnp.full(m_ref.shape, NEG, F32)
    l_ref[...] = jnp.zeros(l_ref.shape, F32)
    acc_ref[...] = jnp.zeros(acc_ref.shape, F32)

    head0 = lax.broadcasted_iota(jnp.int32, (BAND, LANES), 1) < HEAD_DIM
    qi = lax.broadcasted_iota(jnp.int32, (BAND, 2 * BAND), 0)
    ki = lax.broadcasted_iota(jnp.int32, (BAND, 2 * BAND), 1)

    for dil in DILATIONS:
        n_runs = n_res // dil
        run = SUBLANES * dil
        keys_by_token = dil == 1

        def seq_index(u, n_runs=n_runs, run=run):
            return n_runs * (u % run) + u // run

        kj = ki if keys_by_token else seq_index(ki % BAND) + BAND * (ki // BAND)
        rel = BAND + seq_index(qi) - kj
        in_band = (rel >= 0) & (rel <= BAND)
        bias_prev = jnp.where(in_band, 0.0, NEG).astype(F32)
        bias_first = jnp.where(in_band & (ki >= BAND), 0.0, NEG).astype(F32)

        def body(idx, carry, dil=dil, n_runs=n_runs, run=run, keys_by_token=keys_by_token,
                 bias_prev=bias_prev, bias_first=bias_first):
            res = idx % dil
            blk = idx // dil
            starts = [pl.multiple_of((res + dil * c) * per_res + run * blk, run) for c in range(n_runs)]
            prevs = [pl.multiple_of((res + dil * c) * per_res + run * jnp.maximum(blk - 1, 0), run)
                     for c in range(n_runs)]

            def gather(ref, offs, *lead):
                parts = [ref[lead + (pl.ds(o, run), slice(None))] for o in offs]
                return parts[0] if len(parts) == 1 else jnp.concatenate(parts, axis=0)

            def scatter(ref, offs, val, *lead):
                for c, o in enumerate(offs):
                    ref[lead + (pl.ds(o, run), slice(None))] = val[c * run:(c + 1) * run, :]

            q = gather(qr_ref, starts).astype(BF16)
            bias = jnp.where(blk > 0, bias_prev, bias_first)
            alphas, pvs = [], []
            for h in range(2):
                if keys_by_token:
                    cur = pl.multiple_of(blk * BAND, BAND)
                    prev = pl.multiple_of(jnp.maximum(blk - 1, 0) * BAND, BAND)
                    k2 = jnp.concatenate([kt_ref[h, pl.ds(prev, BAND), :], kt_ref[h, pl.ds(cur, BAND), :]], axis=0)
                    v2 = jnp.concatenate([vt_ref[h, pl.ds(prev, BAND), :], vt_ref[h, pl.ds(cur, BAND), :]], axis=0)
                else:
                    k2 = jnp.concatenate([gather(kr_ref, prevs, h), gather(kr_ref, starts, h)], axis=0)
                    v2 = jnp.concatenate([gather(vr_ref, prevs, h), gather(vr_ref, starts, h)], axis=0)
                s = _dot_nt(q, k2) + bias
                m_prev = gather(m_ref, starts, h)
                m_new = jnp.maximum(m_prev, jnp.max(s, axis=-1, keepdims=True))
                scatter(m_ref, starts, m_new, h)
                p = jnp.exp(s - jnp.concatenate([m_new, m_new], axis=1))
                alphas.append(jnp.exp(m_prev - m_new))
                pvs.append(_dot(p.astype(BF16), v2))
            scatter(l_ref, starts, jnp.where(head0, alphas[1], alphas[0]) * gather(l_ref, starts)
                    + jnp.where(head0, pvs[1], pvs[0]))
            scatter(acc_ref, starts, jnp.where(head0, alphas[0], alphas[1]) * gather(acc_ref, starts)
                    + jnp.where(head0, pvs[0], pvs[1]))
            return carry

        lax.fori_loop(0, seq // BAND, body, 0, unroll=8)

    for r in range(n_res):
        tok = slice(r * per_res, (r + 1) * per_res)
        o_ref[pl.ds(r, per_res, stride=n_res), :] = acc_ref[tok, :] / pltpu.roll(l_ref[tok, :], HEAD_DIM, axis=1)


def _attn_prompt(q, k, v_src, v_col, batch, seq):
    spec = pl.BlockSpec((seq, LANES), lambda b, j: (b, j))
    vc = v_col // LANES
    return pl.pallas_call(
        functools.partial(_attn_prompt_kernel, seq=seq),
        grid=(batch, ATT_HEADS // 2),
        in_specs=[spec, spec, pl.BlockSpec((seq, LANES), lambda b, j: (b, vc + j))],
        out_specs=spec,
        out_shape=jax.ShapeDtypeStruct((batch * seq, ATT_HEADS * HEAD_DIM), F32),
        scratch_shapes=[pltpu.VMEM((seq, LANES), F32),
                        pltpu.VMEM((2, seq, LANES), BF16), pltpu.VMEM((2, seq, LANES), BF16),
                        pltpu.VMEM((2, seq, LANES), BF16), pltpu.VMEM((2, seq, LANES), BF16),
                        pltpu.VMEM((2, seq, LANES), F32), pltpu.VMEM((seq, LANES), F32),
                        pltpu.VMEM((seq, LANES), F32)],
        compiler_params=_cparams("parallel", "parallel"),
        name="attn_prompt",
    )(q, k, v_src)


def _sample_counts(steps, buf):
    t = np.arange(steps)[:, None]

    def count(dist, live):
        cnt = np.zeros(dist.shape, np.float32)
        for dil in DILATIONS:
            cnt += (live & (dist >= 0) & (dist % dil == 0) & (dist <= BAND * dil)).astype(np.float32)
        return cnt

    lane = np.arange(LANES)[None, :]
    return (count(buf + t - np.arange(buf)[None, :], True),
            count(t - (lane - (LANES - steps)), lane >= LANES - steps))


def _attn_sample_kernel(q_ref, kn_ref, vn_ref, kc_ref, vc_ref, cc_ref, cn_ref, *rest, steps, aliased):
    if aliased:
        rest = rest[2:]
    o_ref, ko_ref, vo_ref = rest
    q = q_ref[0].astype(BF16)
    k_old, v_old = kc_ref[0, 0], vc_ref[0, 0]
    k_new, v_new = kn_ref[0], vn_ref[0]
    cc, cn = cc_ref[...][None], cn_ref[...][None]

    def scores(keys):
        return lax.dot_general(q, keys.astype(BF16), (((2,), (1,)), ((0,), (0,))), preferred_element_type=F32)

    def weighted(p, values):
        return lax.dot_general(p.astype(BF16), values.astype(BF16), (((2,), (2,)), ((0,), (0,))),
                               preferred_element_type=F32)

    s_old = jnp.where(cc > 0.0, scores(k_old), NEG)
    s_new = jnp.where(cn > 0.0, scores(k_new), NEG)
    m = jnp.maximum(jnp.max(s_old, axis=-1, keepdims=True), jnp.max(s_new, axis=-1, keepdims=True))
    p_old = cc * jnp.exp(s_old - m)
    p_new = cn * jnp.exp(s_new - m)
    den = jnp.sum(p_old, axis=-1, keepdims=True) + jnp.sum(p_new, axis=-1, keepdims=True)
    o_ref[0] = (weighted(p_old, v_old) + weighted(p_new, v_new)) / den

    buf = k_old.shape[-1]
    lane = lax.broadcasted_iota(jnp.int32, k_new.shape, 2)
    for old, new, out_ref in ((k_old, k_new, ko_ref), (v_old, v_new, vo_ref)):
        shifted = pltpu.roll(old, buf - steps, axis=2)
        out_ref[0, 0, :, :, 0:buf - LANES] = shifted[:, :, 0:buf - LANES]
        out_ref[0, 0, :, :, buf - LANES:buf] = jnp.where(lane >= LANES - steps, new, shifted[:, :, buf - LANES:buf])


def _attn_sample(q, k_new, v_new, cache_k, cache_v, layer, prev_out, steps, head_split):
    depth, batch, heads, hd, buf = cache_k.shape
    hb = heads // head_split
    cc, cn = (jnp.asarray(c) for c in _sample_counts(steps, buf))
    q_spec = pl.BlockSpec((1, hb, steps, hd), lambda b, j: (b, j, 0, 0))
    new_spec = pl.BlockSpec((1, hb, hd, LANES), lambda b, j: (b, j, 0, 0))
    cache_spec = pl.BlockSpec((1, 1, hb, hd, buf), lambda b, j: (layer, b, j, 0, 0))
    in_specs = [q_spec, new_spec, new_spec, cache_spec, cache_spec,
                pl.BlockSpec((steps, buf), lambda b, j: (0, 0)), pl.BlockSpec((steps, LANES), lambda b, j: (0, 0))]
    args = [q, k_new, v_new, cache_k, cache_v, cc, cn]
    aliases = {}
    if prev_out is not None:
        any_spec = pl.BlockSpec(memory_space=pl.ANY)
        in_specs += [any_spec, any_spec]
        args += list(prev_out)
        aliases = {7: 1, 8: 2}
    cache_sds = jax.ShapeDtypeStruct(cache_k.shape, F32)
    return pl.pallas_call(
        functools.partial(_attn_sample_kernel, steps=steps, aliased=prev_out is not None),
        grid=(batch, head_split),
        in_specs=in_specs,
        out_specs=[q_spec, cache_spec, cache_spec],
        out_shape=[jax.ShapeDtypeStruct(q.shape, F32), cache_sds, cache_sds],
        input_output_aliases=aliases,
        compiler_params=_cparams("arbitrary", "arbitrary"),
        name="attn_sample",
    )(*args)


def _outproj(oa_ref, ob_ref, oc_ref, wout_ref, x_ref):
    na, nb = oa_ref.shape[1], ob_ref.shape[1]
    mixed = (_dot(oa_ref[...].astype(BF16), wout_ref[0:na, :])
             + _dot(ob_ref[...].astype(BF16), wout_ref[na:na + nb, :])
             + _dot(oc_ref[...].astype(BF16), wout_ref[na + nb:, :]))
    return x_ref[...] + mixed


def _rmsnorm(x, g):
    return (x * lax.rsqrt(jnp.mean(x * x, axis=-1, keepdims=True) + EPS)) * g


def _mix_ffn_kernel(x_ref, oa_ref, ob_ref, oc_ref, wout_ref, g_ref, wg_ref, wu_ref, wd_ref, o_ref):
    x2 = _outproj(oa_ref, ob_ref, oc_ref, wout_ref, x_ref)
    h = _rmsnorm(x2, g_ref[...]).astype(BF16)
    act = _silu(_dot(h, wg_ref[...])) * _dot(h, wu_ref[...])
    o_ref[...] = x2 + _dot(act.astype(BF16), wd_ref[...])


def _mix_ffn(x, oa, ob, oc, wout, g, wg, wu, wd, tm):
    t = x.shape[0]
    d_ff = wg.shape[1]

    def rows(width):
        return pl.BlockSpec((tm, width), lambda i: (i, 0))

    def const(shape):
        return pl.BlockSpec(shape, lambda i: (0, 0), pipeline_mode=pl.Buffered(1))

    return pl.pallas_call(
        _mix_ffn_kernel,
        grid=(t // tm,),
        in_specs=[rows(D_MODEL), rows(oa.shape[1]), rows(ob.shape[1]), rows(oc.shape[1]),
                  const((D_MODEL, D_MODEL)), const((1, D_MODEL)),
                  const((D_MODEL, d_ff)), const((D_MODEL, d_ff)), const((d_ff, D_MODEL))],
        out_specs=rows(D_MODEL),
        out_shape=jax.ShapeDtypeStruct((t, D_MODEL), F32),
        compiler_params=_cparams("parallel"),
        name="mix_ffn",
    )(x, oa, ob, oc, wout, g, wg, wu, wd)


def _mix_router_kernel(x_ref, oa_ref, ob_ref, oc_ref, wout_ref, g_ref, wr_ref, x2_ref, h_ref, idx_ref, gate_ref):
    tm = x_ref.shape[0]
    x2 = _outproj(oa_ref, ob_ref, oc_ref, wout_ref, x_ref)
    x2_ref[...] = x2
    h = _rmsnorm(x2, g_ref[...])
    for s in range(D_MODEL // LANES):
        h_ref[pl.ds(s, tm, stride=D_MODEL // LANES), :] = h[:, s * LANES:(s + 1) * LANES]
    h_hi = h.astype(BF16)
    h_lo = (h - h_hi.astype(F32)).astype(BF16)
    by_hi = _dot(h_hi, wr_ref[...])
    logits = by_hi[:, 0:LANES] + by_hi[:, LANES:2 * LANES] + _dot(h_lo, wr_ref[:, 0:LANES])
    lane = lax.broadcasted_iota(jnp.int32, logits.shape, 1)
    logits = jnp.where(lane < N_EXPERTS, logits, -jnp.inf)
    v1 = jnp.max(logits, axis=-1, keepdims=True)
    i1 = jnp.min(jnp.where(logits == v1, lane, LANES), axis=-1, keepdims=True)
    rest = jnp.where(lane == i1, -jnp.inf, logits)
    v2 = jnp.max(rest, axis=-1, keepdims=True)
    i2 = jnp.min(jnp.where(rest == v2, lane, LANES), axis=-1, keepdims=True)
    e2 = jnp.exp(v2 - v1)
    idx_ref[...] = jnp.where(lane == 0, i1, jnp.where(lane == 1, i2, 0))
    gate_ref[...] = jnp.where(lane == 0, 1.0 / (1.0 + e2), jnp.where(lane == 1, e2 / (1.0 + e2), 0.0))


def _mix_router(x, oa, ob, oc, wout, g, wr, tm):
    t = x.shape[0]

    def rows(width):
        return pl.BlockSpec((tm, width), lambda i: (i, 0))

    def const(shape):
        return pl.BlockSpec(shape, lambda i: (0, 0))

    return pl.pallas_call(
        _mix_router_kernel,
        grid=(t // tm,),
        in_specs=[rows(D_MODEL), rows(oa.shape[1]), rows(ob.shape[1]), rows(oc.shape[1]),
                  const((D_MODEL, D_MODEL)), const((1, D_MODEL)), const((D_MODEL, 2 * LANES))],
        out_specs=[rows(D_MODEL), pl.BlockSpec((tm * ROW_TILES, LANES), lambda i: (i, 0)), rows(LANES), rows(LANES)],
        out_shape=[jax.ShapeDtypeStruct((t, D_MODEL), F32), jax.ShapeDtypeStruct((t * ROW_TILES, LANES), F32),
                   jax.ShapeDtypeStruct((t, LANES), jnp.int32), jax.ShapeDtypeStruct((t, LANES), F32)],
        compiler_params=_cparams("parallel"),
        name="mix_router",
    )(x, oa, ob, oc, wout, g, wr)


def _route(idx, tile):
    t = idx.shape[0]
    n_tiles = (2 * t) // tile
    n_items = n_tiles + N_EXPERTS - 1
    expert = jnp.concatenate([idx[:, 0], idx[:, 1]])
    onehot = (expert[:, None] == jnp.arange(N_EXPERTS, dtype=jnp.int32)[None, :]).astype(jnp.int32)
    csum = jnp.cumsum(onehot, axis=0)
    rank = jnp.take_along_axis(csum, expert[:, None], axis=1)[:, 0] - 1
    group_end = jnp.cumsum(csum[-1])
    group_start = group_end - csum[-1]
    pos = (group_start[expert] + rank).astype(jnp.int32)
    tile_lo = jnp.arange(n_tiles, dtype=jnp.int32)[:, None] * tile
    present = (group_start[None, :] < tile_lo + tile) & (group_end[None, :] > tile_lo)
    seen = jnp.cumsum(present.reshape(-1).astype(jnp.int32))
    k = jnp.arange(n_items, dtype=jnp.int32)
    flat = jnp.sum((seen[None, :] <= jnp.minimum(k, seen[-1] - 1)[:, None]).astype(jnp.int32), axis=1)
    item_tile, item_expert = flat // N_EXPERTS, flat % N_EXPERTS
    lo = jnp.clip(group_start[item_expert] - item_tile * tile, 0, tile)
    hi = jnp.where(k < seen[-1], jnp.clip(group_end[item_expert] - item_tile * tile, 0, tile), lo)
    first = jnp.concatenate([jnp.ones((1,), jnp.int32), (item_tile[1:] != item_tile[:-1]).astype(jnp.int32)])
    return pos, tuple(a.astype(jnp.int32) for a in (item_tile, item_expert, lo, hi, first))


def _moe_dispatch_kernel(pos_ref, h_ref, x_hbm, buf, sem, *, t):
    tm = h_ref.shape[0] // ROW_TILES
    i = pl.program_id(0)
    n = pl.num_programs(0)
    slot = i % 2

    def row_copy(step, sl, j, r):
        a = j * t + step * tm + r
        return pltpu.make_async_copy(buf.at[sl, pl.ds(pl.multiple_of(r * ROW_TILES, ROW_TILES), ROW_TILES), :],
                                     x_hbm.at[pl.ds(pl.multiple_of(pos_ref[a] * ROW_TILES, ROW_TILES), ROW_TILES), :],
                                     sem.at[sl])

    def each_row(fn):
        def body(r, carry):
            fn(0, r)
            fn(1, r)
            return carry
        lax.fori_loop(0, tm, body, 0, unroll=4)

    @pl.when(i >= 2)
    def _():
        each_row(lambda j, r: row_copy(i - 2, slot, j, r).wait())

    buf[slot] = h_ref[...]
    each_row(lambda j, r: row_copy(i, slot, j, r).start())

    @pl.when(i == n - 1)
    def _():
        @pl.when(n >= 2)
        def _():
            each_row(lambda j, r: row_copy(i - 1, 1 - slot, j, r).wait())
        each_row(lambda j, r: row_copy(i, slot, j, r).wait())


def _moe_dispatch(h_rows, pos, tm):
    a_n = pos.shape[0]
    t = a_n // 2
    grid_spec = pltpu.PrefetchScalarGridSpec(
        num_scalar_prefetch=1, grid=(t // tm,),
        in_specs=[pl.BlockSpec((tm * ROW_TILES, LANES), lambda i, p: (i, 0))],
        out_specs=pl.BlockSpec(memory_space=pl.ANY),
        scratch_shapes=[pltpu.VMEM((2, tm * ROW_TILES, LANES), F32), pltpu.SemaphoreType.DMA((2,))])
    return pl.pallas_call(
        functools.partial(_moe_dispatch_kernel, t=t),
        grid_spec=grid_spec,
        out_shape=jax.ShapeDtypeStruct((a_n * ROW_TILES, LANES), F32),
        compiler_params=_cparams("arbitrary"),
        name="moe_dispatch",
    )(pos, h_rows)


def _moe_expert_kernel(it_ref, ie_ref, lo_ref, hi_ref, first_ref, x_ref, wg_ref, wu_ref, wd_ref, y_ref, *, tile):
    del it_ref, ie_ref
    k = pl.program_id(0)
    x = jnp.concatenate([x_ref[pl.ds(s, tile, stride=ROW_TILES), :] for s in range(ROW_TILES)], axis=1)
    x = x.astype(BF16)
    act = _silu(_dot(x, wg_ref[0])) * _dot(x, wu_ref[0])
    y = _dot(act.astype(BF16), wd_ref[0])
    row = lax.broadcasted_iota(jnp.int32, (tile, LANES), 0)
    mine = (row >= lo_ref[k]) & (row < hi_ref[k])

    @pl.when(first_ref[k] == 1)
    def _():
        for s in range(ROW_TILES):
            y_ref[pl.ds(s, tile, stride=ROW_TILES), :] = jnp.where(mine, y[:, s * LANES:(s + 1) * LANES], 0.0)

    @pl.when(first_ref[k] != 1)
    def _():
        for s in range(ROW_TILES):
            rows = pl.ds(s, tile, stride=ROW_TILES)
            y_ref[rows, :] = jnp.where(mine, y[:, s * LANES:(s + 1) * LANES], y_ref[rows, :])


def _moe_experts(x_rows, items, wg, wu, wd, tile):
    n_items = items[0].shape[0]
    _, _, d_ff = wg.shape
    rows_spec = pl.BlockSpec((tile * ROW_TILES, LANES), lambda k, it, ie, lo, hi, fi: (it[k], 0))

    def w_spec(shape):
        return pl.BlockSpec((1,) + shape, lambda k, it, ie, lo, hi, fi: (ie[k], 0, 0))

    grid_spec = pltpu.PrefetchScalarGridSpec(
        num_scalar_prefetch=5, grid=(n_items,),
        in_specs=[rows_spec, w_spec((D_MODEL, d_ff)), w_spec((D_MODEL, d_ff)), w_spec((d_ff, D_MODEL))],
        out_specs=rows_spec)
    return pl.pallas_call(
        functools.partial(_moe_expert_kernel, tile=tile),
        grid_spec=grid_spec,
        out_shape=jax.ShapeDtypeStruct(x_rows.shape, F32),
        compiler_params=_cparams("arbitrary"),
        name="moe_experts",
    )(*items, x_rows, wg, wu, wd)


def _moe_combine_kernel(pos_ref, x2_ref, gate_ref, y_hbm, o_ref, ybuf, sem, *, t):
    tm = x2_ref.shape[0]
    i = pl.program_id(0)
    n = pl.num_programs(0)
    slot = i % 2

    def row_copy(step, sl, j, r):
        a = j * t + step * tm + r
        return pltpu.make_async_copy(y_hbm.at[pl.ds(pl.multiple_of(pos_ref[a] * ROW_TILES, ROW_TILES), ROW_TILES), :],
                                     ybuf.at[sl, j, pl.ds(pl.multiple_of(r * ROW_TILES, ROW_TILES), ROW_TILES), :],
                                     sem.at[sl])

    def each_row(fn):
        def body(r, carry):
            fn(0, r)
            fn(1, r)
            return carry
        lax.fori_loop(0, tm, body, 0, unroll=4)

    @pl.when(i == 0)
    def _():
        each_row(lambda j, r: row_copy(0, 0, j, r).start())

    @pl.when(i + 1 < n)
    def _():
        each_row(lambda j, r: row_copy(i + 1, 1 - slot, j, r).start())

    each_row(lambda j, r: row_copy(i, slot, j, r).wait())
    gates = gate_ref[...]
    g0, g1 = gates[:, 0:1], gates[:, 1:2]
    for s in range(ROW_TILES):
        lanes = slice(s * LANES, (s + 1) * LANES)
        rows = pl.ds(s, tm, stride=ROW_TILES)
        o_ref[:, lanes] = x2_ref[:, lanes] + g0 * ybuf[slot, 0, rows, :] + g1 * ybuf[slot, 1, rows, :]


def _moe_combine(x2, gates, y, pos, tm):
    t = x2.shape[0]
    grid_spec = pltpu.PrefetchScalarGridSpec(
        num_scalar_prefetch=1, grid=(t // tm,),
        in_specs=[pl.BlockSpec((tm, D_MODEL), lambda i, p: (i, 0)), pl.BlockSpec((tm, LANES), lambda i, p: (i, 0)),
                  pl.BlockSpec(memory_space=pl.ANY)],
        out_specs=pl.BlockSpec((tm, D_MODEL), lambda i, p: (i, 0)),
        scratch_shapes=[pltpu.VMEM((2, 2, tm * ROW_TILES, LANES), F32), pltpu.SemaphoreType.DMA((2,))])
    return pl.pallas_call(
        functools.partial(_moe_combine_kernel, t=t),
        grid_spec=grid_spec,
        out_shape=jax.ShapeDtypeStruct((t, D_MODEL), F32),
        compiler_params=_cparams("arbitrary"),
        name="moe_combine",
    )(pos, x2, gates, y)


def _moe(x2, h_rows, idx, gates, wg, wu, wd, tile, tm):
    tile = math.gcd(2 * x2.shape[0], tile)
    pos, items = _route(idx, tile)
    x_rows = _moe_dispatch(h_rows, pos, tm)
    y = _moe_experts(x_rows, items, wg, wu, wd, tile)
    return _moe_combine(x2, gates, y, pos, tm)


def _block_diag_ones(n, group):
    idx = np.arange(n) // group
    return jnp.asarray((idx[:, None] == idx[None, :]).astype(np.float32), dtype=BF16)


def _rearrange_w_in(w):
    offs = np.cumsum([0, 128, 128, 256, GLA_RANK, 256, 256, 256, 256, 256, 512, 512, 512])
    a_q, a_k, a_v, a_lr, a_g, b_q, b_f, b_i, b_g, c_q, c_k, c_v = [
        w[:, offs[n]:offs[n + 1]] for n in range(12)]
    lr = jnp.pad(a_lr, ((0, 0), (0, LANES - GLA_RANK)))
    return jnp.concatenate([c_v, a_q, a_k, a_v, a_g, b_q, b_f, b_i, b_g, c_q, c_k, lr], axis=1).astype(BF16)


def _tile_rows(seq, cap):
    t = cap
    while seq % t:
        t //= 2
    return t


def kernel(x_prompt, x_sample, state_gla, state_hgrn, cache_k_win, cache_v_win, norm_mix_g, w_in,
           gla_w_gate_up, gla_b_gate, gla_norm_g, hgrn_lb_logits, hgrn_norm_g, attn_q_norm_g,
           attn_k_norm_g, w_out, norm_ffn_g, ffn_w_gate, ffn_w_up, ffn_w_down, moe_w_router,
           moe_w_gate, moe_w_up, moe_w_down):
    depth = w_in.shape[0]
    bp, lp, _ = x_prompt.shape
    bs, ls, _ = x_sample.shape
    buf = cache_k_win.shape[2]
    width = ATT_HEADS * HEAD_DIM
    assert lp % (BAND * max(DILATIONS)) == 0 and lp % CHUNK == 0 and ls % CHUNK != 0 and ls == SUBLANES
    assert buf >= BAND * max(DILATIONS) and lp % min(buf, lp) == 0

    lb_w = jax.nn.softmax(hgrn_lb_logits.astype(F32), axis=0)
    lower_bounds = jnp.cumsum(lb_w, axis=0) - lb_w[0]

    ones_att = _block_diag_ones(LANES, HEAD_DIM)
    ones_gla = _block_diag_ones(GLA_HEADS * GLA_DV, GLA_DV)
    ones_hgrn = _block_diag_ones(HGRN_HEADS * HGRN_DV, HGRN_DV)
    tm_p = _tile_rows(bp * lp, 256)
    tm_s = _tile_rows(bs * ls, 256)
    tables_p = _rope_tables(jnp.arange(lp, dtype=jnp.int32))
    tables_s = _rope_tables(jnp.tile(PAST_LEN + jnp.arange(ls, dtype=jnp.int32), tm_s // ls))
    ck = jnp.transpose(cache_k_win, (0, 1, 3, 4, 2))
    cv = jnp.transpose(cache_v_win, (0, 1, 3, 4, 2))

    def sample_heads(a):
        return a.reshape(bs, ls, ATT_HEADS, HEAD_DIM)

    def sample_tail(a):
        a = jnp.transpose(sample_heads(a), (0, 2, 3, 1))
        return jnp.pad(a, ((0, 0), (0, 0), (0, 0), (LANES - ls, 0)))

    xp = x_prompt.reshape(bp * lp, D_MODEL)
    xs = x_sample.reshape(bs * ls, D_MODEL)
    gla_p, hgrn_p, kw_p, vw_p, gla_s, hgrn_s = [], [], [], [], [], []
    caches_s = None
    keep = min(buf, lp)

    for layer in range(depth):
        w_proj = _rearrange_w_in(w_in[layer])
        g_mix = norm_mix_g[layer].reshape(1, D_MODEL)
        wup = jnp.pad(gla_w_gate_up[layer], ((0, LANES - GLA_RANK), (0, 0))).astype(BF16)
        bias = gla_b_gate[layer].reshape(1, -1)
        gain_a = jnp.tile(gla_norm_g[layer], GLA_HEADS).reshape(1, -1)
        gain_b = jnp.tile(hgrn_norm_g[layer], HGRN_HEADS).reshape(1, -1)
        lb = lower_bounds[layer].reshape(1, -1)
        gq = jnp.tile(attn_q_norm_g[layer], 2).reshape(1, LANES)
        gk = jnp.tile(attn_k_norm_g[layer], 2).reshape(1, LANES)
        wout = w_out[layer].astype(BF16)
        g_ffn = norm_ffn_g[layer].reshape(1, D_MODEL)

        mixed = []
        for which, x, batch, seq, tm in (("p", xp, bp, lp, tm_p), ("s", xs, bs, ls, tm_s)):
            proj = _inproj(x, g_mix, w_proj, _tile_rows(x.shape[0], 2 * tm))
            if which == "p":
                qn, kn, k_win, v_win = _qkrope_window(proj, gq, gk, tables_p, ones_att, batch, seq, keep)
                s0_a, s0_b, block, chunk, seqs = None, None, min(seq, 4 * CHUNK), CHUNK, 1
            else:
                qn, kn = _qkrope(proj, gq, gk, tables_s, ones_att, tm, 1)
                s0_a, s0_b, block, chunk, seqs = state_gla[layer], state_hgrn[layer], seq, seq, math.gcd(batch, SAMPLE_SEQS)
            o_a, s_a = _gated_linear("gla", proj, batch, seq, s0_a, gain_a, ones_gla, (wup, bias), block, chunk, seqs)
            o_b, s_b = _gated_linear("hgrn", proj, batch, seq, s0_b, gain_b, ones_hgrn, (lb,), block, chunk, seqs)
            if which == "p":
                o_c = _attn_prompt(qn, kn, proj, COL_CV, batch, seq)
                gla_p.append(s_a)
                hgrn_p.append(s_b)
                kw_p.append(k_win)
                vw_p.append(v_win)
            else:
                o_c, ck_new, cv_new = _attn_sample(
                    jnp.transpose(sample_heads(qn), (0, 2, 1, 3)), sample_tail(kn),
                    sample_tail(proj[:, COL_CV:COL_CV + width]), ck, cv, layer, caches_s, seq, 2)
                o_c = jnp.transpose(o_c, (0, 2, 1, 3)).reshape(batch * seq, width)
                caches_s = (ck_new, cv_new)
                gla_s.append(s_a)
                hgrn_s.append(s_b)
            mixed.append((o_a, o_b, o_c))

        j = layer // 2
        new_x = []
        for (o_a, o_b, o_c), x, tm in ((mixed[0], xp, tm_p), (mixed[1], xs, tm_s)):
            if layer % 2 == 0:
                new_x.append(_mix_ffn(x, o_a, o_b, o_c, wout, g_ffn, ffn_w_gate[j].astype(BF16),
                                      ffn_w_up[j].astype(BF16), ffn_w_down[j].astype(BF16),
                                      _tile_rows(x.shape[0], 2 * tm)))
            else:
                wr = jnp.pad(moe_w_router[j], ((0, 0), (0, LANES - N_EXPERTS)))
                wr_hi = wr.astype(BF16)
                wr = jnp.concatenate([wr_hi, (wr - wr_hi.astype(F32)).astype(BF16)], axis=1)
                x2, h_rows, idx, gates = _mix_router(x, o_a, o_b, o_c, wout, g_ffn, wr, tm)
                new_x.append(_moe(x2, h_rows, idx, gates, moe_w_gate[j].astype(BF16), moe_w_up[j].astype(BF16),
                                  moe_w_down[j].astype(BF16), MOE_TILE, tm))
        xp, xs = new_x

    return (xp.reshape(bp, lp, D_MODEL), xs.reshape(bs, ls, D_MODEL),
            jnp.stack(gla_p), jnp.stack(hgrn_p),
            jnp.transpose(jnp.stack(kw_p), (0, 1, 4, 2, 3)), jnp.transpose(jnp.stack(vw_p), (0, 1, 4, 2, 3)),
            jnp.stack(gla_s), jnp.stack(hgrn_s),
            jnp.transpose(caches_s[0], (0, 1, 4, 2, 3)), jnp.transpose(caches_s[1], (0, 1, 4, 2, 3)))
```

```python
import functools
import math

import numpy as np
import jax
import jax.numpy as jnp
from jax import lax
from jax.experimental import pallas as pl
from jax.experimental.pallas import tpu as pltpu

F32 = jnp.float32
BF16 = jnp.bfloat16

D_MODEL = 1024
HEAD_DIM = 64
GLA_HEADS = 4
GLA_DK = 32
GLA_DV = 64
GLA_RANK = 16
GLA_TAU = 16.0
HGRN_HEADS = 4
HGRN_DK = 64
HGRN_DV = 64
ATT_HEADS = 8
DILATIONS = (1, 4, 16)
BAND = 128
ROT_DIM = 16
ROPE_THETA = 500000.0
CHUNK = 64
N_EXPERTS = 8
EPS = 1e-6
PAST_LEN = 8192
NEG = -1e30

LANES = 128
SUBLANES = 8
VMEM_LIMIT = 56 * 1024 * 1024
ROW_TILES = D_MODEL // LANES
SAMPLE_SEQS = 16
MOE_TILE = 256

COL_CV = 0
COL_AQ, COL_AK, COL_AV, COL_AG = 512, 640, 768, 1024
COL_BQ, COL_BF, COL_BI, COL_BG = 1280, 1536, 1792, 2048
COL_CQ, COL_CK = 2304, 2816
COL_LR = 3328
N_PROJ = 3456


def _cparams(*sem):
    return pltpu.CompilerParams(dimension_semantics=sem, vmem_limit_bytes=VMEM_LIMIT)


def _sigmoid(x):
    return 1.0 / (1.0 + jnp.exp(-x))


def _silu(x):
    return x * _sigmoid(x)


def _log_sigmoid(x):
    return -(jnp.maximum(-x, 0.0) + jnp.log(1.0 + jnp.exp(-jnp.abs(x))))


def _dot(a, b):
    return jnp.dot(a, b, preferred_element_type=F32)


def _dot_nt(a, b):
    return lax.dot_general(a, b, (((1,), (1,)), ((), ())), preferred_element_type=F32)


def _dot_tn(a, b):
    return lax.dot_general(a, b, (((0,), (0,)), ((), ())), preferred_element_type=F32)


def _group_sum(x, ones_bd):
    hi = x.astype(BF16)
    lo = (x - hi.astype(F32)).astype(BF16)
    return _dot(hi, ones_bd) + _dot(lo, ones_bd)


def _inproj_kernel(x_ref, g_ref, w_ref, o_ref):
    x = x_ref[...]
    ms = jnp.mean(x * x, axis=-1, keepdims=True)
    h = (x * lax.rsqrt(ms + EPS)) * g_ref[...]
    o_ref[...] = _dot(h.astype(BF16), w_ref[...])


def _inproj(x, g, w, tm):
    t = x.shape[0]
    return pl.pallas_call(
        _inproj_kernel,
        grid=(t // tm,),
        in_specs=[pl.BlockSpec((tm, D_MODEL), lambda i: (i, 0)),
                  pl.BlockSpec((1, D_MODEL), lambda i: (0, 0)),
                  pl.BlockSpec((D_MODEL, N_PROJ), lambda i: (0, 0))],
        out_specs=pl.BlockSpec((tm, N_PROJ), lambda i: (i, 0)),
        out_shape=jax.ShapeDtypeStruct((t, N_PROJ), F32),
        compiler_params=_cparams("parallel"),
        name="inproj",
    )(x, g, w)


def _qkrope_kernel(q_ref, k_ref, *rest, window):
    if window:
        v_ref, gq_ref, gk_ref, cos_ref, sa_ref, sb_ref, ones_ref, qo_ref, ko_ref, kt_ref, vt_ref = rest
    else:
        gq_ref, gk_ref, cos_ref, sa_ref, sb_ref, ones_ref, qo_ref, ko_ref = rest
    cos, sa, sb = cos_ref[...], sa_ref[...], sb_ref[...]
    ones_bd = ones_ref[...]

    def norm_rope(x, g):
        ms = _group_sum(x * x, ones_bd) * (1.0 / HEAD_DIM)
        y = (x * lax.rsqrt(ms + EPS)) * g
        up = pltpu.roll(y, LANES - ROT_DIM // 2, axis=1)
        dn = pltpu.roll(y, ROT_DIM // 2, axis=1)
        return y * cos + up * sa + dn * sb

    qo_ref[...] = norm_rope(q_ref[...], gq_ref[...]) * (HEAD_DIM ** -0.5)
    k = norm_rope(k_ref[...], gk_ref[...])
    ko_ref[...] = k
    if window:
        @pl.when(pl.program_id(0) == pl.num_programs(0) - 1)
        def _():
            kt_ref[0] = k.T.reshape(kt_ref.shape[1:])
            vt_ref[0] = v_ref[...].T.reshape(vt_ref.shape[1:])


def _rope_tables(pos):
    half = ROT_DIM // 2
    inv_freq = ROPE_THETA ** (-jnp.arange(0, ROT_DIM, 2, dtype=F32) / ROT_DIM)
    ang = pos.astype(F32)[:, None] * inv_freq[None, :]
    cos, sin = jnp.cos(ang), jnp.sin(ang)
    n = pos.shape[0]
    rest = HEAD_DIM - ROT_DIM
    c = jnp.concatenate([cos, cos, jnp.ones((n, rest), F32)], axis=1)
    sa = jnp.concatenate([-sin, jnp.zeros((n, half + rest), F32)], axis=1)
    sb = jnp.concatenate([jnp.zeros((n, half), F32), sin, jnp.zeros((n, rest), F32)], axis=1)
    return tuple(jnp.tile(a, (1, 2)) for a in (c, sa, sb))


def _qkrope(proj, gq, gk, tables, ones_bd, tm, table_blocks):
    t = proj.shape[0]
    cq, ck = COL_CQ // LANES, COL_CK // LANES
    tab_spec = pl.BlockSpec((tm, LANES), lambda i, j: (i % table_blocks, 0))
    row_spec = pl.BlockSpec((1, LANES), lambda i, j: (0, 0))
    out_spec = pl.BlockSpec((tm, LANES), lambda i, j: (i, j))
    out_sds = jax.ShapeDtypeStruct((t, ATT_HEADS * HEAD_DIM), F32)
    return pl.pallas_call(
        functools.partial(_qkrope_kernel, window=False),
        grid=(t // tm, ATT_HEADS // 2),
        in_specs=[pl.BlockSpec((tm, LANES), lambda i, j: (i, cq + j)),
                  pl.BlockSpec((tm, LANES), lambda i, j: (i, ck + j)),
                  row_spec, row_spec, tab_spec, tab_spec, tab_spec,
                  pl.BlockSpec((LANES, LANES), lambda i, j: (0, 0))],
        out_specs=[out_spec, out_spec],
        out_shape=[out_sds, out_sds],
        compiler_params=_cparams("parallel", "parallel"),
        name="qkrope",
    )(proj, proj, gq, gk, *tables, ones_bd)


def _qkrope_window(proj, gq, gk, tables, ones_bd, batch, seq, window):
    cq, ck, cv = COL_CQ // LANES, COL_CK // LANES, COL_CV // LANES
    nt = seq // window

    def tok(col):
        return pl.BlockSpec((window, LANES), lambda s, b, j: (b * nt + s, col + j))

    def win_index(s, b, j):
        last = (s == nt - 1).astype(jnp.int32)
        return (b * last, j * last, 0, 0)

    tab_spec = pl.BlockSpec((window, LANES), lambda s, b, j: (s, 0))
    row_spec = pl.BlockSpec((1, LANES), lambda s, b, j: (0, 0))
    win_spec = pl.BlockSpec((1, 2, HEAD_DIM, window), win_index)
    out_sds = jax.ShapeDtypeStruct((batch * seq, ATT_HEADS * HEAD_DIM), F32)
    win_sds = jax.ShapeDtypeStruct((batch, ATT_HEADS, HEAD_DIM, window), F32)
    return pl.pallas_call(
        functools.partial(_qkrope_kernel, window=True),
        grid=(nt, batch, ATT_HEADS // 2),
        in_specs=[tok(cq), tok(ck), tok(cv), row_spec, row_spec, tab_spec, tab_spec, tab_spec,
                  pl.BlockSpec((LANES, LANES), lambda s, b, j: (0, 0))],
        out_specs=[tok(0), tok(0), win_spec, win_spec],
        out_shape=[out_sds, out_sds, win_sds, win_sds],
        compiler_params=_cparams("arbitrary", "arbitrary", "arbitrary"),
        name="qkrope_window",
    )(proj, proj, proj, gq, gk, *tables, ones_bd)


def _segment_scans(g, chunk):
    rows, hk = g.shape
    n_chunks = rows // chunk
    row = lax.broadcasted_iota(jnp.int32, g.shape, 0)

    def at(x, shift):
        return pltpu.roll(x, shift % rows, axis=0)

    zero = jnp.zeros_like(g)
    c = {1: g, 2: g + jnp.where(row % 2 == 1, at(g, 1), zero)}
    c[4] = c[2] + jnp.where(row % 4 == 2, at(c[2], 1), zero) + jnp.where(row % 4 == 3, at(c[2], 2), zero)
    d = {1: zero, 2: jnp.where(row % 2 == 0, at(g, -1), zero)}
    c4_end = jnp.where(row % 4 == 0, at(c[4], -3),
                       jnp.where(row % 4 == 1, at(c[4], -2),
                                 jnp.where(row % 4 == 2, at(c[4], -1), c[4])))
    d[4] = c4_end - c[4]
    tiles = chunk // SUBLANES
    c4_t = c[4].reshape(n_chunks * tiles, SUBLANES, hk)
    row_t = row.reshape(n_chunks * tiles, SUBLANES, hk)
    c8_t = c4_t + jnp.where(row_t % 8 >= 4, c4_t[:, 3:4, :], jnp.zeros_like(c4_t))
    ends = c8_t[:, SUBLANES - 1:SUBLANES, :]
    prefix = []
    for ci in range(n_chunks):
        acc = [jnp.zeros((1, 1, hk), F32)]
        for j in range(tiles):
            acc.append(acc[-1] + ends[ci * tiles + j:ci * tiles + j + 1])
        prefix.append(acc)

    def per_tile(pick):
        return jnp.concatenate([prefix[ci][pick(j)] for ci in range(n_chunks) for j in range(tiles)], axis=0)

    b_t = c8_t + per_tile(lambda j: j)
    width = 1
    while width * SUBLANES < chunk:
        m = width * SUBLANES
        c[m] = (b_t - per_tile(lambda j: (j // width) * width)).reshape(rows, hk)
        d[m] = (per_tile(lambda j: (j // width + 1) * width) - b_t).reshape(rows, hk)
        width *= 2
    b_rest = (per_tile(lambda j: tiles) - b_t).reshape(rows, hk)
    b_last = [prefix[ci][tiles].reshape(1, hk) for ci in range(n_chunks)]
    return c, d, b_t.reshape(rows, hk), b_rest, b_last


def _pair_masks(chunk, copies):
    t = lax.broadcasted_iota(jnp.int32, (copies * chunk, chunk), 0) % chunk
    s = lax.broadcasted_iota(jnp.int32, (copies * chunk, chunk), 1)
    masks = {0: t == s}
    m = 1
    while m < chunk:
        masks[m] = (t // (2 * m) == s // (2 * m)) & (t % (2 * m) >= m) & (s % (2 * m) < m)
        m *= 2
    return masks


def _gl_block(q, k, g, v, st, heads, dk, dv, chunk):
    rows = q.shape[0]
    hk, hv = heads * dk, heads * dv
    stacked_masks = _pair_masks(chunk, heads)
    levels = sorted(stacked_masks)
    c, d, b, b_rest, b_last = _segment_scans(g, chunk)
    k_bf = k.astype(BF16)
    q_lvl = {0: q.astype(BF16)}
    k_lvl = {0: k_bf, 1: k_bf}
    for m in levels[1:]:
        q_lvl[m] = (q * jnp.exp(c[m])).astype(BF16)
        if m > 1:
            k_lvl[m] = (k * jnp.exp(d[m])).astype(BF16)
    q_in = (q * jnp.exp(b)).astype(BF16)
    k_out = (k * jnp.exp(b_rest)).astype(BF16)
    v_bf = v.astype(BF16)
    lane_k = lax.broadcasted_iota(jnp.int32, (chunk, hk), 1) // dk
    lane_v = lax.broadcasted_iota(jnp.int32, (chunk, hv), 1) // dv
    same_head = (lax.broadcasted_iota(jnp.int32, (hv, hk), 0) // dv
                 == lax.broadcasted_iota(jnp.int32, (hv, hk), 1) // dk)
    zero_k = jnp.zeros((chunk, hk), BF16)
    zero_v = jnp.zeros((chunk, hv), BF16)
    o_intra, st_inc = [], []
    for ci in range(rows // chunk):
        r = slice(ci * chunk, (ci + 1) * chunk)
        a = jnp.zeros((heads * chunk, chunk), F32)
        for m in levels:
            q_heads = jnp.concatenate([jnp.where(lane_k == h, q_lvl[m][r], zero_k) for h in range(heads)], axis=0)
            a = jnp.where(stacked_masks[m], _dot_nt(q_heads, k_lvl[m][r]), a)
        a = a.astype(BF16)
        o = None
        for h in range(heads):
            part = _dot(a[h * chunk:(h + 1) * chunk], jnp.where(lane_v == h, v_bf[r], zero_v))
            o = part if o is None else o + part
        o_intra.append(o)
        st_inc.append(jnp.where(same_head, _dot_tn(v_bf[r], k_out[r]), 0.0))
    independent = isinstance(st, list)
    outs, finals = [], []
    for ci in range(rows // chunk):
        r = slice(ci * chunk, (ci + 1) * chunk)
        cur = st[ci] if independent else st
        outs.append(o_intra[ci] + _dot_nt(q_in[r], cur.astype(BF16)))
        nxt = cur * jnp.exp(b_last[ci]) + st_inc[ci]
        if independent:
            finals.append(nxt)
        else:
            st = nxt
    return (outs[0] if len(outs) == 1 else jnp.concatenate(outs, axis=0)), (finals if independent else st)


def _head_norm_gate(o, gain, gate, ones_bd, dv):
    ms = _group_sum(o * o, ones_bd) * (1.0 / dv)
    return ((o * lax.rsqrt(ms + EPS)) * gain) * _silu(gate)


def _gl_kernel(*refs, mode, heads, dk, dv, chunk, has_init):
    refs = list(refs)
    n_in = 9 if mode == "gla" else 7
    s0_ref = refs.pop(n_in) if has_init else None
    if mode == "gla":
        (q_ref, k_ref, v_ref, gate_ref, lr_ref, wup_ref, bias_ref, gain_ref, ones_ref,
         o_ref, sT_ref, st_ref) = refs
        q = q_ref[...] * (dk ** -0.5)
        k = k_ref[...]
        g = _log_sigmoid(_dot(lr_ref[...].astype(BF16), wup_ref[...]) + bias_ref[...]) * (1.0 / GLA_TAU)
    else:
        (q_ref, f_ref, v_ref, gate_ref, lb_ref, gain_ref, ones_ref,
         o_ref, sT_ref, st_ref) = refs
        lb = lb_ref[...]
        q = _silu(q_ref[...])
        z = f_ref[...]
        g = jnp.log(lb + (1.0 - lb) * _sigmoid(z))
        k = (1.0 - lb) * _sigmoid(-z)
    hk, hv = heads * dk, heads * dv
    eye = (lax.broadcasted_iota(jnp.int32, (hv, dv), 0) % dv
           == lax.broadcasted_iota(jnp.int32, (hv, dv), 1)).astype(BF16)
    same_head = (lax.broadcasted_iota(jnp.int32, (hv, hk), 0) // dv
                 == lax.broadcasted_iota(jnp.int32, (hv, hk), 1) // dk)

    def split3(x):
        hi = x.astype(BF16)
        r = x - hi.astype(F32)
        mid = r.astype(BF16)
        return hi, mid, (r - mid.astype(F32)).astype(BF16)

    def state_in(s):
        parts = split3(s.reshape(hk, dv))
        full = _dot_nt(eye, parts[0]) + _dot_nt(eye, parts[1]) + _dot_nt(eye, parts[2])
        return jnp.where(same_head, full, 0.0)

    def state_out(st):
        parts = split3(st)
        return (_dot_tn(parts[0], eye) + _dot_tn(parts[1], eye) + _dot_tn(parts[2], eye)).reshape(heads, dk, dv)

    seqs = sT_ref.shape[0]
    if seqs > 1:
        o, finals = _gl_block(q, k, g, v_ref[...], [state_in(s0_ref[n]) for n in range(seqs)], heads, dk, dv, chunk)
        for n in range(seqs):
            sT_ref[n] = state_out(finals[n])
    else:
        i = pl.program_id(1)

        @pl.when(i == 0)
        def _():
            st_ref[...] = jnp.zeros(st_ref.shape, F32) if s0_ref is None else state_in(s0_ref[0])

        o, st = _gl_block(q, k, g, v_ref[...], st_ref[...], heads, dk, dv, chunk)
        st_ref[...] = st

        @pl.when(i == pl.num_programs(1) - 1)
        def _():
            sT_ref[0] = state_out(st)
    o_ref[...] = _head_norm_gate(o, gain_ref[...], gate_ref[...], ones_ref[...], dv)


def _gated_linear(mode, proj, batch, seq, s0, gain, ones_bd, extra, block_tokens, chunk, seqs=1):
    assert seqs == 1 or (block_tokens == seq == chunk and batch % seqs == 0)
    heads, dk, dv = (GLA_HEADS, GLA_DK, GLA_DV) if mode == "gla" else (HGRN_HEADS, HGRN_DK, HGRN_DV)
    hk, hv = heads * dk, heads * dv
    nb = seq // block_tokens
    tb = block_tokens * seqs

    def cols(width, col):
        return pl.BlockSpec((tb, width), lambda b, i: (b * nb + i, col // width))

    def const(shape):
        return pl.BlockSpec(shape, lambda b, i: (0,) * len(shape))

    state_spec = pl.BlockSpec((seqs, heads, dk, dv), lambda b, i: (b, 0, 0, 0))
    if mode == "gla":
        wup, bias = extra
        in_specs = [cols(hk, COL_AQ), cols(hk, COL_AK), cols(hv, COL_AV), cols(hv, COL_AG), cols(LANES, COL_LR),
                    const((LANES, hk)), const((1, hk)), const((1, hv)), const((hv, hv))]
        args = (proj, proj, proj, proj, proj, wup, bias, gain, ones_bd)
    else:
        (lb,) = extra
        in_specs = [cols(hk, COL_BQ), cols(hk, COL_BF), cols(hv, COL_BI), cols(hv, COL_BG),
                    const((1, hk)), const((1, hv)), const((hv, hv))]
        args = (proj, proj, proj, proj, lb, gain, ones_bd)
    if s0 is not None:
        in_specs, args = in_specs + [state_spec], args + (s0,)
    return pl.pallas_call(
        functools.partial(_gl_kernel, mode=mode, heads=heads, dk=dk, dv=dv, chunk=chunk, has_init=s0 is not None),
        grid=(batch // seqs, nb),
        in_specs=in_specs,
        out_specs=[pl.BlockSpec((tb, hv), lambda b, i: (b * nb + i, 0)), state_spec],
        out_shape=[jax.ShapeDtypeStruct((batch * seq, hv), F32),
                   jax.ShapeDtypeStruct((batch, heads, dk, dv), F32)],
        scratch_shapes=[pltpu.VMEM((hv, hk), F32)],
        compiler_params=_cparams("parallel", "arbitrary"),
        name=mode + "_scan",
    )(*args)


def _attn_prompt_kernel(q_ref, k_ref, v_ref, o_ref, qr_ref, kt_ref, kr_ref, vt_ref, vr_ref,
                        m_ref, l_ref, acc_ref, *, seq):
    n_res = max(DILATIONS)
    per_res = seq // n_res
    head0 = lax.broadcasted_iota(jnp.int32, (per_res, LANES), 1) < HEAD_DIM
    for r in range(n_res):
        tok = slice(r * per_res, (r + 1) * per_res)
        strided = pl.ds(r, per_res, stride=n_res)
        k_t, v_t = k_ref[tok, :], v_ref[tok, :]
        k_r, v_r = k_ref[strided, :], v_ref[strided, :]
        qr_ref[tok, :] = q_ref[strided, :]
        for h in range(2):
            mine = head0 if h == 0 else ~head0
            kt_ref[h, tok, :] = jnp.where(mine, k_t, 0.0).astype(BF16)
            kr_ref[h, tok, :] = jnp.where(mine, k_r, 0.0).astype(BF16)
            vt_ref[h, tok, :] = jnp.where(mine, v_t, 1.0).astype(BF16)
            vr_ref[h, tok, :] = jnp.where(mine, v_r, 1.0).astype(BF16)
    m_ref[...] = jnp.full(m_ref.shape, NEG, F32)
    l_ref[...] = jnp.zeros(l_ref.shape, F32)
    acc_ref[...] = jnp.zeros(acc_ref.shape, F32)

    head0 = lax.broadcasted_iota(jnp.int32, (BAND, LANES), 1) < HEAD_DIM
    qi = lax.broadcasted_iota(jnp.int32, (BAND, 2 * BAND), 0)
    ki = lax.broadcasted_iota(jnp.int32, (BAND, 2 * BAND), 1)

    for dil in DILATIONS:
        n_runs = n_res // dil
        run = SUBLANES * dil
        keys_by_token = dil == 1

        def seq_index(u, n_runs=n_runs, run=run):
            return n_runs * (u % run) + u // run

        kj = ki if keys_by_token else seq_index(ki % BAND) + BAND * (ki // BAND)
        rel = BAND + seq_index(qi) - kj
        in_band = (rel >= 0) & (rel <= BAND)
        bias_prev = jnp.where(in_band, 0.0, NEG).astype(F32)
        bias_first = jnp.where(in_band & (ki >= BAND), 0.0, NEG).astype(F32)

        def body(idx, carry, dil=dil, n_runs=n_runs, run=run, keys_by_token=keys_by_token,
                 bias_prev=bias_prev, bias_first=bias_first):
            res = idx % dil
            blk = idx // dil
            starts = [pl.multiple_of((res + dil * c) * per_res + run * blk, run) for c in range(n_runs)]
            prevs = [pl.multiple_of((res + dil * c) * per_res + run * jnp.maximum(blk - 1, 0), run)
                     for c in range(n_runs)]

            def gather(ref, offs, *lead):
                parts = [ref[lead + (pl.ds(o, run), slice(None))] for o in offs]
                return parts[0] if len(parts) == 1 else jnp.concatenate(parts, axis=0)

            def scatter(ref, offs, val, *lead):
                for c, o in enumerate(offs):
                    ref[lead + (pl.ds(o, run), slice(None))] = val[c * run:(c + 1) * run, :]

            q = gather(qr_ref, starts).astype(BF16)
            bias = jnp.where(blk > 0, bias_prev, bias_first)
            alphas, pvs = [], []
            for h in range(2):
                if keys_by_token:
                    cur = pl.multiple_of(blk * BAND, BAND)
                    prev = pl.multiple_of(jnp.maximum(blk - 1, 0) * BAND, BAND)
                    k2 = jnp.concatenate([kt_ref[h, pl.ds(prev, BAND), :], kt_ref[h, pl.ds(cur, BAND), :]], axis=0)
                    v2 = jnp.concatenate([vt_ref[h, pl.ds(prev, BAND), :], vt_ref[h, pl.ds(cur, BAND), :]], axis=0)
                else:
                    k2 = jnp.concatenate([gather(kr_ref, prevs, h), gather(kr_ref, starts, h)], axis=0)
                    v2 = jnp.concatenate([gather(vr_ref, prevs, h), gather(vr_ref, starts, h)], axis=0)
                s = _dot_nt(q, k2) + bias
                m_prev = gather(m_ref, starts, h)
                m_new = jnp.maximum(m_prev, jnp.max(s, axis=-1, keepdims=True))
                scatter(m_ref, starts, m_new, h)
                p = jnp.exp(s - jnp.concatenate([m_new, m_new], axis=1))
                alphas.append(jnp.exp(m_prev - m_new))
                pvs.append(_dot(p.astype(BF16), v2))
            scatter(l_ref, starts, jnp.where(head0, alphas[1], alphas[0]) * gather(l_ref, starts)
                    + jnp.where(head0, pvs[1], pvs[0]))
            scatter(acc_ref, starts, jnp.where(head0, alphas[0], alphas[1]) * gather(acc_ref, starts)
                    + jnp.where(head0, pvs[0], pvs[1]))
            return carry

        lax.fori_loop(0, seq // BAND, body, 0, unroll=32)

    for r in range(n_res):
        tok = slice(r * per_res, (r + 1) * per_res)
        o_ref[pl.ds(r, per_res, stride=n_res), :] = acc_ref[tok, :] / pltpu.roll(l_ref[tok, :], HEAD_DIM, axis=1)


def _attn_prompt(q, k, v_src, v_col, batch, seq):
    spec = pl.BlockSpec((seq, LANES), lambda b, j: (b, j))
    vc = v_col // LANES
    return pl.pallas_call(
        functools.partial(_attn_prompt_kernel, seq=seq),
        grid=(batch, ATT_HEADS // 2),
        in_specs=[spec, spec, pl.BlockSpec((seq, LANES), lambda b, j: (b, vc + j))],
        out_specs=spec,
        out_shape=jax.ShapeDtypeStruct((batch * seq, ATT_HEADS * HEAD_DIM), F32),
        scratch_shapes=[pltpu.VMEM((seq, LANES), F32),
                        pltpu.VMEM((2, seq, LANES), BF16), pltpu.VMEM((2, seq, LANES), BF16),
                        pltpu.VMEM((2, seq, LANES), BF16), pltpu.VMEM((2, seq, LANES), BF16),
                        pltpu.VMEM((2, seq, LANES), F32), pltpu.VMEM((seq, LANES), F32),
                        pltpu.VMEM((seq, LANES), F32)],
        compiler_params=_cparams("parallel", "parallel"),
        name="attn_prompt",
    )(q, k, v_src)


def _sample_counts(steps, buf):
    t = np.arange(steps)[:, None]

    def count(dist, live):
        cnt = np.zeros(dist.shape, np.float32)
        for dil in DILATIONS:
            cnt += (live & (dist >= 0) & (dist % dil == 0) & (dist <= BAND * dil)).astype(np.float32)
        return cnt

    lane = np.arange(LANES)[None, :]
    return (count(buf + t - np.arange(buf)[None, :], True),
            count(t - (lane - (LANES - steps)), lane >= LANES - steps))


def _attn_sample_kernel(q_ref, kn_ref, vn_ref, kc_ref, vc_ref, cc_ref, cn_ref, *rest, steps, aliased):
    if aliased:
        rest = rest[2:]
    o_ref, ko_ref, vo_ref = rest
    q = q_ref[0].astype(BF16)
    k_old, v_old = kc_ref[0, 0], vc_ref[0, 0]
    k_new, v_new = kn_ref[0], vn_ref[0]
    cc, cn = cc_ref[...][None], cn_ref[...][None]

    def scores(keys):
        return lax.dot_general(q, keys.astype(BF16), (((2,), (1,)), ((0,), (0,))), preferred_element_type=F32)

    def weighted(p, values):
        return lax.dot_general(p.astype(BF16), values.astype(BF16), (((2,), (2,)), ((0,), (0,))),
                               preferred_element_type=F32)

    s_old = jnp.where(cc > 0.0, scores(k_old), NEG)
    s_new = jnp.where(cn > 0.0, scores(k_new), NEG)
    m = jnp.maximum(jnp.max(s_old, axis=-1, keepdims=True), jnp.max(s_new, axis=-1, keepdims=True))
    p_old = cc * jnp.exp(s_old - m)
    p_new = cn * jnp.exp(s_new - m)
    den = jnp.sum(p_old, axis=-1, keepdims=True) + jnp.sum(p_new, axis=-1, keepdims=True)
    o_ref[0] = (weighted(p_old, v_old) + weighted(p_new, v_new)) / den

    buf = k_old.shape[-1]
    lane = lax.broadcasted_iota(jnp.int32, k_new.shape, 2)
    for old, new, out_ref in ((k_old, k_new, ko_ref), (v_old, v_new, vo_ref)):
        shifted = pltpu.roll(old, buf - steps, axis=2)
        out_ref[0, 0, :, :, 0:buf - LANES] = shifted[:, :, 0:buf - LANES]
        out_ref[0, 0, :, :, buf - LANES:buf] = jnp.where(lane >= LANES - steps, new, shifted[:, :, buf - LANES:buf])


def _attn_sample(q, k_new, v_new, cache_k, cache_v, layer, prev_out, steps, head_split):
    depth, batch, heads, hd, buf = cache_k.shape
    hb = heads // head_split
    cc, cn = (jnp.asarray(c) for c in _sample_counts(steps, buf))
    q_spec = pl.BlockSpec((1, hb, steps, hd), lambda b, j: (b, j, 0, 0))
    new_spec = pl.BlockSpec((1, hb, hd, LANES), lambda b, j: (b, j, 0, 0))
    cache_spec = pl.BlockSpec((1, 1, hb, hd, buf), lambda b, j: (layer, b, j, 0, 0))
    in_specs = [q_spec, new_spec, new_spec, cache_spec, cache_spec,
                pl.BlockSpec((steps, buf), lambda b, j: (0, 0)), pl.BlockSpec((steps, LANES), lambda b, j: (0, 0))]
    args = [q, k_new, v_new, cache_k, cache_v, cc, cn]
    aliases = {}
    if prev_out is not None:
        any_spec = pl.BlockSpec(memory_space=pl.ANY)
        in_specs += [any_spec, any_spec]
        args += list(prev_out)
        aliases = {7: 1, 8: 2}
    cache_sds = jax.ShapeDtypeStruct(cache_k.shape, F32)
    return pl.pallas_call(
        functools.partial(_attn_sample_kernel, steps=steps, aliased=prev_out is not None),
        grid=(batch, head_split),
        in_specs=in_specs,
        out_specs=[q_spec, cache_spec, cache_spec],
        out_shape=[jax.ShapeDtypeStruct(q.shape, F32), cache_sds, cache_sds],
        input_output_aliases=aliases,
        compiler_params=_cparams("arbitrary", "arbitrary"),
        name="attn_sample",
    )(*args)


def _outproj(oa_ref, ob_ref, oc_ref, wout_ref, x_ref):
    na, nb = oa_ref.shape[1], ob_ref.shape[1]
    mixed = (_dot(oa_ref[...].astype(BF16), wout_ref[0:na, :])
             + _dot(ob_ref[...].astype(BF16), wout_ref[na:na + nb, :])
             + _dot(oc_ref[...].astype(BF16), wout_ref[na + nb:, :]))
    return x_ref[...] + mixed


def _rmsnorm(x, g):
    return (x * lax.rsqrt(jnp.mean(x * x, axis=-1, keepdims=True) + EPS)) * g


def _mix_ffn_kernel(x_ref, oa_ref, ob_ref, oc_ref, wout_ref, g_ref, wg_ref, wu_ref, wd_ref, o_ref):
    x2 = _outproj(oa_ref, ob_ref, oc_ref, wout_ref, x_ref)
    h = _rmsnorm(x2, g_ref[...]).astype(BF16)
    act = _silu(_dot(h, wg_ref[...])) * _dot(h, wu_ref[...])
    o_ref[...] = x2 + _dot(act.astype(BF16), wd_ref[...])


def _mix_ffn(x, oa, ob, oc, wout, g, wg, wu, wd, tm):
    t = x.shape[0]
    d_ff = wg.shape[1]

    def rows(width):
        return pl.BlockSpec((tm, width), lambda i: (i, 0))

    def const(shape):
        return pl.BlockSpec(shape, lambda i: (0, 0), pipeline_mode=pl.Buffered(1))

    return pl.pallas_call(
        _mix_ffn_kernel,
        grid=(t // tm,),
        in_specs=[rows(D_MODEL), rows(oa.shape[1]), rows(ob.shape[1]), rows(oc.shape[1]),
                  const((D_MODEL, D_MODEL)), const((1, D_MODEL)),
                  const((D_MODEL, d_ff)), const((D_MODEL, d_ff)), const((d_ff, D_MODEL))],
        out_specs=rows(D_MODEL),
        out_shape=jax.ShapeDtypeStruct((t, D_MODEL), F32),
        compiler_params=_cparams("parallel"),
        name="mix_ffn",
    )(x, oa, ob, oc, wout, g, wg, wu, wd)


def _mix_router_kernel(x_ref, oa_ref, ob_ref, oc_ref, wout_ref, g_ref, wr_ref, x2_ref, h_ref, idx_ref, gate_ref):
    tm = x_ref.shape[0]
    x2 = _outproj(oa_ref, ob_ref, oc_ref, wout_ref, x_ref)
    x2_ref[...] = x2
    h = _rmsnorm(x2, g_ref[...])
    for s in range(D_MODEL // LANES):
        h_ref[pl.ds(s, tm, stride=D_MODEL // LANES), :] = h[:, s * LANES:(s + 1) * LANES]
    h_hi = h.astype(BF16)
    h_lo = (h - h_hi.astype(F32)).astype(BF16)
    by_hi = _dot(h_hi, wr_ref[...])
    logits = by_hi[:, 0:LANES] + by_hi[:, LANES:2 * LANES] + _dot(h_lo, wr_ref[:, 0:LANES])
    lane = lax.broadcasted_iota(jnp.int32, logits.shape, 1)
    logits = jnp.where(lane < N_EXPERTS, logits, -jnp.inf)
    v1 = jnp.max(logits, axis=-1, keepdims=True)
    i1 = jnp.min(jnp.where(logits == v1, lane, LANES), axis=-1, keepdims=True)
    rest = jnp.where(lane == i1, -jnp.inf, logits)
    v2 = jnp.max(rest, axis=-1, keepdims=True)
    i2 = jnp.min(jnp.where(rest == v2, lane, LANES), axis=-1, keepdims=True)
    e2 = jnp.exp(v2 - v1)
    idx_ref[...] = jnp.where(lane == 0, i1, jnp.where(lane == 1, i2, 0))
    gate_ref[...] = jnp.where(lane == 0, 1.0 / (1.0 + e2), jnp.where(lane == 1, e2 / (1.0 + e2), 0.0))


def _mix_router(x, oa, ob, oc, wout, g, wr, tm):
    t = x.shape[0]

    def rows(width):
        return pl.BlockSpec((tm, width), lambda i: (i, 0))

    def const(shape):
        return pl.BlockSpec(shape, lambda i: (0, 0))

    return pl.pallas_call(
        _mix_router_kernel,
        grid=(t // tm,),
        in_specs=[rows(D_MODEL), rows(oa.shape[1]), rows(ob.shape[1]), rows(oc.shape[1]),
                  const((D_MODEL, D_MODEL)), const((1, D_MODEL)), const((D_MODEL, 2 * LANES))],
        out_specs=[rows(D_MODEL), pl.BlockSpec((tm * ROW_TILES, LANES), lambda i: (i, 0)), rows(LANES), rows(LANES)],
        out_shape=[jax.ShapeDtypeStruct((t, D_MODEL), F32), jax.ShapeDtypeStruct((t * ROW_TILES, LANES), F32),
                   jax.ShapeDtypeStruct((t, LANES), jnp.int32), jax.ShapeDtypeStruct((t, LANES), F32)],
        compiler_params=_cparams("parallel"),
        name="mix_router",
    )(x, oa, ob, oc, wout, g, wr)


def _route(idx, tile):
    t = idx.shape[0]
    n_tiles = (2 * t) // tile
    n_items = n_tiles + N_EXPERTS - 1
    expert = jnp.concatenate([idx[:, 0], idx[:, 1]])
    onehot = (expert[:, None] == jnp.arange(N_EXPERTS, dtype=jnp.int32)[None, :]).astype(jnp.int32)
    csum = jnp.cumsum(onehot, axis=0)
    rank = jnp.take_along_axis(csum, expert[:, None], axis=1)[:, 0] - 1
    group_end = jnp.cumsum(csum[-1])
    group_start = group_end - csum[-1]
    pos = (group_start[expert] + rank).astype(jnp.int32)
    tile_lo = jnp.arange(n_tiles, dtype=jnp.int32)[:, None] * tile
    present = (group_start[None, :] < tile_lo + tile) & (group_end[None, :] > tile_lo)
    seen = jnp.cumsum(present.reshape(-1).astype(jnp.int32))
    k = jnp.arange(n_items, dtype=jnp.int32)
    flat = jnp.sum((seen[None, :] <= jnp.minimum(k, seen[-1] - 1)[:, None]).astype(jnp.int32), axis=1)
    item_tile, item_expert = flat // N_EXPERTS, flat % N_EXPERTS
    lo = jnp.clip(group_start[item_expert] - item_tile * tile, 0, tile)
    hi = jnp.where(k < seen[-1], jnp.clip(group_end[item_expert] - item_tile * tile, 0, tile), lo)
    first = jnp.concatenate([jnp.ones((1,), jnp.int32), (item_tile[1:] != item_tile[:-1]).astype(jnp.int32)])
    return pos, tuple(a.astype(jnp.int32) for a in (item_tile, item_expert, lo, hi, first))


def _moe_dispatch_kernel(pos_ref, h_ref, x_hbm, buf, sem, *, t):
    tm = h_ref.shape[0] // ROW_TILES
    i = pl.program_id(0)
    n = pl.num_programs(0)
    slot = i % 2

    def row_copy(step, sl, j, r):
        a = j * t + step * tm + r
        return pltpu.make_async_copy(buf.at[sl, pl.ds(pl.multiple_of(r * ROW_TILES, ROW_TILES), ROW_TILES), :],
                                     x_hbm.at[pl.ds(pl.multiple_of(pos_ref[a] * ROW_TILES, ROW_TILES), ROW_TILES), :],
                                     sem.at[sl])

    def each_row(fn):
        def body(r, carry):
            fn(0, r)
            fn(1, r)
            return carry
        lax.fori_loop(0, tm, body, 0, unroll=4)

    @pl.when(i >= 2)
    def _():
        each_row(lambda j, r: row_copy(i - 2, slot, j, r).wait())

    buf[slot] = h_ref[...]
    each_row(lambda j, r: row_copy(i, slot, j, r).start())

    @pl.when(i == n - 1)
    def _():
        @pl.when(n >= 2)
        def _():
            each_row(lambda j, r: row_copy(i - 1, 1 - slot, j, r).wait())
        each_row(lambda j, r: row_copy(i, slot, j, r).wait())


def _moe_dispatch(h_rows, pos, tm):
    a_n = pos.shape[0]
    t = a_n // 2
    grid_spec = pltpu.PrefetchScalarGridSpec(
        num_scalar_prefetch=1, grid=(t // tm,),
        in_specs=[pl.BlockSpec((tm * ROW_TILES, LANES), lambda i, p: (i, 0))],
        out_specs=pl.BlockSpec(memory_space=pl.ANY),
        scratch_shapes=[pltpu.VMEM((2, tm * ROW_TILES, LANES), F32), pltpu.SemaphoreType.DMA((2,))])
    return pl.pallas_call(
        functools.partial(_moe_dispatch_kernel, t=t),
        grid_spec=grid_spec,
        out_shape=jax.ShapeDtypeStruct((a_n * ROW_TILES, LANES), F32),
        compiler_params=_cparams("arbitrary"),
        name="moe_dispatch",
    )(pos, h_rows)


def _moe_expert_kernel(it_ref, ie_ref, lo_ref, hi_ref, first_ref, x_ref, wg_ref, wu_ref, wd_ref, y_ref, *, tile):
    del it_ref, ie_ref
    k = pl.program_id(0)
    x = jnp.concatenate([x_ref[pl.ds(s, tile, stride=ROW_TILES), :] for s in range(ROW_TILES)], axis=1)
    x = x.astype(BF16)
    act = _silu(_dot(x, wg_ref[0])) * _dot(x, wu_ref[0])
    y = _dot(act.astype(BF16), wd_ref[0])
    row = lax.broadcasted_iota(jnp.int32, (tile, LANES), 0)
    mine = (row >= lo_ref[k]) & (row < hi_ref[k])

    @pl.when(first_ref[k] == 1)
    def _():
        for s in range(ROW_TILES):
            y_ref[pl.ds(s, tile, stride=ROW_TILES), :] = jnp.where(mine, y[:, s * LANES:(s + 1) * LANES], 0.0)

    @pl.when(first_ref[k] != 1)
    def _():
        for s in range(ROW_TILES):
            rows = pl.ds(s, tile, stride=ROW_TILES)
            y_ref[rows, :] = jnp.where(mine, y[:, s * LANES:(s + 1) * LANES], y_ref[rows, :])


def _moe_experts(x_rows, items, wg, wu, wd, tile):
    n_items = items[0].shape[0]
    _, _, d_ff = wg.shape
    rows_spec = pl.BlockSpec((tile * ROW_TILES, LANES), lambda k, it, ie, lo, hi, fi: (it[k], 0))

    def w_spec(shape):
        return pl.BlockSpec((1,) + shape, lambda k, it, ie, lo, hi, fi: (ie[k], 0, 0))

    grid_spec = pltpu.PrefetchScalarGridSpec(
        num_scalar_prefetch=5, grid=(n_items,),
        in_specs=[rows_spec, w_spec((D_MODEL, d_ff)), w_spec((D_MODEL, d_ff)), w_spec((d_ff, D_MODEL))],
        out_specs=rows_spec)
    return pl.pallas_call(
        functools.partial(_moe_expert_kernel, tile=tile),
        grid_spec=grid_spec,
        out_shape=jax.ShapeDtypeStruct(x_rows.shape, F32),
        compiler_params=_cparams("arbitrary"),
        name="moe_experts",
    )(*items, x_rows, wg, wu, wd)


def _moe_combine_kernel(pos_ref, x2_ref, gate_ref, y_hbm, o_ref, ybuf, sem, *, t):
    tm = x2_ref.shape[0]
    i = pl.program_id(0)
    n = pl.num_programs(0)
    slot = i % 2

    def row_copy(step, sl, j, r):
        a = j * t + step * tm + r
        return pltpu.make_async_copy(y_hbm.at[pl.ds(pl.multiple_of(pos_ref[a] * ROW_TILES, ROW_TILES), ROW_TILES), :],
                                     ybuf.at[sl, j, pl.ds(pl.multiple_of(r * ROW_TILES, ROW_TILES), ROW_TILES), :],
                                     sem.at[sl])

    def each_row(fn):
        def body(r, carry):
            fn(0, r)
            fn(1, r)
            return carry
        lax.fori_loop(0, tm, body, 0, unroll=4)

    @pl.when(i == 0)
    def _():
        each_row(lambda j, r: row_copy(0, 0, j, r).start())

    @pl.when(i + 1 < n)
    def _():
        each_row(lambda j, r: row_copy(i + 1, 1 - slot, j, r).start())

    each_row(lambda j, r: row_copy(i, slot, j, r).wait())
    gates = gate_ref[...]
    g0, g1 = gates[:, 0:1], gates[:, 1:2]
    for s in range(ROW_TILES):
        lanes = slice(s * LANES, (s + 1) * LANES)
        rows = pl.ds(s, tm, stride=ROW_TILES)
        o_ref[:, lanes] = x2_ref[:, lanes] + g0 * ybuf[slot, 0, rows, :] + g1 * ybuf[slot, 1, rows, :]


def _moe_combine(x2, gates, y, pos, tm):
    t = x2.shape[0]
    grid_spec = pltpu.PrefetchScalarGridSpec(
        num_scalar_prefetch=1, grid=(t // tm,),
        in_specs=[pl.BlockSpec((tm, D_MODEL), lambda i, p: (i, 0)), pl.BlockSpec((tm, LANES), lambda i, p: (i, 0)),
                  pl.BlockSpec(memory_space=pl.ANY)],
        out_specs=pl.BlockSpec((tm, D_MODEL), lambda i, p: (i, 0)),
        scratch_shapes=[pltpu.VMEM((2, 2, tm * ROW_TILES, LANES), F32), pltpu.SemaphoreType.DMA((2,))])
    return pl.pallas_call(
        functools.partial(_moe_combine_kernel, t=t),
        grid_spec=grid_spec,
        out_shape=jax.ShapeDtypeStruct((t, D_MODEL), F32),
        compiler_params=_cparams("arbitrary"),
        name="moe_combine",
    )(pos, x2, gates, y)


def _moe(x2, h_rows, idx, gates, wg, wu, wd, tile, tm):
    tile = math.gcd(2 * x2.shape[0], tile)
    pos, items = _route(idx, tile)
    x_rows = _moe_dispatch(h_rows, pos, tm)
    y = _moe_experts(x_rows, items, wg, wu, wd, tile)
    return _moe_combine(x2, gates, y, pos, tm)


def _block_diag_ones(n, group):
    idx = np.arange(n) // group
    return jnp.asarray((idx[:, None] == idx[None, :]).astype(np.float32), dtype=BF16)


def _rearrange_w_in(w):
    offs = np.cumsum([0, 128, 128, 256, GLA_RANK, 256, 256, 256, 256, 256, 512, 512, 512])
    a_q, a_k, a_v, a_lr, a_g, b_q, b_f, b_i, b_g, c_q, c_k, c_v = [
        w[:, offs[n]:offs[n + 1]] for n in range(12)]
    lr = jnp.pad(a_lr, ((0, 0), (0, LANES - GLA_RANK)))
    return jnp.concatenate([c_v, a_q, a_k, a_v, a_g, b_q, b_f, b_i, b_g, c_q, c_k, lr], axis=1).astype(BF16)


def _tile_rows(seq, cap):
    t = cap
    while seq % t:
        t //= 2
    return t


def kernel(x_prompt, x_sample, state_gla, state_hgrn, cache_k_win, cache_v_win, norm_mix_g, w_in,
           gla_w_gate_up, gla_b_gate, gla_norm_g, hgrn_lb_logits, hgrn_norm_g, attn_q_norm_g,
           attn_k_norm_g, w_out, norm_ffn_g, ffn_w_gate, ffn_w_up, ffn_w_down, moe_w_router,
           moe_w_gate, moe_w_up, moe_w_down):
    depth = w_in.shape[0]
    bp, lp, _ = x_prompt.shape
    bs, ls, _ = x_sample.shape
    buf = cache_k_win.shape[2]
    width = ATT_HEADS * HEAD_DIM
    assert lp % (BAND * max(DILATIONS)) == 0 and lp % CHUNK == 0 and ls % CHUNK != 0 and ls == SUBLANES
    assert buf >= BAND * max(DILATIONS) and lp % min(buf, lp) == 0

    lb_w = jax.nn.softmax(hgrn_lb_logits.astype(F32), axis=0)
    lower_bounds = jnp.cumsum(lb_w, axis=0) - lb_w[0]

    ones_att = _block_diag_ones(LANES, HEAD_DIM)
    ones_gla = _block_diag_ones(GLA_HEADS * GLA_DV, GLA_DV)
    ones_hgrn = _block_diag_ones(HGRN_HEADS * HGRN_DV, HGRN_DV)
    tm_p = _tile_rows(bp * lp, 256)
    tm_s = _tile_rows(bs * ls, 256)
    tables_p = _rope_tables(jnp.arange(lp, dtype=jnp.int32))
    tables_s = _rope_tables(jnp.tile(PAST_LEN + jnp.arange(ls, dtype=jnp.int32), tm_s // ls))
    ck = jnp.transpose(cache_k_win, (0, 1, 3, 4, 2))
    cv = jnp.transpose(cache_v_win, (0, 1, 3, 4, 2))

    def sample_heads(a):
        return a.reshape(bs, ls, ATT_HEADS, HEAD_DIM)

    def sample_tail(a):
        a = jnp.transpose(sample_heads(a), (0, 2, 3, 1))
        return jnp.pad(a, ((0, 0), (0, 0), (0, 0), (LANES - ls, 0)))

    xp = x_prompt.reshape(bp * lp, D_MODEL)
    xs = x_sample.reshape(bs * ls, D_MODEL)
    gla_p, hgrn_p, kw_p, vw_p, gla_s, hgrn_s = [], [], [], [], [], []
    caches_s = None
    keep = min(buf, lp)

    for layer in range(depth):
        w_proj = _rearrange_w_in(w_in[layer])
        g_mix = norm_mix_g[layer].reshape(1, D_MODEL)
        wup = jnp.pad(gla_w_gate_up[layer], ((0, LANES - GLA_RANK), (0, 0))).astype(BF16)
        bias = gla_b_gate[layer].reshape(1, -1)
        gain_a = jnp.tile(gla_norm_g[layer], GLA_HEADS).reshape(1, -1)
        gain_b = jnp.tile(hgrn_norm_g[layer], HGRN_HEADS).reshape(1, -1)
        lb = lower_bounds[layer].reshape(1, -1)
        gq = jnp.tile(attn_q_norm_g[layer], 2).reshape(1, LANES)
        gk = jnp.tile(attn_k_norm_g[layer], 2).reshape(1, LANES)
        wout = w_out[layer].astype(BF16)
        g_ffn = norm_ffn_g[layer].reshape(1, D_MODEL)

        mixed = []
        for which, x, batch, seq, tm in (("p", xp, bp, lp, tm_p), ("s", xs, bs, ls, tm_s)):
            proj = _inproj(x, g_mix, w_proj, _tile_rows(x.shape[0], 2 * tm))
            if which == "p":
                qn, kn, k_win, v_win = _qkrope_window(proj, gq, gk, tables_p, ones_att, batch, seq, keep)
                s0_a, s0_b, block, chunk, seqs = None, None, min(seq, 4 * CHUNK), CHUNK, 1
            else:
                qn, kn = _qkrope(proj, gq, gk, tables_s, ones_att, tm, 1)
                s0_a, s0_b, block, chunk, seqs = state_gla[layer], state_hgrn[layer], seq, seq, math.gcd(batch, SAMPLE_SEQS)
            o_a, s_a = _gated_linear("gla", proj, batch, seq, s0_a, gain_a, ones_gla, (wup, bias), block, chunk, seqs)
            o_b, s_b = _gated_linear("hgrn", proj, batch, seq, s0_b, gain_b, ones_hgrn, (lb,), block, chunk, seqs)
            if which == "p":
                o_c = _attn_prompt(qn, kn, proj, COL_CV, batch, seq)
                gla_p.append(s_a)
                hgrn_p.append(s_b)
                kw_p.append(k_win)
                vw_p.append(v_win)
            else:
                o_c, ck_new, cv_new = _attn_sample(
                    jnp.transpose(sample_heads(qn), (0, 2, 1, 3)), sample_tail(kn),
                    sample_tail(proj[:, COL_CV:COL_CV + width]), ck, cv, layer, caches_s, seq, 2)
                o_c = jnp.transpose(o_c, (0, 2, 1, 3)).reshape(batch * seq, width)
                caches_s = (ck_new, cv_new)
                gla_s.append(s_a)
                hgrn_s.append(s_b)
            mixed.append((o_a, o_b, o_c))

        j = layer // 2
        new_x = []
        for (o_a, o_b, o_c), x, tm in ((mixed[0], xp, tm_p), (mixed[1], xs, tm_s)):
            if layer % 2 == 0:
                new_x.append(_mix_ffn(x, o_a, o_b, o_c, wout, g_ffn, ffn_w_gate[j].astype(BF16),
                                      ffn_w_up[j].astype(BF16), ffn_w_down[j].astype(BF16),
                                      _tile_rows(x.shape[0], 2 * tm)))
            else:
                wr = jnp.pad(moe_w_router[j], ((0, 0), (0, LANES - N_EXPERTS)))
                wr_hi = wr.astype(BF16)
                wr = jnp.concatenate([wr_hi, (wr - wr_hi.astype(F32)).astype(BF16)], axis=1)
                x2, h_rows, idx, gates = _mix_router(x, o_a, o_b, o_c, wout, g_ffn, wr, tm)
                new_x.append(_moe(x2, h_rows, idx, gates, moe_w_gate[j].astype(BF16), moe_w_up[j].astype(BF16),
                                  moe_w_down[j].astype(BF16), MOE_TILE, tm))
        xp, xs = new_x

    return (xp.reshape(bp, lp, D_MODEL), xs.reshape(bs, ls, D_MODEL),
            jnp.stack(gla_p), jnp.stack(hgrn_p),
            jnp.transpose(jnp.stack(kw_p), (0, 1, 4, 2, 3)), jnp.transpose(jnp.stack(vw_p), (0, 1, 4, 2, 3)),
            jnp.stack(gla_s), jnp.stack(hgrn_s),
            jnp.transpose(caches_s[0], (0, 1, 4, 2, 3)), jnp.transpose(caches_s[1], (0, 1, 4, 2, 3)))
```
